```python
import jax, jax.numpy as jnp
from jax import lax
import numpy as np

D_MODEL = 2048
BATCH = 4
SEQ = 2048
DEPTH = 2
DEC_BATCH = 32
DEC_SEQ = 4
PAST_LEN = 8192
PAGE_SIZE = 128

HEAD_DIM = 128
MIX_WIDTH = D_MODEL
N_MEM = 256
MEM_HEADS = 4
MEM_WIDTH = MEM_HEADS * HEAD_DIM
CONV_CH = MIX_WIDTH - MEM_WIDTH
CONV_WIDTH = 31
FOX_HEADS = (MIX_WIDTH - MEM_WIDTH) // HEAD_DIM
FOX_WIDTH = FOX_HEADS * HEAD_DIM
D_FF = 256 * ((8 * D_MODEL // 3 + 255) // 256)
N_EXPERTS = 8
TOP_K = 2
D_FF_EXPERT = D_FF
Q_BLOCK = 128
N_A = DEPTH // 2
N_B = DEPTH - N_A
N_DENSE = (DEPTH + 1) // 2
N_MOE = DEPTH // 2
EPS = 1e-6
NEG_INF = -1e30
ATTN_SCALE = HEAD_DIM ** -0.5

kernel_name = 'yoco_conformer_fox_memory_moe_step'


def rmsnorm(x, g):
    xf = x.astype(jnp.float32)
    y = xf * lax.rsqrt(jnp.mean(xf * xf, axis=-1, keepdims=True) + EPS)
    return (y * g.astype(jnp.float32)).astype(x.dtype)


def layernorm(x, g, b):
    xf = x.astype(jnp.float32)
    mu = jnp.mean(xf, axis=-1, keepdims=True)
    var = jnp.mean(jnp.square(xf - mu), axis=-1, keepdims=True)
    return ((xf - mu) * lax.rsqrt(var + EPS) * g.astype(jnp.float32) + b.astype(jnp.float32)).astype(x.dtype)


def swiglu(x, w_gate, w_up, w_down):
    return (jax.nn.silu(x @ w_gate) * (x @ w_up)) @ w_down


def moe(x, w_router, b_router, w_gate, w_up, w_down):
    logits = (x @ w_router).astype(jnp.float32) + b_router.astype(jnp.float32)
    top_v, top_i = lax.top_k(logits, TOP_K)
    probs = jax.nn.softmax(top_v, axis=-1)
    gates = jnp.sum(jax.nn.one_hot(top_i, N_EXPERTS, dtype=jnp.float32) * probs[..., None], axis=-2)
    y = jnp.zeros_like(x)
    for e in range(N_EXPERTS):
        y = y + gates[..., e:e + 1].astype(x.dtype) * swiglu(x, w_gate[e], w_up[e], w_down[e])
    return y


def conv_module(u_ext, w, b, ln_g, ln_b):
    h = lax.conv_general_dilated(u_ext, w[:, None, :].astype(u_ext.dtype), (1,), 'VALID',
                                 dimension_numbers=('NWC', 'WIO', 'NWC'),
                                 feature_group_count=u_ext.shape[-1]) + b
    return jax.nn.silu(layernorm(h, ln_g, ln_b))


def mem_kv(mem, g, w):
    n, m = mem.shape[:2]
    z = rmsnorm(mem, g) @ w
    return (z[..., :MEM_WIDTH].reshape(n, m, MEM_HEADS, HEAD_DIM),
            z[..., MEM_WIDTH:].reshape(n, m, MEM_HEADS, HEAD_DIM))


def mem_attend(q, k, v):
    n, t = q.shape[:2]
    s = jnp.einsum('nthd,nmhd->nhtm', q, k, preferred_element_type=jnp.float32) * ATTN_SCALE
    p = jax.nn.softmax(s, axis=-1).astype(v.dtype)
    return jnp.einsum('nhtm,nmhd->nthd', p, v).reshape(n, t, MEM_WIDTH)


def shared_kv(h, g_kv, w_kvf, b_f):
    n, t = h.shape[:2]
    z = rmsnorm(h, g_kv) @ w_kvf
    k = z[..., :FOX_WIDTH].reshape(n, t, FOX_HEADS, HEAD_DIM)
    v = z[..., FOX_WIDTH:2 * FOX_WIDTH].reshape(n, t, FOX_HEADS, HEAD_DIM)
    logf = jax.nn.log_sigmoid(z[..., 2 * FOX_WIDTH:].astype(jnp.float32) + b_f.astype(jnp.float32))
    return k, v, logf


def fox_prompt(q, k, v, logf):
    bsz, seq, n_h, hd = q.shape
    n_blk = seq // Q_BLOCK
    c = jnp.cumsum(logf, axis=1).transpose(0, 2, 1)
    q_blocks = q.reshape(bsz, n_blk, Q_BLOCK, n_h, hd).transpose(1, 0, 2, 3, 4)
    c_blocks = c.reshape(bsz, n_h, n_blk, Q_BLOCK).transpose(2, 0, 1, 3)
    key_pos = jnp.arange(seq)

    def one_block(args):
        blk, q_b, c_b = args
        query_pos = blk * Q_BLOCK + jnp.arange(Q_BLOCK)
        s = jnp.einsum('bqhd,bkhd->bhqk', q_b, k, preferred_element_type=jnp.float32) * ATTN_SCALE
        s = s + (c_b[..., :, None] - c[..., None, :])
        s = jnp.where(key_pos[None, :] <= query_pos[:, None], s, NEG_INF)
        p = jax.nn.softmax(s, axis=-1)
        return jnp.einsum('bhqk,bkhd->bqhd', p.astype(v.dtype), v)

    o = lax.map(one_block, (jnp.arange(n_blk), q_blocks, c_blocks))
    return o.transpose(1, 0, 2, 3, 4).reshape(bsz, seq, n_h * hd)


def fox_sample(q, k_new, v_new, logf_new, k_past, v_past, logf_past):
    n, t = q.shape[:2]
    past = k_past.shape[1]
    c_new = jnp.cumsum(logf_new, axis=1).transpose(0, 2, 1)
    d_past = (lax.cumsum(logf_past, axis=1, reverse=True) - logf_past).transpose(0, 2, 1)
    s_past = jnp.einsum('nthd,nphd->nhtp', q, k_past, preferred_element_type=jnp.float32) * ATTN_SCALE
    s_past = s_past + c_new[..., :, None] + d_past[..., None, :]
    s_new = jnp.einsum('nthd,nshd->nhts', q, k_new, preferred_element_type=jnp.float32) * ATTN_SCALE
    s_new = s_new + (c_new[..., :, None] - c_new[..., None, :])
    s_new = jnp.where(jnp.tril(jnp.ones((t, t), dtype=bool)), s_new, NEG_INF)
    p = jax.nn.softmax(jnp.concatenate([s_past, s_new], axis=-1), axis=-1)
    o = (jnp.einsum('nhtp,nphd->nthd', p[..., :past].astype(v_past.dtype), v_past)
         + jnp.einsum('nhts,nshd->nthd', p[..., past:].astype(v_new.dtype), v_new))
    return o.reshape(n, t, FOX_WIDTH)


def channel_mixer(i, h, p):
    hf = rmsnorm(h, p['g_ffn'][i])
    j = i // 2
    if i % 2 == 0:
        return h + swiglu(hf, p['w_ff_gate'][j], p['w_ff_up'][j], p['w_ff_down'][j])
    return h + moe(hf, p['w_router'][j], p['b_router'][j], p['w_e_gate'][j], p['w_e_up'][j], p['w_e_down'][j])


def trunk(h, p, mem_k, mem_v, conv_ctx, fox_attend):
    n, t = h.shape[:2]
    new_conv = []
    kv = None
    for i in range(DEPTH):
        hn = rmsnorm(h, p['g_mix'][i])
        if i < N_A:
            z = hn @ p['w_in_a'][i]
            u = z[..., :CONV_CH] * jax.nn.sigmoid(z[..., CONV_CH:2 * CONV_CH])
            q_mem = z[..., 2 * CONV_CH:]
            u_ext = jnp.concatenate([conv_ctx[i].astype(u.dtype), u], axis=1)
            new_conv.append(u_ext[:, -(CONV_WIDTH - 1):])
            mix = conv_module(u_ext, p['conv_w'][i], p['conv_b'][i], p['conv_ln_g'][i], p['conv_ln_b'][i])
            w_out = p['w_out_a'][i]
        else:
            j = i - N_A
            if kv is None:
                kv = shared_kv(h, p['g_kv'], p['w_kvf'], p['b_f'])
            k, v, logf = kv
            z = hn @ p['w_in_b'][j]
            q = z[..., :FOX_WIDTH].reshape(n, t, FOX_HEADS, HEAD_DIM)
            q_mem = z[..., FOX_WIDTH:]
            mix = fox_attend(q, k, v, logf)
            w_out = p['w_out_b'][j]
        mo = mem_attend(q_mem.reshape(n, t, MEM_HEADS, HEAD_DIM), mem_k[i], mem_v[i])
        h = h + jnp.concatenate([mix, mo], axis=-1) @ w_out
        h = channel_mixer(i, h, p)
    return rmsnorm(h, p['g_final']), kv, new_conv


def setup_inputs(seed: int = 0) -> dict:
    key = jax.random.key(seed)
    ks = iter(jax.random.split(key, 48))

    def nrm(shape, scale):
        return jax.random.normal(next(ks), shape, jnp.float32) * scale

    n_pages = PAST_LEN // PAGE_SIZE
    n_used = DEC_BATCH * n_pages
    n_pool = (5 * n_used + 3) // 4
    page_table = jax.random.permutation(next(ks), n_pool)[:n_used].reshape(DEC_BATCH, n_pages).astype(jnp.int32)
    b_f = jnp.linspace(3.0, 6.0, FOX_HEADS, dtype=jnp.float32) + nrm((FOX_HEADS,), 0.01)
    return {
        'x_prompt': nrm((BATCH, SEQ, D_MODEL), 1.0),
        'x_sample': nrm((DEC_BATCH, DEC_SEQ, D_MODEL), 1.0),
        'cache_fox_k': nrm((n_pool, PAGE_SIZE, FOX_HEADS, HEAD_DIM), 1.0),
        'cache_fox_v': nrm((n_pool, PAGE_SIZE, FOX_HEADS, HEAD_DIM), 1.0),
        'cache_fox_logf': jax.nn.log_sigmoid(b_f + nrm((n_pool, PAGE_SIZE, FOX_HEADS), 1.0)),
        'state_conv': nrm((N_A, DEC_BATCH, CONV_WIDTH - 1, CONV_CH), 0.5),
        'cache_mem_k': nrm((DEPTH, DEC_BATCH, N_MEM, MEM_HEADS, HEAD_DIM), 1.0),
        'cache_mem_v': nrm((DEPTH, DEC_BATCH, N_MEM, MEM_HEADS, HEAD_DIM), 1.0),
        'page_table': page_table,
        'mem_prompt': nrm((BATCH, N_MEM, D_MODEL), 1.0),
        'g_mix': 1.0 + nrm((DEPTH, D_MODEL), 0.02),
        'g_ffn': 1.0 + nrm((DEPTH, D_MODEL), 0.02),
        'g_mem': 1.0 + nrm((DEPTH, D_MODEL), 0.02),
        'w_mem_kv': nrm((DEPTH, D_MODEL, 2 * MEM_WIDTH), D_MODEL ** -0.5),
        'w_in_a': nrm((N_A, D_MODEL, 2 * CONV_CH + MEM_WIDTH), D_MODEL ** -0.5),
        'w_out_a': nrm((N_A, CONV_CH + MEM_WIDTH, D_MODEL), (CONV_CH + MEM_WIDTH) ** -0.5),
        'conv_w': nrm((N_A, CONV_WIDTH, CONV_CH), CONV_WIDTH ** -0.5),
        'conv_b': nrm((N_A, CONV_CH), 0.02),
        'conv_ln_g': 1.0 + nrm((N_A, CONV_CH), 0.02),
        'conv_ln_b': nrm((N_A, CONV_CH), 0.02),
        'g_kv': 1.0 + nrm((D_MODEL,), 0.02),
        'w_kvf': nrm((D_MODEL, 2 * FOX_WIDTH + FOX_HEADS), D_MODEL ** -0.5),
        'b_f': b_f,
        'w_in_b': nrm((N_B, D_MODEL, FOX_WIDTH + MEM_WIDTH), D_MODEL ** -0.5),
        'w_out_b': nrm((N_B, FOX_WIDTH + MEM_WIDTH, D_MODEL), (FOX_WIDTH + MEM_WIDTH) ** -0.5),
        'w_ff_gate': nrm((N_DENSE, D_MODEL, D_FF), D_MODEL ** -0.5),
        'w_ff_up': nrm((N_DENSE, D_MODEL, D_FF), D_MODEL ** -0.5),
        'w_ff_down': nrm((N_DENSE, D_FF, D_MODEL), D_FF ** -0.5),
        'w_router': nrm((N_MOE, D_MODEL, N_EXPERTS), D_MODEL ** -0.5),
        'b_router': nrm((N_MOE, N_EXPERTS), 0.01),
        'w_e_gate': nrm((N_MOE, N_EXPERTS, D_MODEL, D_FF_EXPERT), D_MODEL ** -0.5),
        'w_e_up': nrm((N_MOE, N_EXPERTS, D_MODEL, D_FF_EXPERT), D_MODEL ** -0.5),
        'w_e_down': nrm((N_MOE, N_EXPERTS, D_FF_EXPERT, D_MODEL), D_FF_EXPERT ** -0.5),
        'g_final': 1.0 + nrm((D_MODEL,), 0.02),
    }


def reference(x_prompt, x_sample, cache_fox_k, cache_fox_v, cache_fox_logf, state_conv,
              cache_mem_k, cache_mem_v, page_table, mem_prompt,
              g_mix, g_ffn, g_mem, w_mem_kv, w_in_a, w_out_a, conv_w, conv_b, conv_ln_g, conv_ln_b,
              g_kv, w_kvf, b_f, w_in_b, w_out_b, w_ff_gate, w_ff_up, w_ff_down,
              w_router, b_router, w_e_gate, w_e_up, w_e_down, g_final):
    p = dict(g_mix=g_mix, g_ffn=g_ffn, w_in_a=w_in_a, w_out_a=w_out_a, conv_w=conv_w, conv_b=conv_b,
             conv_ln_g=conv_ln_g, conv_ln_b=conv_ln_b, g_kv=g_kv, w_kvf=w_kvf, b_f=b_f,
             w_in_b=w_in_b, w_out_b=w_out_b, w_ff_gate=w_ff_gate, w_ff_up=w_ff_up, w_ff_down=w_ff_down,
             w_router=w_router, b_router=b_router, w_e_gate=w_e_gate, w_e_up=w_e_up, w_e_down=w_e_down,
             g_final=g_final)

    mem_kv_p = [mem_kv(mem_prompt, g_mem[i], w_mem_kv[i]) for i in range(DEPTH)]
    mem_k_p = [kv[0] for kv in mem_kv_p]
    mem_v_p = [kv[1] for kv in mem_kv_p]
    zero_ctx = [jnp.zeros((x_prompt.shape[0], CONV_WIDTH - 1, CONV_CH), x_prompt.dtype) for _ in range(N_A)]
    y_prompt, kv_p, conv_p = trunk(x_prompt, p, mem_k_p, mem_v_p, zero_ctx, fox_prompt)
    k_p, v_p, logf_p = kv_p

    n_dec, n_pages = page_table.shape
    past = n_pages * cache_fox_k.shape[1]
    k_past = cache_fox_k[page_table].reshape(n_dec, past, FOX_HEADS, HEAD_DIM)
    v_past = cache_fox_v[page_table].reshape(n_dec, past, FOX_HEADS, HEAD_DIM)
    logf_past = cache_fox_logf[page_table].reshape(n_dec, past, FOX_HEADS).astype(jnp.float32)

    def fox_with_past(q, k, v, logf):
        return fox_sample(q, k, v, logf, k_past, v_past, logf_past)

    y_sample, kv_s, conv_s = trunk(x_sample, p,
                                   [cache_mem_k[i] for i in range(DEPTH)],
                                   [cache_mem_v[i] for i in range(DEPTH)],
                                   [state_conv[a] for a in range(N_A)], fox_with_past)
    k_s, v_s, logf_s = kv_s
    return (y_prompt, y_sample, k_p, v_p, logf_p, jnp.stack(conv_p), jnp.stack(mem_k_p), jnp.stack(mem_v_p),
            k_s, v_s, logf_s, jnp.stack(conv_s))
```

```python
import functools

import jax
import jax.numpy as jnp
from jax import lax
from jax.experimental import pallas as pl
from jax.experimental.pallas import tpu as pltpu

HEAD_DIM = 128
MEM_HEADS = 4
MEM_WIDTH = MEM_HEADS * HEAD_DIM
TOP_K = 2
EPS = 1e-6
NEG_INF = -1e30
ATTN_SCALE = HEAD_DIM ** -0.5
LANES = 128
HEAD_PAD = 16
CONV_HALO = 32
VMEM_LIMIT_BYTES = 56 * 1024 * 1024

BF16 = jnp.bfloat16
F32 = jnp.float32


def _cparams(*semantics):
    return pltpu.CompilerParams(dimension_semantics=semantics, vmem_limit_bytes=VMEM_LIMIT_BYTES)


def _pick(n, candidates):
    for c in candidates:
        if c <= n and n % c == 0:
            return c
    return n


def _row_tile(t):
    return _pick(t, (1040, 1024, 640, 512, 256, 128, 64, 32, 16))


def _dot(a, b):
    return jnp.dot(a, b, preferred_element_type=F32)


def _dot_nt(a, b):
    return lax.dot_general(a, b, (((1,), (1,)), ((), ())), preferred_element_type=F32)


def _dot_exact(a, b):
    return jnp.dot(a, b, preferred_element_type=F32, precision=lax.Precision.HIGHEST)


def _log_sigmoid(x):
    return jnp.minimum(x, 0.0) - jnp.log1p(jnp.exp(-jnp.abs(x)))


def _rmsnorm_kernel(x_ref, g_ref, o_ref):
    x = x_ref[...]
    y = x * lax.rsqrt(jnp.mean(x * x, axis=-1, keepdims=True) + EPS)
    o_ref[...] = (y * g_ref[...]).astype(o_ref.dtype)


def _rmsnorm(x, g, out_dtype):
    t, d = x.shape
    tm = _pick(t, (640, 512, 256, 128, 64, 32, 16))
    return pl.pallas_call(
        _rmsnorm_kernel,
        grid=(t // tm,),
        in_specs=[pl.BlockSpec((tm, d), lambda i: (i, 0)),
                  pl.BlockSpec((1, d), lambda i: (0, 0))],
        out_specs=pl.BlockSpec((tm, d), lambda i: (i, 0)),
        out_shape=jax.ShapeDtypeStruct((t, d), out_dtype),
        compiler_params=_cparams("parallel"),
        name="rmsnorm",
    )(x, g.reshape(1, d))


def _mm_kernel(*refs, n_a, epilogue):
    a_refs = refs[:n_a]
    w_refs = refs[n_a:2 * n_a]
    pos = 2 * n_a
    extra_ref = None
    if epilogue in ("resid", "logsig"):
        extra_ref = refs[pos]
        pos += 1
    o_ref = refs[pos]
    wb_refs = refs[pos + 1:pos + 1 + n_a]

    @pl.when(pl.program_id(1) == 0)
    def _():
        for w_ref, wb_ref in zip(w_refs, wb_refs):
            wb_ref[...] = w_ref[...].astype(BF16)

    acc = None
    for a_ref, wb_ref in zip(a_refs, wb_refs):
        d = _dot(a_ref[...], wb_ref[...])
        acc = d if acc is None else acc + d
    if epilogue == "resid":
        acc = extra_ref[...] + acc
    elif epilogue == "logsig":
        acc = _log_sigmoid(acc + extra_ref[...])
    o_ref[...] = acc.astype(o_ref.dtype)


def _matmul(a_list, w, row_offs, col_off, n_out, *, epilogue="none", extra=None, out_dtype=F32):
    t = a_list[0].shape[0]
    tm = _row_tile(t)
    tn = _pick(n_out, (512, 256, 128))
    n_a = len(a_list)
    in_specs, scratch = [], []
    for a in a_list:
        in_specs.append(pl.BlockSpec((tm, a.shape[1]), lambda n, m: (m, 0)))
    for a, ro in zip(a_list, row_offs):
        k = a.shape[1]
        assert ro % k == 0 and col_off % tn == 0
        in_specs.append(pl.BlockSpec((k, tn), functools.partial(
            lambda n, m, rb, cb: (rb, cb + n), rb=ro // k, cb=col_off // tn)))
        scratch.append(pltpu.VMEM((k, tn), BF16))
    args = list(a_list) + [w] * n_a
    if epilogue == "resid":
        in_specs.append(pl.BlockSpec((tm, tn), lambda n, m: (m, n)))
        args.append(extra)
    elif epilogue == "logsig":
        in_specs.append(pl.BlockSpec((1, tn), lambda n, m: (0, n)))
        args.append(extra)
    return pl.pallas_call(
        functools.partial(_mm_kernel, n_a=n_a, epilogue=epilogue),
        grid=(n_out // tn, t // tm),
        in_specs=in_specs,
        out_specs=pl.BlockSpec((tm, tn), lambda n, m: (m, n)),
        out_shape=jax.ShapeDtypeStruct((t, n_out), out_dtype),
        scratch_shapes=scratch,
        compiler_params=_cparams("arbitrary", "arbitrary"),
        name="matmul_" + epilogue,
    )(*args)


def _glu_kernel(a_ref, w1_ref, w2_ref, o_ref, wb1_ref, wb2_ref):
    @pl.when(pl.program_id(1) == 0)
    def _():
        wb1_ref[...] = w1_ref[...].astype(BF16)
        wb2_ref[...] = w2_ref[...].astype(BF16)

    a = a_ref[...]
    o_ref[...] = _dot(a, wb1_ref[...]) * jax.nn.sigmoid(_dot(a, wb2_ref[...]))


def _matmul_glu(a, w, n_out):
    t, k = a.shape
    tm = _row_tile(t)
    tn = _pick(n_out, (512, 256, 128))
    nb = n_out // tn
    return pl.pallas_call(
        _glu_kernel,
        grid=(nb, t // tm),
        in_specs=[pl.BlockSpec((tm, k), lambda n, m: (m, 0)),
                  pl.BlockSpec((k, tn), lambda n, m: (0, n)),
                  pl.BlockSpec((k, tn), lambda n, m: (0, n + nb))],
        out_specs=pl.BlockSpec((tm, tn), lambda n, m: (m, n)),
        out_shape=jax.ShapeDtypeStruct((t, n_out), F32),
        scratch_shapes=[pltpu.VMEM((k, tn), BF16), pltpu.VMEM((k, tn), BF16)],
        compiler_params=_cparams("arbitrary", "arbitrary"),
        name="matmul_glu",
    )(a, w, w)


def _ffn_kernel(*refs, gated):
    if gated:
        x_ref, h_ref, gate_ref, wg_ref, wu_ref, wd_ref, o_ref = refs
    else:
        x_ref, h_ref, wg_ref, wu_ref, wd_ref, o_ref = refs
    e = pl.program_id(1)
    f = pl.program_id(2)

    @pl.when(jnp.logical_and(e == 0, f == 0))
    def _():
        o_ref[...] = h_ref[...]

    x = x_ref[...]
    g = _dot(x, wg_ref[0].astype(BF16))
    u = _dot(x, wu_ref[0].astype(BF16))
    act = g * jax.nn.sigmoid(g) * u
    if gated:
        gates = gate_ref[...]
        lane = lax.broadcasted_iota(jnp.int32, gates.shape, 1)
        act = act * jnp.sum(jnp.where(lane == e, gates, 0.0), axis=-1, keepdims=True)
    o_ref[...] += _dot(act.astype(BF16), wd_ref[0].astype(BF16))


def _ffn(x, h, w_gate, w_up, w_down, gates=None):
    t, d = x.shape
    n_e, _, f = w_gate.shape
    tm = _row_tile(t)
    tf = _pick(f, (256, 128))
    single = pl.Buffered(1)
    in_specs = [pl.BlockSpec((tm, d), lambda m, e, j: (m, 0), pipeline_mode=single),
                pl.BlockSpec((tm, d), lambda m, e, j: (m, 0), pipeline_mode=single)]
    args = [x, h]
    if gates is not None:
        in_specs.append(pl.BlockSpec((tm, LANES), lambda m, e, j: (m, 0)))
        args.append(gates)
    in_specs += [pl.BlockSpec((1, d, tf), lambda m, e, j: (e, 0, j)),
                 pl.BlockSpec((1, d, tf), lambda m, e, j: (e, 0, j)),
                 pl.BlockSpec((1, tf, d), lambda m, e, j: (e, j, 0))]
    args += [w_gate, w_up, w_down]
    return pl.pallas_call(
        functools.partial(_ffn_kernel, gated=gates is not None),
        grid=(t // tm, n_e, f // tf),
        in_specs=in_specs,
        out_specs=pl.BlockSpec((tm, d), lambda m, e, j: (m, 0)),
        out_shape=jax.ShapeDtypeStruct((t, d), F32),
        compiler_params=_cparams("parallel", "arbitrary", "arbitrary"),
        name="ffn_gated" if gates is not None else "ffn",
    )(*args)


def _router_kernel(x_ref, w_ref, b_ref, o_ref, *, n_experts):
    logits = _dot(x_ref[...], w_ref[...].astype(BF16)) + b_ref[...]
    lane = lax.broadcasted_iota(jnp.int32, logits.shape, 1)
    logits = jnp.where(lane < n_experts, logits, NEG_INF)
    v1 = jnp.max(logits, axis=-1, keepdims=True)
    i1 = jnp.min(jnp.where(logits == v1, lane, LANES), axis=-1, keepdims=True)
    rest = jnp.where(lane == i1, NEG_INF, logits)
    v2 = jnp.max(rest, axis=-1, keepdims=True)
    i2 = jnp.min(jnp.where(rest == v2, lane, LANES), axis=-1, keepdims=True)
    e2 = jnp.exp(v2 - v1)
    p1 = 1.0 / (1.0 + e2)
    p2 = e2 / (1.0 + e2)
    o_ref[...] = jnp.where(lane == i1, p1, jnp.where(lane == i2, p2, 0.0))


def _router(x, w_router, b_router):
    t, d = x.shape
    n_e = w_router.shape[-1]
    tm = _row_tile(t)
    w_pad = jnp.pad(w_router, ((0, 0), (0, LANES - n_e)))
    b_pad = jnp.pad(b_router, (0, LANES - n_e)).reshape(1, LANES)
    return pl.pallas_call(
        functools.partial(_router_kernel, n_experts=n_e),
        grid=(t // tm,),
        in_specs=[pl.BlockSpec((tm, d), lambda m: (m, 0)),
                  pl.BlockSpec((d, LANES), lambda m: (0, 0)),
                  pl.BlockSpec((1, LANES), lambda m: (0, 0))],
        out_specs=pl.BlockSpec((tm, LANES), lambda m: (m, 0)),
        out_shape=jax.ShapeDtypeStruct((t, LANES), F32),
        compiler_params=_cparams("parallel"),
        name="router",
    )(x, w_pad, b_pad)


def _ln_swish(hc, lg, lb):
    mu = jnp.mean(hc, axis=-1, keepdims=True)
    xc = hc - mu
    var = jnp.mean(xc * xc, axis=-1, keepdims=True)
    y = xc * lax.rsqrt(var + EPS) * lg + lb
    return y * jax.nn.sigmoid(y)


def _conv_kernel(main_ref, nxt_ref, w_ref, b_ref, lg_ref, lb_ref, o_ref, ext_ref, acc_ref, *, tt, width):
    ext_ref[0:tt, :] = main_ref[0]
    ext_ref[tt:tt + CONV_HALO, :] = nxt_ref[0]
    n_chunks = o_ref.shape[-1] // LANES

    def chunk(c, carry):
        c0 = pl.multiple_of(c * LANES, LANES)
        acc = jnp.zeros((tt, LANES), F32)
        for j in range(width):
            acc = acc + ext_ref[pl.ds(j, tt), pl.ds(c0, LANES)] * w_ref[pl.ds(j, 1), pl.ds(c0, LANES)]
        acc_ref[:, pl.ds(c0, LANES)] = acc + b_ref[:, pl.ds(c0, LANES)]
        return carry

    lax.fori_loop(0, n_chunks, chunk, 0)
    o_ref[0] = _ln_swish(acc_ref[...], lg_ref[...], lb_ref[...])


def _conv_prompt(u_ext, w, b, lg, lb, seq):
    bsz, _, c = u_ext.shape
    width = w.shape[0]
    tt = _pick(seq, (256, 128, 64, 32))
    return pl.pallas_call(
        functools.partial(_conv_kernel, tt=tt, width=width),
        grid=(bsz, seq // tt),
        in_specs=[pl.BlockSpec((1, tt, c), lambda bb, i: (bb, i, 0)),
                  pl.BlockSpec((1, CONV_HALO, c), lambda bb, i: (bb, (i + 1) * (tt // CONV_HALO), 0)),
                  pl.BlockSpec((width, c), lambda bb, i: (0, 0)),
                  pl.BlockSpec((1, c), lambda bb, i: (0, 0)),
                  pl.BlockSpec((1, c), lambda bb, i: (0, 0)),
                  pl.BlockSpec((1, c), lambda bb, i: (0, 0))],
        out_specs=pl.BlockSpec((1, tt, c), lambda bb, i: (bb, i, 0)),
        out_shape=jax.ShapeDtypeStruct((bsz, seq, c), F32),
        scratch_shapes=[pltpu.VMEM((tt + CONV_HALO, c), F32), pltpu.VMEM((tt, c), F32)],
        compiler_params=_cparams("parallel", "parallel"),
        name="conv_prompt",
    )(u_ext, u_ext, w, b.reshape(1, c), lg.reshape(1, c), lb.reshape(1, c))


def _conv_step_kernel(ext_ref, w_ref, b_ref, lg_ref, lb_ref, o_ref, *, width):
    n_t = o_ref.shape[0]
    for t in range(n_t):
        acc = ext_ref[t] * w_ref[0:1, :]
        for j in range(1, width):
            acc = acc + ext_ref[t + j] * w_ref[j:j + 1, :]
        o_ref[t] = _ln_swish(acc + b_ref[...], lg_ref[...], lb_ref[...])


def _conv_sample(ext_t, w, b, lg, lb):
    l_ext, n, c = ext_t.shape
    width = w.shape[0]
    n_t = l_ext - (width - 1)
    return pl.pallas_call(
        functools.partial(_conv_step_kernel, width=width),
        out_shape=jax.ShapeDtypeStruct((n_t, n, c), F32),
        compiler_params=pltpu.CompilerParams(vmem_limit_bytes=VMEM_LIMIT_BYTES),
        name="conv_sample",
    )(ext_t, w, b.reshape(1, c), lg.reshape(1, c), lb.reshape(1, c))


def _mem_attn_kernel(q_ref, k_ref, v_ref, o_ref):
    for hh in range(MEM_HEADS):
        sl = slice(hh * HEAD_DIM, (hh + 1) * HEAD_DIM)
        q = q_ref[0, :, sl].astype(BF16)
        k = k_ref[0, :, sl].astype(BF16)
        v = v_ref[0, :, sl].astype(BF16)
        s = _dot_nt(q, k) * ATTN_SCALE
        m = jnp.max(s, axis=-1, keepdims=True)
        p = jnp.exp(s - m)
        l = jnp.sum(p, axis=-1, keepdims=True)
        o_ref[0, :, sl] = _dot(p.astype(BF16), v) / l


def _mem_attn(q, k, v):
    n, t, w = q.shape
    m = k.shape[1]
    tq = _pick(t, (512, 256, 128, 64, 32, 16, 8))
    return pl.pallas_call(
        _mem_attn_kernel,
        grid=(n, t // tq),
        in_specs=[pl.BlockSpec((1, tq, w), lambda i, j: (i, j, 0)),
                  pl.BlockSpec((1, m, w), lambda i, j: (i, 0, 0)),
                  pl.BlockSpec((1, m, w), lambda i, j: (i, 0, 0))],
        out_specs=pl.BlockSpec((1, tq, w), lambda i, j: (i, j, 0)),
        out_shape=jax.ShapeDtypeStruct((n, t, w), F32),
        compiler_params=_cparams("parallel", "parallel"),
        name="mem_attn",
    )(q, k, v)


def _cumsum_kernel(lf_ref, crep_ref, crow_ref, *, blk, n_heads):
    seq = lf_ref.shape[1]
    r = lax.broadcasted_iota(jnp.int32, (blk, blk), 0)
    c = lax.broadcasted_iota(jnp.int32, (blk, blk), 1)
    tri = (c <= r).astype(F32)
    carry = jnp.zeros((1, LANES), F32)
    for i in range(seq // blk):
        rows = slice(i * blk, (i + 1) * blk)
        cs = _dot_exact(tri, lf_ref[0, rows, :]) + carry
        carry = cs[blk - 1:blk, :]
        crow_ref[0, :, rows] = cs.T[0:HEAD_PAD, :]
        for hh in range(n_heads):
            crep_ref[0, hh, rows, :] = jnp.broadcast_to(cs[:, hh:hh + 1], (blk, LANES))


def _fox_cumsum(logf_pad, n_heads):
    bsz, seq, _ = logf_pad.shape
    blk = _pick(seq, (256, 128))
    return pl.pallas_call(
        functools.partial(_cumsum_kernel, blk=blk, n_heads=n_heads),
        grid=(bsz,),
        in_specs=[pl.BlockSpec((1, seq, LANES), lambda b: (b, 0, 0))],
        out_specs=[pl.BlockSpec((1, n_heads, seq, LANES), lambda b: (b, 0, 0, 0)),
                   pl.BlockSpec((1, HEAD_PAD, seq), lambda b: (b, 0, 0))],
        out_shape=[jax.ShapeDtypeStruct((bsz, n_heads, seq, LANES), F32),
                   jax.ShapeDtypeStruct((bsz, HEAD_PAD, seq), F32)],
        compiler_params=_cparams("parallel"),
        name="fox_cumsum",
    )(logf_pad)


def _fox_prompt_kernel(q_ref, k_ref, v_ref, crep_ref, crow_ref, o_ref,
                       kb_ref, vb_ref, m_ref, l_ref, acc_ref, *, tq, tk):
    hh = pl.program_id(1)
    qi = pl.program_id(2)

    @pl.when(qi == 0)
    def _():
        kb_ref[...] = k_ref[...].astype(BF16)
        vb_ref[...] = v_ref[...].astype(BF16)

    q = q_ref[...].astype(BF16)
    cq = jnp.tile(crep_ref[0, 0], (1, tk // LANES))
    m_ref[...] = jnp.full(m_ref.shape, NEG_INF, F32)
    l_ref[...] = jnp.zeros(l_ref.shape, F32)
    acc_ref[...] = jnp.zeros(acc_ref.shape, F32)
    row_pos = qi * tq + lax.broadcasted_iota(jnp.int32, (tq, tk), 0)
    col_iota = lax.broadcasted_iota(jnp.int32, (tq, tk), 1)

    def body(kj, carry):
        k0 = pl.multiple_of(kj * tk, tk)
        s = _dot_nt(q, kb_ref[pl.ds(k0, tk), :]) * ATTN_SCALE
        s = s + (cq - crow_ref[0, pl.ds(hh, 1), pl.ds(k0, tk)])
        s = jnp.where(col_iota + k0 <= row_pos, s, NEG_INF)
        m_old = m_ref[...]
        m_new = jnp.maximum(m_old, jnp.max(s, axis=-1, keepdims=True))
        alpha = jnp.exp(m_old - m_new)
        p = jnp.exp(s - m_new)
        l_ref[...] = alpha * l_ref[...] + jnp.sum(p, axis=-1, keepdims=True)
        acc_ref[...] = alpha * acc_ref[...] + _dot(p.astype(BF16), vb_ref[pl.ds(k0, tk), :])
        m_ref[...] = m_new
        return carry

    lax.fori_loop(0, (qi + 1) * (tq // tk), body, 0)
    o_ref[...] = acc_ref[...] / l_ref[...]


def _fox_prompt(q, kv, crep, crow, bsz, seq, n_heads):
    tq = _pick(seq, (512, 256, 128))
    tk = _pick(tq, (256, 128))
    nq = seq // tq
    return pl.pallas_call(
        functools.partial(_fox_prompt_kernel, tq=tq, tk=tk),
        grid=(bsz, n_heads, nq),
        in_specs=[pl.BlockSpec((tq, HEAD_DIM), lambda b, h, i: (b * nq + i, h)),
                  pl.BlockSpec((seq, HEAD_DIM), lambda b, h, i: (b, h)),
                  pl.BlockSpec((seq, HEAD_DIM), lambda b, h, i: (b, n_heads + h)),
                  pl.BlockSpec((1, 1, tq, LANES), lambda b, h, i: (b, h, i, 0)),
                  pl.BlockSpec((1, HEAD_PAD, seq), lambda b, h, i: (b, 0, 0))],
        out_specs=pl.BlockSpec((tq, HEAD_DIM), lambda b, h, i: (b * nq + i, h)),
        out_shape=jax.ShapeDtypeStruct((bsz * seq, n_heads * HEAD_DIM), F32),
        scratch_shapes=[pltpu.VMEM((seq, HEAD_DIM), BF16), pltpu.VMEM((seq, HEAD_DIM), BF16),
                        pltpu.VMEM((tq, 1), F32), pltpu.VMEM((tq, 1), F32),
                        pltpu.VMEM((tq, HEAD_DIM), F32)],
        compiler_params=_cparams("parallel", "parallel", "arbitrary"),
        name="fox_prompt",
    )(q, kv, kv, crep, crow)


def _fox_sample_kernel(*refs, pages_per_step, n_new):
    pp = pages_per_step
    pt_ref = refs[0]
    del pt_ref
    q_ref, knew_ref, vnew_ref, lfnew_ref = refs[1:5]
    k_refs = refs[5:5 + pp]
    v_refs = refs[5 + pp:5 + 2 * pp]
    lf_refs = refs[5 + 2 * pp:5 + 3 * pp]
    o_ref = refs[5 + 3 * pp]
    m_ref, l_ref, acc_ref, carry_ref, cn_ref = refs[6 + 3 * pp:]
    step = pl.program_id(1)
    n_rows = q_ref.shape[1]
    page = k_refs[0].shape[1]

    r = lax.broadcasted_iota(jnp.int32, (page, page), 0)
    c = lax.broadcasted_iota(jnp.int32, (page, page), 1)

    @pl.when(step == 0)
    def _():
        m_ref[...] = jnp.full(m_ref.shape, NEG_INF, F32)
        l_ref[...] = jnp.zeros(l_ref.shape, F32)
        acc_ref[...] = jnp.zeros(acc_ref.shape, F32)
        carry_ref[...] = jnp.zeros(carry_ref.shape, F32)
        c_new = _dot_exact(lfnew_ref[0], (r <= c).astype(F32))
        for t in range(n_new):
            cn_ref[t * HEAD_PAD:(t + 1) * HEAD_PAD, :] = c_new[:, t:t + 1]

    q = q_ref[0]

    def attend(s, v_bf16):
        m_old = m_ref[...]
        m_new = jnp.maximum(m_old, jnp.max(s, axis=-1, keepdims=True))
        alpha = jnp.exp(m_old - m_new)
        p = jnp.exp(s - m_new)
        l_ref[...] = alpha * l_ref[...] + jnp.sum(p, axis=-1, keepdims=True)
        acc_ref[...] = alpha * acc_ref[...] + _dot(p.astype(BF16), v_bf16)
        m_ref[...] = m_new

    suffix = (r > c).astype(F32)
    for j in range(pp):
        lf = lf_refs[j][0]
        d16 = _dot_exact(lf, suffix) + carry_ref[...]
        carry_ref[...] = carry_ref[...] + jnp.sum(lf, axis=-1, keepdims=True)
        d = jnp.concatenate([d16] * n_new, axis=0)
        s = _dot_nt(q, k_refs[j][0].astype(BF16)) * ATTN_SCALE + (cn_ref[...] + d)
        attend(s, v_refs[j][0].astype(BF16))

    @pl.when(step == pl.num_programs(1) - 1)
    def _():
        c_new = _dot_exact(lfnew_ref[0], (r <= c).astype(F32))
        d = jnp.concatenate([c_new] * n_new, axis=0)
        s = _dot_nt(q, knew_ref[0].astype(BF16)) * ATTN_SCALE + (cn_ref[...] - d)
        key = lax.broadcasted_iota(jnp.int32, (n_rows, page), 1)
        tok = lax.broadcasted_iota(jnp.int32, (n_rows, page), 0) // HEAD_PAD
        s = jnp.where(key <= tok, s, NEG_INF)
        attend(s, vnew_ref[0].astype(BF16))
        o_ref[0] = acc_ref[...] / l_ref[...]


def _fox_sample(q_rows, k_new, v_new, lf_new, k_pool, v_pool, lf_pool, page_table, n_new):
    n, n_rows, w = q_rows.shape
    page = k_pool.shape[1]
    n_pages = page_table.shape[1]
    pp = _pick(n_pages, (8, 4, 2, 1))
    n_steps = n_pages // pp

    def pool_map(j):
        return lambda i, s, pt: (pt[i, n_pages - 1 - (s * pp + j)], 0, 0)

    req = lambda i, s, pt: (i, 0, 0)
    in_specs = [pl.BlockSpec((1, n_rows, w), req),
                pl.BlockSpec((1, page, w), req),
                pl.BlockSpec((1, page, w), req),
                pl.BlockSpec((1, HEAD_PAD, page), req)]
    in_specs += [pl.BlockSpec((1, page, w), pool_map(j)) for j in range(pp)]
    in_specs += [pl.BlockSpec((1, page, w), pool_map(j)) for j in range(pp)]
    in_specs += [pl.BlockSpec((1, HEAD_PAD, page), pool_map(j)) for j in range(pp)]
    grid_spec = pltpu.PrefetchScalarGridSpec(
        num_scalar_prefetch=1,
        grid=(n, n_steps),
        in_specs=in_specs,
        out_specs=pl.BlockSpec((1, n_rows, w), req),
        scratch_shapes=[pltpu.VMEM((n_rows, 1), F32), pltpu.VMEM((n_rows, 1), F32),
                        pltpu.VMEM((n_rows, w), F32), pltpu.VMEM((HEAD_PAD, 1), F32),
                        pltpu.VMEM((n_rows, 1), F32)],
    )
    return pl.pallas_call(
        functools.partial(_fox_sample_kernel, pages_per_step=pp, n_new=n_new),
        grid_spec=grid_spec,
        out_shape=jax.ShapeDtypeStruct((n, n_rows, w), F32),
        compiler_params=_cparams("parallel", "arbitrary"),
        name="fox_sample",
    )(page_table, q_rows, k_new, v_new, lf_new,
      *([k_pool] * pp), *([v_pool] * pp), *([lf_pool] * pp))


def kernel(x_prompt, x_sample, cache_fox_k, cache_fox_v, cache_fox_logf, state_conv, cache_mem_k, cache_mem_v,
           page_table, mem_prompt, g_mix, g_ffn, g_mem, w_mem_kv, w_in_a, w_out_a, conv_w, conv_b, conv_ln_g,
           conv_ln_b, g_kv, w_kvf, b_f, w_in_b, w_out_b, w_ff_gate, w_ff_up, w_ff_down, w_router, b_router,
           w_e_gate, w_e_up, w_e_down, g_final):
    bsz, seq, d = x_prompt.shape
    n_dec, t_dec, _ = x_sample.shape
    depth = g_mix.shape[0]
    n_a = w_in_a.shape[0]
    conv_ch = conv_w.shape[-1]
    width = conv_w.shape[1]
    n_mem = mem_prompt.shape[1]
    fox_w = w_in_b.shape[-1] - MEM_WIDTH
    n_heads = fox_w // HEAD_DIM
    n_pool, page = cache_fox_k.shape[:2]
    tp = bsz * seq
    ts = n_dec * t_dec
    assert n_heads <= HEAD_PAD and width - 1 <= CONV_HALO and t_dec <= 8

    h = jnp.concatenate([x_prompt.reshape(tp, d), x_sample.reshape(ts, d)], axis=0)

    mem_flat = mem_prompt.reshape(bsz * n_mem, d)
    mem_k_p, mem_v_p = [], []
    for i in range(depth):
        z = _matmul([_rmsnorm(mem_flat, g_mem[i], BF16)], w_mem_kv[i], [0], 0, 2 * MEM_WIDTH)
        mem_k_p.append(z[:, :MEM_WIDTH].reshape(bsz, n_mem, MEM_WIDTH))
        mem_v_p.append(z[:, MEM_WIDTH:].reshape(bsz, n_mem, MEM_WIDTH))

    def mem_attend(q_mem, i):
        q_p = q_mem[:tp].reshape(bsz, seq, MEM_WIDTH)
        q_s = jnp.pad(q_mem[tp:].reshape(n_dec, t_dec, MEM_WIDTH), ((0, 0), (0, 8 - t_dec), (0, 0)))
        o_p = _mem_attn(q_p, mem_k_p[i], mem_v_p[i])
        o_s = _mem_attn(q_s, cache_mem_k[i].reshape(n_dec, n_mem, MEM_WIDTH),
                        cache_mem_v[i].reshape(n_dec, n_mem, MEM_WIDTH))
        return jnp.concatenate([o_p.reshape(tp, MEM_WIDTH), o_s[:, :t_dec].reshape(ts, MEM_WIDTH)], axis=0)

    new_conv_p, new_conv_s = [], []
    kv = None
    for i in range(depth):
        hn = _rmsnorm(h, g_mix[i], BF16)
        if i < n_a:
            u = _matmul_glu(hn, w_in_a[i], conv_ch)
            q_mem = _matmul([hn], w_in_a[i], [0], 2 * conv_ch, MEM_WIDTH)
            u_p = u[:tp].reshape(bsz, seq, conv_ch)
            u_s = u[tp:].reshape(n_dec, t_dec, conv_ch)
            ext_p = jnp.concatenate([jnp.zeros((bsz, width - 1, conv_ch), F32), u_p,
                                     jnp.zeros((bsz, CONV_HALO - (width - 1), conv_ch), F32)], axis=1)
            ext_s = jnp.concatenate([state_conv[i], u_s], axis=1)
            new_conv_p.append(ext_p[:, seq:seq + width - 1])
            new_conv_s.append(ext_s[:, t_dec:])
            mix_p = _conv_prompt(ext_p, conv_w[i], conv_b[i], conv_ln_g[i], conv_ln_b[i], seq)
            mix_s = _conv_sample(ext_s.transpose(1, 0, 2), conv_w[i], conv_b[i], conv_ln_g[i], conv_ln_b[i])
            mix = jnp.concatenate([mix_p.reshape(tp, conv_ch),
                                   mix_s.transpose(1, 0, 2).reshape(ts, conv_ch)], axis=0)
            w_out = w_out_a[i]
        else:
            j = i - n_a
            if kv is None:
                hk = _rmsnorm(h, g_kv, BF16)
                kv_all = _matmul([hk], w_kvf, [0], 0, 2 * fox_w)
                w_f = jnp.pad(w_kvf[:, 2 * fox_w:], ((0, 0), (0, LANES - n_heads)))
                b_f_pad = jnp.pad(b_f, (0, LANES - n_heads)).reshape(1, LANES)
                logf_pad = _matmul([hk], w_f, [0], 0, LANES, epilogue="logsig", extra=b_f_pad)
                crep, crow = _fox_cumsum(logf_pad[:tp].reshape(bsz, seq, LANES), n_heads)
                k_new = jnp.pad(kv_all[tp:, :fox_w].reshape(n_dec, t_dec, fox_w),
                                ((0, 0), (0, page - t_dec), (0, 0)))
                v_new = jnp.pad(kv_all[tp:, fox_w:].reshape(n_dec, t_dec, fox_w),
                                ((0, 0), (0, page - t_dec), (0, 0)))
                lf_new = jnp.pad(logf_pad[tp:, :n_heads].reshape(n_dec, t_dec, n_heads).transpose(0, 2, 1),
                                 ((0, 0), (0, HEAD_PAD - n_heads), (0, page - t_dec)))
                lf_pool = jnp.pad(cache_fox_logf.astype(F32).transpose(0, 2, 1),
                                  ((0, 0), (0, HEAD_PAD - n_heads), (0, 0)))
                kv = (kv_all, logf_pad)
            q = _matmul([hn], w_in_b[j], [0], 0, fox_w)
            q_mem = _matmul([hn], w_in_b[j], [0], fox_w, MEM_WIDTH)
            mix_p = _fox_prompt(q, kv_all, crep, crow, bsz, seq, n_heads)
            q_s = q[tp:].reshape(n_dec, t_dec, n_heads, HEAD_DIM)
            hid = jnp.arange(n_heads)
            q_rows = jnp.zeros((n_dec, t_dec, HEAD_PAD, n_heads, HEAD_DIM), F32).at[:, :, hid, hid].set(q_s)
            q_rows = q_rows.reshape(n_dec, t_dec * HEAD_PAD, fox_w).astype(BF16)
            o_rows = _fox_sample(q_rows, k_new, v_new, lf_new,
                                 cache_fox_k.reshape(n_pool, page, fox_w),
                                 cache_fox_v.reshape(n_pool, page, fox_w), lf_pool, page_table, t_dec)
            o_rows = o_rows.reshape(n_dec, t_dec, HEAD_PAD, n_heads, HEAD_DIM)
            mix_s = o_rows[:, :, hid, hid]
            mix = jnp.concatenate([mix_p, mix_s.reshape(ts, fox_w)], axis=0)
            w_out = w_out_b[j]
        mo = mem_attend(q_mem, i)
        h = _matmul([mix.astype(BF16), mo.astype(BF16)], w_out, [0, mix.shape[1]], 0, d,
                    epilogue="resid", extra=h)
        hf = _rmsnorm(h, g_ffn[i], BF16)
        jj = i // 2
        if i % 2 == 0:
            h = _ffn(hf, h, w_ff_gate[jj][None], w_ff_up[jj][None], w_ff_down[jj][None])
        else:
            gates = _router(hf, w_router[jj], b_router[jj])
            h = _ffn(hf, h, w_e_gate[jj], w_e_up[jj], w_e_down[jj], gates=gates)

    y = _rmsnorm(h, g_final, F32)
    kv_all, logf_pad = kv
    k_all = kv_all[:, :fox_w]
    v_all = kv_all[:, fox_w:]
    logf = logf_pad[:, :n_heads]
    return (y[:tp].reshape(bsz, seq, d),
            y[tp:].reshape(n_dec, t_dec, d),
            k_all[:tp].reshape(bsz, seq, n_heads, HEAD_DIM),
            v_all[:tp].reshape(bsz, seq, n_heads, HEAD_DIM),
            logf[:tp].reshape(bsz, seq, n_heads),
            jnp.stack(new_conv_p),
            jnp.stack([k.reshape(bsz, n_mem, MEM_HEADS, HEAD_DIM) for k in mem_k_p]),
            jnp.stack([v.reshape(bsz, n_mem, MEM_HEADS, HEAD_DIM) for v in mem_v_p]),
            k_all[tp:].reshape(n_dec, t_dec, n_heads, HEAD_DIM),
            v_all[tp:].reshape(n_dec, t_dec, n_heads, HEAD_DIM),
            logf[tp:].reshape(n_dec, t_dec, n_heads),
            jnp.stack(new_conv_s))
```

```python
import functools

import jax
import jax.numpy as jnp
from jax import lax
from jax.experimental import pallas as pl
from jax.experimental.pallas import tpu as pltpu

HEAD_DIM = 128
MEM_HEADS = 4
MEM_WIDTH = MEM_HEADS * HEAD_DIM
TOP_K = 2
EPS = 1e-6
NEG_INF = -1e30
ATTN_SCALE = HEAD_DIM ** -0.5
LANES = 128
HEAD_PAD = 16
Q_ROWS = 8
CONV_HALO = 32
VMEM_LIMIT_BYTES = 56 * 1024 * 1024
MOE_SUB = 256
MOE_ROW_TILE = 3 * MOE_SUB
MOE_UNSELECTED = -1.0e9

BF16 = jnp.bfloat16
F32 = jnp.float32


def _cparams(*semantics):
    return pltpu.CompilerParams(dimension_semantics=semantics, vmem_limit_bytes=VMEM_LIMIT_BYTES)


def _pick(n, candidates):
    for c in candidates:
        if c <= n and n % c == 0:
            return c
    return n


def _row_tile(t):
    return _pick(t, (1040, 1024, 640, 512, 256, 128, 64, 32, 16))


def _dot(a, b):
    return jnp.dot(a, b, preferred_element_type=F32)


def _dot_nt(a, b):
    return lax.dot_general(a, b, (((1,), (1,)), ((), ())), preferred_element_type=F32)


def _dot_exact(a, b):
    return jnp.dot(a, b, preferred_element_type=F32, precision=lax.Precision.HIGHEST)


def _log_sigmoid(x):
    return jnp.minimum(x, 0.0) - jnp.log1p(jnp.exp(-jnp.abs(x)))


def _rmsnorm_kernel(x_ref, g_ref, o_ref):
    x = x_ref[...]
    y = x * lax.rsqrt(jnp.mean(x * x, axis=-1, keepdims=True) + EPS)
    o_ref[...] = (y * g_ref[...]).astype(o_ref.dtype)


def _rmsnorm(x, g, out_dtype):
    t, d = x.shape
    tm = _pick(t, (640, 512, 256, 128, 64, 32, 16))
    return pl.pallas_call(
        _rmsnorm_kernel,
        grid=(t // tm,),
        in_specs=[pl.BlockSpec((tm, d), lambda i: (i, 0)),
                  pl.BlockSpec((1, d), lambda i: (0, 0))],
        out_specs=pl.BlockSpec((tm, d), lambda i: (i, 0)),
        out_shape=jax.ShapeDtypeStruct((t, d), out_dtype),
        compiler_params=_cparams("parallel"),
        name="rmsnorm",
    )(x, g.reshape(1, d))


def _mm_kernel(*refs, n_a, epilogue):
    a_refs = refs[:n_a]
    w_refs = refs[n_a:2 * n_a]
    pos = 2 * n_a
    extra_ref = None
    if epilogue in ("resid", "logsig"):
        extra_ref = refs[pos]
        pos += 1
    o_ref = refs[pos]
    wb_refs = refs[pos + 1:pos + 1 + n_a]

    @pl.when(pl.program_id(1) == 0)
    def _():
        for w_ref, wb_ref in zip(w_refs, wb_refs):
            wb_ref[...] = w_ref[...].astype(BF16)

    acc = None
    for a_ref, wb_ref in zip(a_refs, wb_refs):
        d = _dot(a_ref[...], wb_ref[...])
        acc = d if acc is None else acc + d
    if epilogue == "resid":
        acc = extra_ref[...] + acc
    elif epilogue == "logsig":
        acc = _log_sigmoid(acc + extra_ref[...])
    o_ref[...] = acc.astype(o_ref.dtype)


def _matmul(a_list, w, row_offs, col_off, n_out, *, epilogue="none", extra=None, out_dtype=F32):
    t = a_list[0].shape[0]
    tm = _row_tile(t)
    tn = _pick(n_out, (512, 256, 128))
    n_a = len(a_list)
    in_specs, scratch = [], []
    for a in a_list:
        in_specs.append(pl.BlockSpec((tm, a.shape[1]), lambda n, m: (m, 0)))
    for a, ro in zip(a_list, row_offs):
        k = a.shape[1]
        assert ro % k == 0 and col_off % tn == 0
        in_specs.append(pl.BlockSpec((k, tn), functools.partial(
            lambda n, m, rb, cb: (rb, cb + n), rb=ro // k, cb=col_off // tn)))
        scratch.append(pltpu.VMEM((k, tn), BF16))
    args = list(a_list) + [w] * n_a
    if epilogue == "resid":
        in_specs.append(pl.BlockSpec((tm, tn), lambda n, m: (m, n)))
        args.append(extra)
    elif epilogue == "logsig":
        in_specs.append(pl.BlockSpec((1, tn), lambda n, m: (0, n)))
        args.append(extra)
    return pl.pallas_call(
        functools.partial(_mm_kernel, n_a=n_a, epilogue=epilogue),
        grid=(n_out // tn, t // tm),
        in_specs=in_specs,
        out_specs=pl.BlockSpec((tm, tn), lambda n, m: (m, n)),
        out_shape=jax.ShapeDtypeStruct((t, n_out), out_dtype),
        scratch_shapes=scratch,
        compiler_params=_cparams("arbitrary", "arbitrary"),
        name="matmul_" + epilogue,
    )(*args)


def _glu_kernel(a_ref, w1_ref, w2_ref, o_ref, wb1_ref, wb2_ref):
    @pl.when(pl.program_id(1) == 0)
    def _():
        wb1_ref[...] = w1_ref[...].astype(BF16)
        wb2_ref[...] = w2_ref[...].astype(BF16)

    a = a_ref[...]
    o_ref[...] = _dot(a, wb1_ref[...]) * jax.nn.sigmoid(_dot(a, wb2_ref[...]))


def _matmul_glu(a, w, n_out):
    t, k = a.shape
    tm = _row_tile(t)
    tn = _pick(n_out, (512, 256, 128))
    nb = n_out // tn
    return pl.pallas_call(
        _glu_kernel,
        grid=(nb, t // tm),
        in_specs=[pl.BlockSpec((tm, k), lambda n, m: (m, 0)),
                  pl.BlockSpec((k, tn), lambda n, m: (0, n)),
                  pl.BlockSpec((k, tn), lambda n, m: (0, n + nb))],
        out_specs=pl.BlockSpec((tm, tn), lambda n, m: (m, n)),
        out_shape=jax.ShapeDtypeStruct((t, n_out), F32),
        scratch_shapes=[pltpu.VMEM((k, tn), BF16), pltpu.VMEM((k, tn), BF16)],
        compiler_params=_cparams("arbitrary", "arbitrary"),
        name="matmul_glu",
    )(a, w, w)


def _swiglu_partial(x, wg, wu, wd):
    g = _dot(x, wg)
    u = _dot(x, wu)
    return _dot((g * jax.nn.sigmoid(g) * u).astype(BF16), wd)


def _ffn_kernel(x_ref, h_ref, wg_ref, wu_ref, wd_ref, o_ref):
    @pl.when(pl.program_id(1) == 0)
    def _():
        o_ref[...] = h_ref[...]

    o_ref[...] += _swiglu_partial(x_ref[...], wg_ref[...].astype(BF16), wu_ref[...].astype(BF16),
                                  wd_ref[...].astype(BF16))


def _ffn(x, h, w_gate, w_up, w_down):
    t, d = x.shape
    f = w_gate.shape[1]
    tm = _row_tile(t)
    tf = _pick(f, (256, 128))
    single = pl.Buffered(1)
    return pl.pallas_call(
        _ffn_kernel,
        grid=(t // tm, f // tf),
        in_specs=[pl.BlockSpec((tm, d), lambda m, j: (m, 0), pipeline_mode=single),
                  pl.BlockSpec((tm, d), lambda m, j: (m, 0), pipeline_mode=single),
                  pl.BlockSpec((d, tf), lambda m, j: (0, j)),
                  pl.BlockSpec((d, tf), lambda m, j: (0, j)),
                  pl.BlockSpec((tf, d), lambda m, j: (j, 0))],
        out_specs=pl.BlockSpec((tm, d), lambda m, j: (m, 0)),
        out_shape=jax.ShapeDtypeStruct((t, d), F32),
        compiler_params=_cparams("parallel", "arbitrary"),
        name="ffn",
    )(x, h, w_gate, w_up, w_down)


def _router_kernel(x_ref, w_ref, b_ref, o_ref, *, n_experts):
    logits = _dot(x_ref[...], w_ref[...].astype(BF16)) + b_ref[...]
    lane = lax.broadcasted_iota(jnp.int32, logits.shape, 1)
    logits = jnp.where(lane < n_experts, logits, NEG_INF)
    v1 = jnp.max(logits, axis=-1, keepdims=True)
    i1 = jnp.min(jnp.where(logits == v1, lane, LANES), axis=-1, keepdims=True)
    rest = jnp.where(lane == i1, NEG_INF, logits)
    v2 = jnp.max(rest, axis=-1, keepdims=True)
    i2 = jnp.min(jnp.where(rest == v2, lane, LANES), axis=-1, keepdims=True)
    e2 = jnp.exp(v2 - v1)
    p1 = 1.0 / (1.0 + e2)
    p2 = e2 / (1.0 + e2)
    o_ref[...] = jnp.where(lane == i1, p1, jnp.where(lane == i2, p2, 0.0))


def _router(x, w_router, b_router):
    t, d = x.shape
    n_e = w_router.shape[-1]
    tm = _row_tile(t)
    w_pad = jnp.pad(w_router, ((0, 0), (0, LANES - n_e)))
    b_pad = jnp.pad(b_router, (0, LANES - n_e)).reshape(1, LANES)
    return pl.pallas_call(
        functools.partial(_router_kernel, n_experts=n_e),
        grid=(t // tm,),
        in_specs=[pl.BlockSpec((tm, d), lambda m: (m, 0)),
                  pl.BlockSpec((d, LANES), lambda m: (0, 0)),
                  pl.BlockSpec((1, LANES), lambda m: (0, 0))],
        out_specs=pl.BlockSpec((tm, LANES), lambda m: (m, 0)),
        out_shape=jax.ShapeDtypeStruct((t, LANES), F32),
        compiler_params=_cparams("parallel"),
        name="router",
    )(x, w_pad, b_pad)


def _moe_rank_kernel(g_ref, pos_ref, pos_t_ref, start_ref, cnt_ref, carry_ref):
    i = pl.program_id(0)
    tm = g_ref.shape[0]

    @pl.when(i == 0)
    def _():
        carry_ref[...] = jnp.zeros(carry_ref.shape, F32)

    sel = g_ref[...] > 0.0
    r = lax.broadcasted_iota(jnp.int32, (tm, tm), 0)
    c = lax.broadcasted_iota(jnp.int32, (tm, tm), 1)
    before = jnp.where(c < r, 1.0, 0.0).astype(BF16)
    ones = jnp.where(sel, 1.0, 0.0)
    carry = carry_ref[...]
    rank = _dot(before, ones.astype(BF16)) + carry
    start_ref[...] = jnp.broadcast_to(carry, start_ref.shape)
    carry = rank[tm - 1:tm, :] + ones[tm - 1:tm, :]
    carry_ref[...] = carry
    cnt_ref[...] = jnp.broadcast_to(carry, cnt_ref.shape)
    pos = jnp.where(sel, rank, MOE_UNSELECTED)
    pos_ref[...] = pos
    pos_t_ref[...] = pos.T[0:HEAD_PAD, :]


def _moe_rank(gates, tc):
    t = gates.shape[0]
    nch = t // tc
    return pl.pallas_call(
        _moe_rank_kernel,
        grid=(nch,),
        in_specs=[pl.BlockSpec((tc, LANES), lambda i: (i, 0))],
        out_specs=[pl.BlockSpec((tc, LANES), lambda i: (i, 0)),
                   pl.BlockSpec((HEAD_PAD, tc), lambda i: (0, i)),
                   pl.BlockSpec((8, LANES), lambda i: (i, 0)),
                   pl.BlockSpec((8, LANES), lambda i: (0, 0))],
        out_shape=[jax.ShapeDtypeStruct((t, LANES), F32),
                   jax.ShapeDtypeStruct((HEAD_PAD, t), F32),
                   jax.ShapeDtypeStruct((nch * 8, LANES), F32),
                   jax.ShapeDtypeStruct((8, LANES), F32)],
        scratch_shapes=[pltpu.VMEM((1, LANES), F32)],
        compiler_params=_cparams("arbitrary"),
        name="moe_rank",
    )(gates)


def _moe_gather_kernel(sb_ref, ch_ref, first_ref, last_ref, valid_ref, esb_ref, base_ref,
                       pos_t_ref, x_ref, o_ref, acc_ref):
    i = pl.program_id(0)

    @pl.when(valid_ref[i] == 1)
    def _():
        sb = sb_ref[i]
        local = pos_t_ref[pl.ds(esb_ref[sb], 1), :].astype(jnp.int32) + base_ref[sb]
        rows = lax.broadcasted_iota(jnp.int32, (o_ref.shape[0], local.shape[1]), 0)
        onehot = jnp.where(local == rows, 1.0, 0.0).astype(BF16)
        part = _dot(onehot, x_ref[...])

        @pl.when(first_ref[i] == 1)
        def _():
            acc_ref[...] = part

        @pl.when(first_ref[i] == 0)
        def _():
            acc_ref[...] += part

        @pl.when(last_ref[i] == 1)
        def _():
            o_ref[...] = acc_ref[...].astype(o_ref.dtype)


def _moe_gather(plan, pos_t, x, n_rows, tc):
    d = x.shape[1]
    sb = MOE_SUB
    prefetch = (plan["g_sb"], plan["g_ch"], plan["g_first"], plan["g_last"], plan["g_valid"],
                plan["e_sb"], plan["base_sb"])
    grid_spec = pltpu.PrefetchScalarGridSpec(
        num_scalar_prefetch=len(prefetch),
        grid=(plan["g_sb"].shape[0],),
        in_specs=[pl.BlockSpec((HEAD_PAD, tc), lambda i, sbl, chl, *_: (0, chl[i])),
                  pl.BlockSpec((tc, d), lambda i, sbl, chl, *_: (chl[i], 0))],
        out_specs=pl.BlockSpec((sb, d), lambda i, sbl, chl, *_: (sbl[i], 0)),
        scratch_shapes=[pltpu.VMEM((sb, d), F32)],
    )
    return pl.pallas_call(
        _moe_gather_kernel,
        grid_spec=grid_spec,
        out_shape=jax.ShapeDtypeStruct((n_rows, d), BF16),
        compiler_params=_cparams("arbitrary"),
        name="moe_gather",
    )(*prefetch, pos_t, x)


def _moe_ffn_kernel(et_ref, nv_ref, x_ref, wg_ref, wu_ref, wd_ref, o_ref, acc_ref):
    del et_ref
    r = pl.program_id(0)
    f = pl.program_id(1)
    last_f = pl.num_programs(1) - 1
    n_valid = nv_ref[r]
    n_sub = o_ref.shape[0] // MOE_SUB

    @pl.when(n_valid > 0)
    def _():
        wg = wg_ref[0].astype(BF16)
        wu = wu_ref[0].astype(BF16)
        wd = wd_ref[0].astype(BF16)
        for j in range(n_sub):
            rows = slice(j * MOE_SUB, (j + 1) * MOE_SUB)

            @pl.when(j < n_valid)
            def _():
                part = _swiglu_partial(x_ref[rows, :], wg, wu, wd)

                @pl.when(f == 0)
                def _():
                    acc_ref[rows, :] = part

                @pl.when(f > 0)
                def _():
                    acc_ref[rows, :] += part

                @pl.when(f == last_f)
                def _():
                    o_ref[rows, :] = acc_ref[rows, :].astype(o_ref.dtype)

    @pl.when(f == last_f)
    def _():
        for j in range(n_sub):
            @pl.when(j >= n_valid)
            def _():
                o_ref[j * MOE_SUB:(j + 1) * MOE_SUB, :] = jnp.zeros((MOE_SUB, o_ref.shape[1]), o_ref.dtype)


def _moe_ffn(plan, xs, w_gate, w_up, w_down):
    n_rows, d = xs.shape
    f = w_gate.shape[2]
    tr = MOE_ROW_TILE
    tf = _pick(f, (256, 128))
    nf = f // tf

    def w_in_map(r, j, et, nv):
        return (et[r], 0, jnp.where(nv[r] > 0, j, nf - 1))

    def w_out_map(r, j, et, nv):
        return (et[r], jnp.where(nv[r] > 0, j, nf - 1), 0)

    grid_spec = pltpu.PrefetchScalarGridSpec(
        num_scalar_prefetch=2,
        grid=(n_rows // tr, nf),
        in_specs=[pl.BlockSpec((tr, d), lambda r, j, et, nv: (r, 0)),
                  pl.BlockSpec((1, d, tf), w_in_map),
                  pl.BlockSpec((1, d, tf), w_in_map),
                  pl.BlockSpec((1, tf, d), w_out_map)],
        out_specs=pl.BlockSpec((tr, d), lambda r, j, et, nv: (r, 0)),
        scratch_shapes=[pltpu.VMEM((tr, d), F32)],
    )
    return pl.pallas_call(
        _moe_ffn_kernel,
        grid_spec=grid_spec,
        out_shape=jax.ShapeDtypeStruct((n_rows, d), BF16),
        compiler_params=_cparams("arbitrary", "arbitrary"),
        name="moe_ffn",
    )(plan["e_tile"], plan["nv_tile"], xs, w_gate, w_up, w_down)


def _moe_combine_kernel(ch_ref, sb_ref, first_ref, valid_ref, esb_ref, base_ref,
                        pos_ref, gate_ref, h_ref, y_ref, o_ref):
    i = pl.program_id(0)

    @pl.when(first_ref[i] == 1)
    def _():
        o_ref[...] = h_ref[...]

    @pl.when(valid_ref[i] == 1)
    def _():
        sb = sb_ref[i]
        lane = lax.broadcasted_iota(jnp.int32, pos_ref.shape, 1)
        mine = lane == esb_ref[sb]
        rank = jnp.sum(jnp.where(mine, pos_ref[...], 0.0), axis=-1, keepdims=True)
        local = rank.astype(jnp.int32) + base_ref[sb]
        gate = jnp.sum(jnp.where(mine, gate_ref[...], 0.0), axis=-1, keepdims=True)
        cols = lax.broadcasted_iota(jnp.int32, (pos_ref.shape[0], y_ref.shape[0]), 1)
        onehot = jnp.where(local == cols, 1.0, 0.0).astype(BF16)
        o_ref[...] += gate * _dot(onehot, y_ref[...])


def _moe_combine(plan, pos, gates, h, ys, tc):
    t, d = h.shape
    sb = MOE_SUB
    prefetch = (plan["c_ch"], plan["c_sb"], plan["c_first"], plan["c_valid"], plan["e_sb"], plan["base_sb"])
    grid_spec = pltpu.PrefetchScalarGridSpec(
        num_scalar_prefetch=len(prefetch),
        grid=(plan["c_ch"].shape[0],),
        in_specs=[pl.BlockSpec((tc, LANES), lambda i, chl, sbl, *_: (chl[i], 0)),
                  pl.BlockSpec((tc, LANES), lambda i, chl, sbl, *_: (chl[i], 0)),
                  pl.BlockSpec((tc, d), lambda i, chl, sbl, *_: (chl[i], 0)),
                  pl.BlockSpec((sb, d), lambda i, chl, sbl, *_: (sbl[i], 0))],
        out_specs=pl.BlockSpec((tc, d), lambda i, chl, sbl, *_: (chl[i], 0)),
    )
    return pl.pallas_call(
        _moe_combine_kernel,
        grid_spec=grid_spec,
        out_shape=jax.ShapeDtypeStruct((t, d), F32),
        compiler_params=_cparams("arbitrary"),
        name="moe_combine",
    )(*prefetch, pos, gates, h, ys)


def _pair_list(overlap, length):
    n_b = overlap.shape[1]
    flat = overlap.reshape(-1)
    n = jnp.sum(flat.astype(jnp.int32))
    idx = jnp.nonzero(flat, size=length, fill_value=0)[0].astype(jnp.int32)
    step = jnp.arange(length, dtype=jnp.int32)
    valid = step < n
    idx = jnp.where(valid, idx, idx[jnp.maximum(n - 1, 0)])
    a = idx // n_b
    b = idx % n_b
    nxt_valid = jnp.concatenate([valid[1:], jnp.zeros((1,), bool)])
    first = valid & (a != jnp.concatenate([jnp.full((1,), -1, jnp.int32), a[:-1]]))
    last = valid & ((a != jnp.concatenate([a[1:], jnp.full((1,), -1, jnp.int32)])) | ~nxt_valid)
    as_i32 = lambda v: v.astype(jnp.int32)
    return a, b, as_i32(first), as_i32(last), as_i32(valid)


def _moe_plan(counts, chunk_start, n_rows, nch):
    n_e = counts.shape[0]
    tr, sb = MOE_ROW_TILE, MOE_SUB
    seg = (counts + tr - 1) // tr * tr
    seg_end = jnp.cumsum(seg)
    off = seg_end - seg
    chunk_end = jnp.concatenate([chunk_start[:, 1:], counts[:, None]], axis=1)

    def owner(start):
        return jnp.clip(jnp.searchsorted(seg_end, start, side="right"), 0, n_e - 1).astype(jnp.int32)

    sb_start = jnp.arange(n_rows // sb, dtype=jnp.int32) * sb
    e_sb = owner(sb_start)
    k0 = sb_start - off[e_sb]
    k1 = jnp.minimum(k0 + sb, counts[e_sb])
    sb_used = (sb_start < seg_end[-1]) & (k0 < counts[e_sb])
    overlap = sb_used[:, None] & (chunk_start[e_sb] < k1[:, None]) & (chunk_end[e_sb] > k0[:, None])
    length = n_rows // sb + n_e * nch
    g_sb, g_ch, g_first, g_last, g_valid = _pair_list(overlap, length)
    c_ch, c_sb, c_first, _, c_valid = _pair_list(overlap.T, length)

    tile_start = jnp.arange(n_rows // tr, dtype=jnp.int32) * tr
    e_tile = owner(tile_start)
    left = counts[e_tile] - (tile_start - off[e_tile])
    nv_tile = jnp.where(tile_start < seg_end[-1], jnp.clip((left + sb - 1) // sb, 0, tr // sb), 0)
    return dict(g_sb=g_sb, g_ch=g_ch, g_first=g_first, g_last=g_last, g_valid=g_valid,
                c_ch=c_ch, c_sb=c_sb, c_first=c_first, c_valid=c_valid,
                e_sb=e_sb, base_sb=(off[e_sb] - sb_start).astype(jnp.int32),
                e_tile=e_tile, nv_tile=nv_tile.astype(jnp.int32))


def _moe(x, h, w_router, b_router, w_gate, w_up, w_down):
    t = x.shape[0]
    n_e = w_router.shape[-1]
    tc = _pick(t, (640, 512, 384, 256, 128))
    assert tc % LANES == 0 and t % tc == 0 and n_e <= 8
    nch = t // tc
    n_rows = -(-(TOP_K * t + n_e * (MOE_ROW_TILE - 1)) // MOE_ROW_TILE) * MOE_ROW_TILE
    gates = _router(x, w_router, b_router)
    pos, pos_t, starts, totals = _moe_rank(gates, tc)
    counts = totals[0, :n_e].astype(jnp.int32)
    chunk_start = starts.reshape(nch, 8, LANES)[:, 0, :n_e].T.astype(jnp.int32)
    plan = _moe_plan(counts, chunk_start, n_rows, nch)
    xs = _moe_gather(plan, pos_t, x, n_rows, tc)
    ys = _moe_ffn(plan, xs, w_gate, w_up, w_down)
    return _moe_combine(plan, pos, gates, h, ys, tc)


def _ln_swish(hc, lg, lb):
    mu = jnp.mean(hc, axis=-1, keepdims=True)
    xc = hc - mu
    var = jnp.mean(xc * xc, axis=-1, keepdims=True)
    y = xc * lax.rsqrt(var + EPS) * lg + lb
    return y * jax.nn.sigmoid(y)


def _conv_kernel(main_ref, nxt_ref, w_ref, b_ref, lg_ref, lb_ref, o_ref, ext_ref, acc_ref, *, tt, width):
    ext_ref[0:tt, :] = main_ref[0]
    ext_ref[tt:tt + CONV_HALO, :] = nxt_ref[0]
    n_chunks = o_ref.shape[-1] // LANES

    def chunk(c, carry):
        c0 = pl.multiple_of(c * LANES, LANES)
        acc = jnp.zeros((tt, LANES), F32)
        for j in range(width):
            acc = acc + ext_ref[pl.ds(j, tt), pl.ds(c0, LANES)] * w_ref[pl.ds(j, 1), pl.ds(c0, LANES)]
        acc_ref[:, pl.ds(c0, LANES)] = acc + b_ref[:, pl.ds(c0, LANES)]
        return carry

    lax.fori_loop(0, n_chunks, chunk, 0)
    o_ref[0] = _ln_swish(acc_ref[...], lg_ref[...], lb_ref[...])


def _conv_prompt(u_ext, w, b, lg, lb, seq):
    bsz, _, c = u_ext.shape
    width = w.shape[0]
    tt = _pick(seq, (256, 128, 64, 32))
    return pl.pallas_call(
        functools.partial(_conv_kernel, tt=tt, width=width),
        grid=(bsz, seq // tt),
        in_specs=[pl.BlockSpec((1, tt, c), lambda bb, i: (bb, i, 0)),
                  pl.BlockSpec((1, CONV_HALO, c), lambda bb, i: (bb, (i + 1) * (tt // CONV_HALO), 0)),
                  pl.BlockSpec((width, c), lambda bb, i: (0, 0)),
                  pl.BlockSpec((1, c), lambda bb, i: (0, 0)),
                  pl.BlockSpec((1, c), lambda bb, i: (0, 0)),
                  pl.BlockSpec((1, c), lambda bb, i: (0, 0))],
        out_specs=pl.BlockSpec((1, tt, c), lambda bb, i: (bb, i, 0)),
        out_shape=jax.ShapeDtypeStruct((bsz, seq, c), F32),
        scratch_shapes=[pltpu.VMEM((tt + CONV_HALO, c), F32), pltpu.VMEM((tt, c), F32)],
        compiler_params=_cparams("parallel", "parallel"),
        name="conv_prompt",
    )(u_ext, u_ext, w, b.reshape(1, c), lg.reshape(1, c), lb.reshape(1, c))


def _conv_step_kernel(ext_ref, w_ref, b_ref, lg_ref, lb_ref, o_ref, *, width):
    n_t = o_ref.shape[0]
    for t in range(n_t):
        acc = ext_ref[t] * w_ref[0:1, :]
        for j in range(1, width):
            acc = acc + ext_ref[t + j] * w_ref[j:j + 1, :]
        o_ref[t] = _ln_swish(acc + b_ref[...], lg_ref[...], lb_ref[...])


def _conv_sample(ext_t, w, b, lg, lb):
    l_ext, n, c = ext_t.shape
    width = w.shape[0]
    n_t = l_ext - (width - 1)
    return pl.pallas_call(
        functools.partial(_conv_step_kernel, width=width),
        out_shape=jax.ShapeDtypeStruct((n_t, n, c), F32),
        compiler_params=pltpu.CompilerParams(vmem_limit_bytes=VMEM_LIMIT_BYTES),
        name="conv_sample",
    )(ext_t, w, b.reshape(1, c), lg.reshape(1, c), lb.reshape(1, c))


def _mem_attn_kernel(q_ref, k_ref, v_ref, o_ref):
    for hh in range(MEM_HEADS):
        sl = slice(hh * HEAD_DIM, (hh + 1) * HEAD_DIM)
        q = q_ref[0, :, sl].astype(BF16)
        k = k_ref[0, :, sl].astype(BF16)
        v = v_ref[0, :, sl].astype(BF16)
        s = _dot_nt(q, k) * ATTN_SCALE
        m = jnp.max(s, axis=-1, keepdims=True)
        p = jnp.exp(s - m)
        l = jnp.sum(p, axis=-1, keepdims=True)
        o_ref[0, :, sl] = _dot(p.astype(BF16), v) / l


def _mem_attn(q, k, v):
    n, t, w = q.shape
    m = k.shape[1]
    tq = _pick(t, (512, 256, 128, 64, 32, 16, 8))
    return pl.pallas_call(
        _mem_attn_kernel,
        grid=(n, t // tq),
        in_specs=[pl.BlockSpec((1, tq, w), lambda i, j: (i, j, 0)),
                  pl.BlockSpec((1, m, w), lambda i, j: (i, 0, 0)),
                  pl.BlockSpec((1, m, w), lambda i, j: (i, 0, 0))],
        out_specs=pl.BlockSpec((1, tq, w), lambda i, j: (i, j, 0)),
        out_shape=jax.ShapeDtypeStruct((n, t, w), F32),
        compiler_params=_cparams("parallel", "parallel"),
        name="mem_attn",
    )(q, k, v)


def _cumsum_kernel(lf_ref, crep_ref, crow_ref, *, blk, n_heads):
    seq = lf_ref.shape[1]
    r = lax.broadcasted_iota(jnp.int32, (blk, blk), 0)
    c = lax.broadcasted_iota(jnp.int32, (blk, blk), 1)
    tri = (c <= r).astype(F32)
    carry = jnp.zeros((1, LANES), F32)
    for i in range(seq // blk):
        rows = slice(i * blk, (i + 1) * blk)
        cs = _dot_exact(tri, lf_ref[0, rows, :]) + carry
        carry = cs[blk - 1:blk, :]
        crow_ref[0, :, rows] = cs.T[0:HEAD_PAD, :]
        for hh in range(n_heads):
            crep_ref[0, hh, rows, :] = jnp.broadcast_to(cs[:, hh:hh + 1], (blk, LANES))


def _fox_cumsum(logf_pad, n_heads):
    bsz, seq, _ = logf_pad.shape
    blk = _pick(seq, (256, 128))
    return pl.pallas_call(
        functools.partial(_cumsum_kernel, blk=blk, n_heads=n_heads),
        grid=(bsz,),
        in_specs=[pl.BlockSpec((1, seq, LANES), lambda b: (b, 0, 0))],
        out_specs=[pl.BlockSpec((1, n_heads, seq, LANES), lambda b: (b, 0, 0, 0)),
                   pl.BlockSpec((1, HEAD_PAD, seq), lambda b: (b, 0, 0))],
        out_shape=[jax.ShapeDtypeStruct((bsz, n_heads, seq, LANES), F32),
                   jax.ShapeDtypeStruct((bsz, HEAD_PAD, seq), F32)],
        compiler_params=_cparams("parallel"),
        name="fox_cumsum",
    )(logf_pad)


def _fox_prompt_kernel(q_ref, k_ref, v_ref, crep_ref, crow_ref, o_ref,
                       kb_ref, vt_ref, m_ref, l_ref, acc_ref, *, tq, tk):
    hh = pl.program_id(1)
    qi = pl.program_id(2)

    @pl.when(qi == 0)
    def _():
        kb_ref[...] = k_ref[...].astype(BF16)
        vt_ref[...] = v_ref[...].T.astype(BF16)

    q = q_ref[...].astype(BF16)
    q0 = pl.multiple_of(qi * tq, tq)
    cq = crow_ref[0, pl.ds(hh, 1), pl.ds(q0, tq)]
    m_ref[...] = jnp.full(m_ref.shape, NEG_INF, F32)
    l_ref[...] = jnp.zeros(l_ref.shape, F32)
    acc_ref[...] = jnp.zeros(acc_ref.shape, F32)
    q_pos = q0 + lax.broadcasted_iota(jnp.int32, (tk, tq), 1)
    k_iota = lax.broadcasted_iota(jnp.int32, (tk, tq), 0)

    def body(kj, carry):
        k0 = pl.multiple_of(kj * tk, tk)
        ck = jnp.tile(crep_ref[0, 0, pl.ds(k0, tk), :], (1, tq // LANES))
        s = _dot_nt(kb_ref[pl.ds(k0, tk), :], q) * ATTN_SCALE + (cq - ck)
        s = jnp.where(k_iota + k0 <= q_pos, s, NEG_INF)
        m_old = m_ref[...]
        m_new = jnp.maximum(m_old, jnp.max(s, axis=0, keepdims=True))
        alpha = jnp.exp(m_old - m_new)
        p = jnp.exp(s - m_new)
        l_ref[...] = alpha * l_ref[...] + jnp.sum(p, axis=0, keepdims=True)
        acc_ref[...] = alpha * acc_ref[...] + _dot(vt_ref[:, pl.ds(k0, tk)], p.astype(BF16))
        m_ref[...] = m_new
        return carry

    lax.fori_loop(0, (qi + 1) * (tq // tk), body, 0)
    o_ref[...] = (acc_ref[...] / l_ref[...]).T


def _fox_prompt(q, kv, crep, crow, bsz, seq, n_heads):
    tq = _pick(seq, (512, 256, 128))
    tk = _pick(tq, (256, 128))
    nq = seq // tq
    return pl.pallas_call(
        functools.partial(_fox_prompt_kernel, tq=tq, tk=tk),
        grid=(bsz, n_heads, nq),
        in_specs=[pl.BlockSpec((tq, HEAD_DIM), lambda b, h, i: (b * nq + i, h)),
                  pl.BlockSpec((seq, HEAD_DIM), lambda b, h, i: (b, h)),
                  pl.BlockSpec((seq, HEAD_DIM), lambda b, h, i: (b, n_heads + h)),
                  pl.BlockSpec((1, 1, seq, LANES), lambda b, h, i: (b, h, 0, 0)),
                  pl.BlockSpec((1, HEAD_PAD, seq), lambda b, h, i: (b, 0, 0))],
        out_specs=pl.BlockSpec((tq, HEAD_DIM), lambda b, h, i: (b * nq + i, h)),
        out_shape=jax.ShapeDtypeStruct((bsz * seq, n_heads * HEAD_DIM), F32),
        scratch_shapes=[pltpu.VMEM((seq, HEAD_DIM), BF16), pltpu.VMEM((HEAD_DIM, seq), BF16),
                        pltpu.VMEM((1, tq), F32), pltpu.VMEM((1, tq), F32),
                        pltpu.VMEM((HEAD_DIM, tq), F32)],
        compiler_params=_cparams("parallel", "parallel", "arbitrary"),
        name="fox_prompt",
    )(q, kv, kv, crep, crow)


def _fox_sample_kernel(*refs, pages_per_step, n_new, n_heads):
    pp = pages_per_step
    q_ref, knew_ref, vnew_ref, lfnew_ref, lfcol_ref = refs[1:6]
    k_refs = refs[6:6 + pp]
    v_refs = refs[6 + pp:6 + 2 * pp]
    lf_refs = refs[6 + 2 * pp:6 + 3 * pp]
    o_ref = refs[6 + 3 * pp]
    m_ref, l_ref, acc_ref, carry_ref, cn_ref = refs[7 + 3 * pp:]
    step = pl.program_id(1)
    page = k_refs[0].shape[1]
    n_rows = n_heads * Q_ROWS

    r = lax.broadcasted_iota(jnp.int32, (page, page), 0)
    c = lax.broadcasted_iota(jnp.int32, (page, page), 1)

    @pl.when(step == 0)
    def _():
        m_ref[...] = jnp.full(m_ref.shape, NEG_INF, F32)
        l_ref[...] = jnp.zeros(l_ref.shape, F32)
        acc_ref[...] = jnp.zeros(acc_ref.shape, F32)
        carry_ref[...] = jnp.zeros(carry_ref.shape, F32)
        lf_col = lfcol_ref[0]
        tok = lax.broadcasted_iota(jnp.int32, lf_col.shape, 0)
        c_col = jnp.zeros(lf_col.shape, F32)
        for j in range(n_new):
            c_col = c_col + jnp.where(tok >= j, lf_col[j:j + 1, :], 0.0)
        for hh in range(n_heads):
            cn_ref[hh * Q_ROWS:(hh + 1) * Q_ROWS, :] = c_col[:, hh:hh + 1]

    def per_head_rows(x16):
        return jnp.concatenate([jnp.broadcast_to(x16[hh:hh + 1, :], (Q_ROWS, page)) for hh in range(n_heads)],
                               axis=0)

    def head_major(x_ref):
        return pltpu.einshape("khd->hkd", x_ref[0].astype(BF16))

    def attend(k_ref, v_ref, bias, mask=None):
        kh = head_major(k_ref)
        vh = head_major(v_ref)
        s = jnp.concatenate([_dot_nt(q_ref[0, hh], kh[hh]) for hh in range(n_heads)], axis=0)
        s = s * ATTN_SCALE + bias
        if mask is not None:
            s = jnp.where(mask, s, NEG_INF)
        m_old = m_ref[...]
        m_new = jnp.maximum(m_old, jnp.max(s, axis=-1, keepdims=True))
        alpha = jnp.exp(m_old - m_new)
        p = jnp.exp(s - m_new)
        l_ref[...] = alpha * l_ref[...] + jnp.sum(p, axis=-1, keepdims=True)
        pv = jnp.concatenate(
            [_dot(p[hh * Q_ROWS:(hh + 1) * Q_ROWS, :].astype(BF16), vh[hh]) for hh in range(n_heads)], axis=0)
        acc_ref[...] = alpha * acc_ref[...] + pv
        m_ref[...] = m_new

    suffix = (r > c).astype(F32)
    for j in range(pp):
        lf = lf_refs[j][0]
        d16 = _dot_exact(lf, suffix) + carry_ref[...]
        carry_ref[...] = carry_ref[...] + jnp.sum(lf, axis=-1, keepdims=True)
        attend(k_refs[j], v_refs[j], cn_ref[...] + per_head_rows(d16))

    @pl.when(step == pl.num_programs(1) - 1)
    def _():
        c_row = _dot_exact(lfnew_ref[0], (r <= c).astype(F32))
        key = lax.broadcasted_iota(jnp.int32, (n_rows, page), 1)
        tok = lax.broadcasted_iota(jnp.int32, (n_rows, page), 0) % Q_ROWS
        attend(knew_ref, vnew_ref, cn_ref[...] - per_head_rows(c_row), mask=key <= tok)
        o_ref[0] = acc_ref[...] / l_ref[...]


def _fox_sample(q_rows, k_new, v_new, lf_new, lf_new_col, k_pool, v_pool, lf_pool, page_table, n_new):
    n, n_heads = q_rows.shape[:2]
    page = k_pool.shape[1]
    n_pages = page_table.shape[1]
    pp = _pick(n_pages, (4, 2, 1))
    n_steps = n_pages // pp
    n_rows = n_heads * Q_ROWS

    def kv_map(j):
        return lambda i, s, pt: (pt[i, n_pages - 1 - (s * pp + j)], 0, 0, 0)

    def lf_map(j):
        return lambda i, s, pt: (pt[i, n_pages - 1 - (s * pp + j)], 0, 0)

    req3 = lambda i, s, pt: (i, 0, 0)
    req4 = lambda i, s, pt: (i, 0, 0, 0)
    kv_block = (1, page, n_heads, HEAD_DIM)
    in_specs = [pl.BlockSpec((1, n_heads, Q_ROWS, HEAD_DIM), req4),
                pl.BlockSpec(kv_block, req4),
                pl.BlockSpec(kv_block, req4),
                pl.BlockSpec((1, HEAD_PAD, page), req3),
                pl.BlockSpec((1, Q_ROWS, LANES), req3)]
    in_specs += [pl.BlockSpec(kv_block, kv_map(j)) for j in range(pp)]
    in_specs += [pl.BlockSpec(kv_block, kv_map(j)) for j in range(pp)]
    in_specs += [pl.BlockSpec((1, HEAD_PAD, page), lf_map(j)) for j in range(pp)]
    grid_spec = pltpu.PrefetchScalarGridSpec(
        num_scalar_prefetch=1,
        grid=(n, n_steps),
        in_specs=in_specs,
        out_specs=pl.BlockSpec((1, n_rows, HEAD_DIM), req3),
        scratch_shapes=[pltpu.VMEM((n_rows, 1), F32), pltpu.VMEM((n_rows, 1), F32),
                        pltpu.VMEM((n_rows, HEAD_DIM), F32), pltpu.VMEM((HEAD_PAD, 1), F32),
                        pltpu.VMEM((n_rows, 1), F32)],
    )
    return pl.pallas_call(
        functools.partial(_fox_sample_kernel, pages_per_step=pp, n_new=n_new, n_heads=n_heads),
        grid_spec=grid_spec,
        out_shape=jax.ShapeDtypeStruct((n, n_rows, HEAD_DIM), F32),
        compiler_params=_cparams("parallel", "arbitrary"),
        name="fox_sample",
    )(page_table, q_rows, k_new, v_new, lf_new, lf_new_col,
      *([k_pool] * pp), *([v_pool] * pp), *([lf_pool] * pp))


def kernel(x_prompt, x_sample, cache_fox_k, cache_fox_v, cache_fox_logf, state_conv, cache_mem_k, cache_mem_v,
           page_table, mem_prompt, g_mix, g_ffn, g_mem, w_mem_kv, w_in_a, w_out_a, conv_w, conv_b, conv_ln_g,
           conv_ln_b, g_kv, w_kvf, b_f, w_in_b, w_out_b, w_ff_gate, w_ff_up, w_ff_down, w_router, b_router,
           w_e_gate, w_e_up, w_e_down, g_final):
    bsz, seq, d = x_prompt.shape
    n_dec, t_dec, _ = x_sample.shape
    depth = g_mix.shape[0]
    n_a = w_in_a.shape[0]
    conv_ch = conv_w.shape[-1]
    width = conv_w.shape[1]
    n_mem = mem_prompt.shape[1]
    fox_w = w_in_b.shape[-1] - MEM_WIDTH
    n_heads = fox_w // HEAD_DIM
    n_pool, page = cache_fox_k.shape[:2]
    tp = bsz * seq
    ts = n_dec * t_dec
    assert n_heads <= HEAD_PAD and width - 1 <= CONV_HALO and t_dec <= 8

    h = jnp.concatenate([x_prompt.reshape(tp, d), x_sample.reshape(ts, d)], axis=0)

    mem_flat = mem_prompt.reshape(bsz * n_mem, d)
    mem_k_p, mem_v_p = [], []
    for i in range(depth):
        z = _matmul([_rmsnorm(mem_flat, g_mem[i], BF16)], w_mem_kv[i], [0], 0, 2 * MEM_WIDTH)
        mem_k_p.append(z[:, :MEM_WIDTH].reshape(bsz, n_mem, MEM_WIDTH))
        mem_v_p.append(z[:, MEM_WIDTH:].reshape(bsz, n_mem, MEM_WIDTH))

    def mem_attend(q_mem, i):
        q_p = q_mem[:tp].reshape(bsz, seq, MEM_WIDTH)
        q_s = jnp.pad(q_mem[tp:].reshape(n_dec, t_dec, MEM_WIDTH), ((0, 0), (0, 8 - t_dec), (0, 0)))
        o_p = _mem_attn(q_p, mem_k_p[i], mem_v_p[i])
        o_s = _mem_attn(q_s, cache_mem_k[i].reshape(n_dec, n_mem, MEM_WIDTH),
                        cache_mem_v[i].reshape(n_dec, n_mem, MEM_WIDTH))
        return jnp.concatenate([o_p.reshape(tp, MEM_WIDTH), o_s[:, :t_dec].reshape(ts, MEM_WIDTH)], axis=0)

    new_conv_p, new_conv_s = [], []
    kv = None
    for i in range(depth):
        hn = _rmsnorm(h, g_mix[i], BF16)
        if i < n_a:
            u = _matmul_glu(hn, w_in_a[i], conv_ch)
            q_mem = _matmul([hn], w_in_a[i], [0], 2 * conv_ch, MEM_WIDTH)
            u_p = u[:tp].reshape(bsz, seq, conv_ch)
            u_s = u[tp:].reshape(n_dec, t_dec, conv_ch)
            ext_p = jnp.concatenate([jnp.zeros((bsz, width - 1, conv_ch), F32), u_p,
                                     jnp.zeros((bsz, CONV_HALO - (width - 1), conv_ch), F32)], axis=1)
            ext_s = jnp.concatenate([state_conv[i], u_s], axis=1)
            new_conv_p.append(ext_p[:, seq:seq + width - 1])
            new_conv_s.append(ext_s[:, t_dec:])
            mix_p = _conv_prompt(ext_p, conv_w[i], conv_b[i], conv_ln_g[i], conv_ln_b[i], seq)
            mix_s = _conv_sample(ext_s.transpose(1, 0, 2), conv_w[i], conv_b[i], conv_ln_g[i], conv_ln_b[i])
            mix = jnp.concatenate([mix_p.reshape(tp, conv_ch),
                                   mix_s.transpose(1, 0, 2).reshape(ts, conv_ch)], axis=0)
            w_out = w_out_a[i]
        else:
            j = i - n_a
            if kv is None:
                hk = _rmsnorm(h, g_kv, BF16)
                kv_all = _matmul([hk], w_kvf, [0], 0, 2 * fox_w)
                w_f = jnp.pad(w_kvf[:, 2 * fox_w:], ((0, 0), (0, LANES - n_heads)))
                b_f_pad = jnp.pad(b_f, (0, LANES - n_heads)).reshape(1, LANES)
                logf_pad = _matmul([hk], w_f, [0], 0, LANES, epilogue="logsig", extra=b_f_pad)
                crep, crow = _fox_cumsum(logf_pad[:tp].reshape(bsz, seq, LANES), n_heads)
                pad_rows = ((0, 0), (0, page - t_dec), (0, 0), (0, 0))
                k_new = jnp.pad(kv_all[tp:, :fox_w].reshape(n_dec, t_dec, n_heads, HEAD_DIM), pad_rows)
                v_new = jnp.pad(kv_all[tp:, fox_w:].reshape(n_dec, t_dec, n_heads, HEAD_DIM), pad_rows)
                lf_new = jnp.pad(logf_pad[tp:, :n_heads].reshape(n_dec, t_dec, n_heads).transpose(0, 2, 1),
                                 ((0, 0), (0, HEAD_PAD - n_heads), (0, page - t_dec)))
                lf_new_col = jnp.pad(logf_pad[tp:].reshape(n_dec, t_dec, LANES),
                                     ((0, 0), (0, Q_ROWS - t_dec), (0, 0)))
                lf_pool = jnp.pad(cache_fox_logf.astype(F32).transpose(0, 2, 1),
                                  ((0, 0), (0, HEAD_PAD - n_heads), (0, 0)))
                kv = (kv_all, logf_pad)
            q = _matmul([hn], w_in_b[j], [0], 0, fox_w)
            q_mem = _matmul([hn], w_in_b[j], [0], fox_w, MEM_WIDTH)
            mix_p = _fox_prompt(q, kv_all, crep, crow, bsz, seq, n_heads)
            q_rows = jnp.pad(q[tp:].reshape(n_dec, t_dec, n_heads, HEAD_DIM).transpose(0, 2, 1, 3),
                             ((0, 0), (0, 0), (0, Q_ROWS - t_dec), (0, 0))).astype(BF16)
            o_rows = _fox_sample(q_rows, k_new, v_new, lf_new, lf_new_col, cache_fox_k, cache_fox_v, lf_pool,
                                 page_table, t_dec)
            mix_s = o_rows.reshape(n_dec, n_heads, Q_ROWS, HEAD_DIM)[:, :, :t_dec].transpose(0, 2, 1, 3)
            mix = jnp.concatenate([mix_p, mix_s.reshape(ts, fox_w)], axis=0)
            w_out = w_out_b[j]
        mo = mem_attend(q_mem, i)
        h = _matmul([mix.astype(BF16), mo.astype(BF16)], w_out, [0, mix.shape[1]], 0, d,
                    epilogue="resid", extra=h)
        hf = _rmsnorm(h, g_ffn[i], BF16)
        jj = i // 2
        if i % 2 == 0:
            h = _ffn(hf, h, w_ff_gate[jj], w_ff_up[jj], w_ff_down[jj])
        else:
            h = _moe(hf, h, w_router[jj], b_router[jj], w_e_gate[jj], w_e_up[jj], w_e_down[jj])

    y = _rmsnorm(h, g_final, F32)
    kv_all, logf_pad = kv
    k_all = kv_all[:, :fox_w]
    v_all = kv_all[:, fox_w:]
    logf = logf_pad[:, :n_heads]
    return (y[:tp].reshape(bsz, seq, d),
            y[tp:].reshape(n_dec, t_dec, d),
            k_all[:tp].reshape(bsz, seq, n_heads, HEAD_DIM),
            v_all[:tp].reshape(bsz, seq, n_heads, HEAD_DIM),
            logf[:tp].reshape(bsz, seq, n_heads),
            jnp.stack(new_conv_p),
            jnp.stack([k.reshape(bsz, n_mem, MEM_HEADS, HEAD_DIM) for k in mem_k_p]),
            jnp.stack([v.reshape(bsz, n_mem, MEM_HEADS, HEAD_DIM) for v in mem_v_p]),
            k_all[tp:].reshape(n_dec, t_dec, n_heads, HEAD_DIM),
            v_all[tp:].reshape(n_dec, t_dec, n_heads, HEAD_DIM),
            logf[tp:].reshape(n_dec, t_dec, n_heads),
            jnp.stack(new_conv_s))
```

```python
import functools
import math

import jax
import jax.numpy as jnp
from jax import lax
from jax.experimental import pallas as pl
from jax.experimental.pallas import tpu as pltpu

HEAD_DIM = 128
MEM_HEADS = 4
MEM_WIDTH = MEM_HEADS * HEAD_DIM
TOP_K = 2
EPS = 1e-6
NEG_INF = -1e30
ATTN_SCALE = HEAD_DIM ** -0.5
LANES = 128
HEAD_PAD = 16
Q_ROWS = 8
SAMPLE_Q_PAD = 16
CONV_HALO = 32
VMEM_LIMIT_BYTES = 56 * 1024 * 1024
MOE_SUB = 256
MOE_ROW_TILE = 3 * MOE_SUB
MOE_UNSELECTED = -1.0e9

BF16 = jnp.bfloat16
F32 = jnp.float32


def _cparams(*semantics):
    return pltpu.CompilerParams(dimension_semantics=semantics, vmem_limit_bytes=VMEM_LIMIT_BYTES)


def _pick(n, candidates):
    for c in candidates:
        if c <= n and n % c == 0:
            return c
    return n


def _row_tile(t):
    return _pick(t, (1040, 1024, 640, 512, 256, 128, 64, 32, 16))


def _dot(a, b):
    return jnp.dot(a, b, preferred_element_type=F32)


def _dot_nt(a, b):
    return lax.dot_general(a, b, (((1,), (1,)), ((), ())), preferred_element_type=F32)


def _dot_exact(a, b):
    return jnp.dot(a, b, preferred_element_type=F32, precision=lax.Precision.HIGHEST)


def _log_sigmoid(x):
    return jnp.minimum(x, 0.0) - jnp.log1p(jnp.exp(-jnp.abs(x)))


def _rmsnorm_kernel(x_ref, g_ref, o_ref):
    x = x_ref[...]
    y = x * lax.rsqrt(jnp.mean(x * x, axis=-1, keepdims=True) + EPS)
    o_ref[...] = (y * g_ref[...]).astype(o_ref.dtype)


def _rmsnorm(x, g, out_dtype, row_start=0, n_rows=None):
    d = x.shape[1]
    n_rows = x.shape[0] if n_rows is None else n_rows
    tm = _pick(math.gcd(n_rows, row_start), (1024, 640, 512, 256, 128, 64, 32, 16))
    first = row_start // tm
    return pl.pallas_call(
        _rmsnorm_kernel,
        grid=(n_rows // tm,),
        in_specs=[pl.BlockSpec((tm, d), lambda i: (first + i, 0)),
                  pl.BlockSpec((1, d), lambda i: (0, 0))],
        out_specs=pl.BlockSpec((tm, d), lambda i: (i, 0)),
        out_shape=jax.ShapeDtypeStruct((n_rows, d), out_dtype),
        compiler_params=_cparams("parallel"),
        name="rmsnorm",
    )(x, g.reshape(1, d))


def _mm_kernel(*refs, n_a, epilogue):
    a_refs = refs[:n_a]
    w_refs = refs[n_a:2 * n_a]
    pos = 2 * n_a
    extra_ref = None
    if epilogue in ("resid", "logsig"):
        extra_ref = refs[pos]
        pos += 1
    o_ref = refs[pos]
    wb_refs = refs[pos + 1:pos + 1 + n_a]

    @pl.when(pl.program_id(1) == 0)
    def _():
        for w_ref, wb_ref in zip(w_refs, wb_refs):
            wb_ref[...] = w_ref[...].astype(BF16)

    acc = None
    for a_ref, wb_ref in zip(a_refs, wb_refs):
        d = _dot(a_ref[...], wb_ref[...])
        acc = d if acc is None else acc + d
    if epilogue == "resid":
        acc = extra_ref[...] + acc
    elif epilogue == "logsig":
        acc = _log_sigmoid(acc + extra_ref[...])
    if len(o_ref.shape) == 4:
        for j in range(o_ref.shape[1]):
            o_ref[0, j] = acc[:, j * HEAD_DIM:(j + 1) * HEAD_DIM].astype(o_ref.dtype)
    else:
        o_ref[...] = acc.astype(o_ref.dtype)


def _matmul(a_list, w, row_offs, col_off, n_out, *, epilogue="none", extra=None, out_dtype=F32,
            row_start=0, n_rows=None, heads_of=None):
    n_rows = a_list[0].shape[0] if n_rows is None else n_rows
    align = math.gcd(n_rows, row_start) if heads_of is None else math.gcd(heads_of[1], row_start)
    tm = _row_tile(align)
    tn = _pick(n_out, (512, 256, 128))
    first = row_start // tm
    n_a = len(a_list)
    in_specs, scratch = [], []
    for a in a_list:
        in_specs.append(pl.BlockSpec((tm, a.shape[1]), lambda n, m: (first + m, 0)))
    for a, ro in zip(a_list, row_offs):
        k = a.shape[1]
        assert ro % k == 0 and col_off % tn == 0
        in_specs.append(pl.BlockSpec((k, tn), functools.partial(
            lambda n, m, rb, cb: (rb, cb + n), rb=ro // k, cb=col_off // tn)))
        scratch.append(pltpu.VMEM((k, tn), BF16))
    args = list(a_list) + [w] * n_a
    if epilogue == "resid":
        in_specs.append(pl.BlockSpec((tm, tn), lambda n, m: (first + m, n)))
        args.append(extra)
    elif epilogue == "logsig":
        in_specs.append(pl.BlockSpec((1, tn), lambda n, m: (0, n)))
        args.append(extra)
    if heads_of is None:
        out_spec = pl.BlockSpec((tm, tn), lambda n, m: (m, n))
        out_shape = jax.ShapeDtypeStruct((n_rows, n_out), out_dtype)
    else:
        bsz, seq = heads_of
        per_seq = seq // tm
        out_spec = pl.BlockSpec((1, tn // HEAD_DIM, tm, HEAD_DIM),
                                lambda n, m: (m // per_seq, n, m % per_seq, 0))
        out_shape = jax.ShapeDtypeStruct((bsz, n_out // HEAD_DIM, seq, HEAD_DIM), out_dtype)
    return pl.pallas_call(
        functools.partial(_mm_kernel, n_a=n_a, epilogue=epilogue),
        grid=(n_out // tn, n_rows // tm),
        in_specs=in_specs,
        out_specs=out_spec,
        out_shape=out_shape,
        scratch_shapes=scratch,
        compiler_params=_cparams("arbitrary", "arbitrary"),
        name="matmul_" + epilogue,
    )(*args)


def _glu_kernel(a_ref, w1_ref, w2_ref, o_ref, wb1_ref, wb2_ref):
    @pl.when(pl.program_id(1) == 0)
    def _():
        wb1_ref[...] = w1_ref[...].astype(BF16)
        wb2_ref[...] = w2_ref[...].astype(BF16)

    a = a_ref[...]
    o_ref[...] = _dot(a, wb1_ref[...]) * jax.nn.sigmoid(_dot(a, wb2_ref[...]))


def _matmul_glu(a, w, n_out):
    t, k = a.shape
    tm = _row_tile(t)
    tn = _pick(n_out, (512, 256, 128))
    nb = n_out // tn
    return pl.pallas_call(
        _glu_kernel,
        grid=(nb, t // tm),
        in_specs=[pl.BlockSpec((tm, k), lambda n, m: (m, 0)),
                  pl.BlockSpec((k, tn), lambda n, m: (0, n)),
                  pl.BlockSpec((k, tn), lambda n, m: (0, n + nb))],
        out_specs=pl.BlockSpec((tm, tn), lambda n, m: (m, n)),
        out_shape=jax.ShapeDtypeStruct((t, n_out), F32),
        scratch_shapes=[pltpu.VMEM((k, tn), BF16), pltpu.VMEM((k, tn), BF16)],
        compiler_params=_cparams("arbitrary", "arbitrary"),
        name="matmul_glu",
    )(a, w, w)


def _swiglu_partial(x, wg, wu, wd):
    g = _dot(x, wg)
    u = _dot(x, wu)
    return _dot((g * jax.nn.sigmoid(g) * u).astype(BF16), wd)


def _ffn_kernel(x_ref, h_ref, wg_ref, wu_ref, wd_ref, o_ref):
    @pl.when(pl.program_id(1) == 0)
    def _():
        o_ref[...] = h_ref[...]

    o_ref[...] += _swiglu_partial(x_ref[...], wg_ref[...].astype(BF16), wu_ref[...].astype(BF16),
                                  wd_ref[...].astype(BF16))


def _ffn(x, h, w_gate, w_up, w_down):
    t, d = x.shape
    f = w_gate.shape[1]
    tm = _row_tile(t)
    tf = _pick(f, (256, 128))
    single = pl.Buffered(1)
    return pl.pallas_call(
        _ffn_kernel,
        grid=(t // tm, f // tf),
        in_specs=[pl.BlockSpec((tm, d), lambda m, j: (m, 0), pipeline_mode=single),
                  pl.BlockSpec((tm, d), lambda m, j: (m, 0), pipeline_mode=single),
                  pl.BlockSpec((d, tf), lambda m, j: (0, j)),
                  pl.BlockSpec((d, tf), lambda m, j: (0, j)),
                  pl.BlockSpec((tf, d), lambda m, j: (j, 0))],
        out_specs=pl.BlockSpec((tm, d), lambda m, j: (m, 0)),
        out_shape=jax.ShapeDtypeStruct((t, d), F32),
        compiler_params=_cparams("parallel", "arbitrary"),
        name="ffn",
    )(x, h, w_gate, w_up, w_down)


def _router_kernel(x_ref, w_ref, b_ref, o_ref, *, n_experts):
    logits = _dot(x_ref[...], w_ref[...].astype(BF16)) + b_ref[...]
    lane = lax.broadcasted_iota(jnp.int32, logits.shape, 1)
    logits = jnp.where(lane < n_experts, logits, NEG_INF)
    v1 = jnp.max(logits, axis=-1, keepdims=True)
    i1 = jnp.min(jnp.where(logits == v1, lane, LANES), axis=-1, keepdims=True)
    rest = jnp.where(lane == i1, NEG_INF, logits)
    v2 = jnp.max(rest, axis=-1, keepdims=True)
    i2 = jnp.min(jnp.where(rest == v2, lane, LANES), axis=-1, keepdims=True)
    e2 = jnp.exp(v2 - v1)
    p1 = 1.0 / (1.0 + e2)
    p2 = e2 / (1.0 + e2)
    o_ref[...] = jnp.where(lane == i1, p1, jnp.where(lane == i2, p2, 0.0))


def _router(x, w_router, b_router):
    t, d = x.shape
    n_e = w_router.shape[-1]
    tm = _row_tile(t)
    w_pad = jnp.pad(w_router, ((0, 0), (0, LANES - n_e)))
    b_pad = jnp.pad(b_router, (0, LANES - n_e)).reshape(1, LANES)
    return pl.pallas_call(
        functools.partial(_router_kernel, n_experts=n_e),
        grid=(t // tm,),
        in_specs=[pl.BlockSpec((tm, d), lambda m: (m, 0)),
                  pl.BlockSpec((d, LANES), lambda m: (0, 0)),
                  pl.BlockSpec((1, LANES), lambda m: (0, 0))],
        out_specs=pl.BlockSpec((tm, LANES), lambda m: (m, 0)),
        out_shape=jax.ShapeDtypeStruct((t, LANES), F32),
        compiler_params=_cparams("parallel"),
        name="router",
    )(x, w_pad, b_pad)


def _moe_rank_kernel(g_ref, pos_ref, pos_t_ref, start_ref, cnt_ref, carry_ref):
    i = pl.program_id(0)
    tm = g_ref.shape[0]

    @pl.when(i == 0)
    def _():
        carry_ref[...] = jnp.zeros(carry_ref.shape, F32)

    sel = g_ref[...] > 0.0
    r = lax.broadcasted_iota(jnp.int32, (tm, tm), 0)
    c = lax.broadcasted_iota(jnp.int32, (tm, tm), 1)
    before = jnp.where(c < r, 1.0, 0.0).astype(BF16)
    ones = jnp.where(sel, 1.0, 0.0)
    carry = carry_ref[...]
    rank = _dot(before, ones.astype(BF16)) + carry
    start_ref[...] = jnp.broadcast_to(carry, start_ref.shape)
    carry = rank[tm - 1:tm, :] + ones[tm - 1:tm, :]
    carry_ref[...] = carry
    cnt_ref[...] = jnp.broadcast_to(carry, cnt_ref.shape)
    pos = jnp.where(sel, rank, MOE_UNSELECTED)
    pos_ref[...] = pos
    pos_t_ref[...] = pos.T[0:HEAD_PAD, :]


def _moe_rank(gates, tc):
    t = gates.shape[0]
    nch = t // tc
    return pl.pallas_call(
        _moe_rank_kernel,
        grid=(nch,),
        in_specs=[pl.BlockSpec((tc, LANES), lambda i: (i, 0))],
        out_specs=[pl.BlockSpec((tc, LANES), lambda i: (i, 0)),
                   pl.BlockSpec((HEAD_PAD, tc), lambda i: (0, i)),
                   pl.BlockSpec((8, LANES), lambda i: (i, 0)),
                   pl.BlockSpec((8, LANES), lambda i: (0, 0))],
        out_shape=[jax.ShapeDtypeStruct((t, LANES), F32),
                   jax.ShapeDtypeStruct((HEAD_PAD, t), F32),
                   jax.ShapeDtypeStruct((nch * 8, LANES), F32),
                   jax.ShapeDtypeStruct((8, LANES), F32)],
        scratch_shapes=[pltpu.VMEM((1, LANES), F32)],
        compiler_params=_cparams("arbitrary"),
        name="moe_rank",
    )(gates)


def _moe_gather_kernel(sb_ref, ch_ref, first_ref, last_ref, valid_ref, esb_ref, base_ref,
                       pos_t_ref, x_ref, o_ref, acc_ref):
    i = pl.program_id(0)

    @pl.when(valid_ref[i] == 1)
    def _():
        sb = sb_ref[i]
        local = pos_t_ref[pl.ds(esb_ref[sb], 1), :].astype(jnp.int32) + base_ref[sb]
        rows = lax.broadcasted_iota(jnp.int32, (o_ref.shape[0], local.shape[1]), 0)
        onehot = jnp.where(local == rows, 1.0, 0.0).astype(BF16)
        part = _dot(onehot, x_ref[...])

        @pl.when(first_ref[i] == 1)
        def _():
            acc_ref[...] = part

        @pl.when(first_ref[i] == 0)
        def _():
            acc_ref[...] += part

        @pl.when(last_ref[i] == 1)
        def _():
            o_ref[...] = acc_ref[...].astype(o_ref.dtype)


def _moe_gather(plan, pos_t, x, n_rows, tc):
    d = x.shape[1]
    sb = MOE_SUB
    prefetch = (plan["g_sb"], plan["g_ch"], plan["g_first"], plan["g_last"], plan["g_valid"],
                plan["e_sb"], plan["base_sb"])
    grid_spec = pltpu.PrefetchScalarGridSpec(
        num_scalar_prefetch=len(prefetch),
        grid=(plan["g_sb"].shape[0],),
        in_specs=[pl.BlockSpec((HEAD_PAD, tc), lambda i, sbl, chl, *_: (0, chl[i])),
                  pl.BlockSpec((tc, d), lambda i, sbl, chl, *_: (chl[i], 0))],
        out_specs=pl.BlockSpec((sb, d), lambda i, sbl, chl, *_: (sbl[i], 0)),
        scratch_shapes=[pltpu.VMEM((sb, d), F32)],
    )
    return pl.pallas_call(
        _moe_gather_kernel,
        grid_spec=grid_spec,
        out_shape=jax.ShapeDtypeStruct((n_rows, d), BF16),
        compiler_params=_cparams("arbitrary"),
        name="moe_gather",
    )(*prefetch, pos_t, x)


def _moe_ffn_kernel(et_ref, nv_ref, x_ref, wg_ref, wu_ref, wd_ref, o_ref, acc_ref):
    del et_ref
    r = pl.program_id(0)
    f = pl.program_id(1)
    last_f = pl.num_programs(1) - 1
    n_valid = nv_ref[r]
    n_sub = o_ref.shape[0] // MOE_SUB

    @pl.when(f == 0)
    def _():
        acc_ref[...] = jnp.zeros(acc_ref.shape, F32)

    for n in range(1, n_sub + 1):
        @pl.when(n_valid == n)
        def _():
            rows = slice(0, n * MOE_SUB)
            acc_ref[rows, :] += _swiglu_partial(x_ref[rows, :], wg_ref[0].astype(BF16), wu_ref[0].astype(BF16),
                                                wd_ref[0].astype(BF16))

    @pl.when(f == last_f)
    def _():
        o_ref[...] = acc_ref[...].astype(o_ref.dtype)


def _moe_ffn(plan, xs, w_gate, w_up, w_down):
    n_rows, d = xs.shape
    f = w_gate.shape[2]
    tr = MOE_ROW_TILE
    tf = _pick(f, (256, 128))
    nf = f // tf

    def w_in_map(r, j, et, nv):
        return (et[r], 0, jnp.where(nv[r] > 0, j, nf - 1))

    def w_out_map(r, j, et, nv):
        return (et[r], jnp.where(nv[r] > 0, j, nf - 1), 0)

    grid_spec = pltpu.PrefetchScalarGridSpec(
        num_scalar_prefetch=2,
        grid=(n_rows // tr, nf),
        in_specs=[pl.BlockSpec((tr, d), lambda r, j, et, nv: (r, 0)),
                  pl.BlockSpec((1, d, tf), w_in_map),
                  pl.BlockSpec((1, d, tf), w_in_map),
                  pl.BlockSpec((1, tf, d), w_out_map)],
        out_specs=pl.BlockSpec((tr, d), lambda r, j, et, nv: (r, 0)),
        scratch_shapes=[pltpu.VMEM((tr, d), F32)],
    )
    return pl.pallas_call(
        _moe_ffn_kernel,
        grid_spec=grid_spec,
        out_shape=jax.ShapeDtypeStruct((n_rows, d), BF16),
        compiler_params=_cparams("arbitrary", "arbitrary"),
        name="moe_ffn",
    )(plan["e_tile"], plan["nv_tile"], xs, w_gate, w_up, w_down)


def _moe_combine_kernel(ch_ref, sb_ref, first_ref, valid_ref, esb_ref, base_ref,
                        pos_ref, gate_ref, h_ref, y_ref, o_ref):
    i = pl.program_id(0)

    @pl.when(first_ref[i] == 1)
    def _():
        o_ref[...] = h_ref[...]

    @pl.when(valid_ref[i] == 1)
    def _():
        sb = sb_ref[i]
        lane = lax.broadcasted_iota(jnp.int32, pos_ref.shape, 1)
        mine = lane == esb_ref[sb]
        rank = jnp.sum(jnp.where(mine, pos_ref[...], 0.0), axis=-1, keepdims=True)
        local = rank.astype(jnp.int32) + base_ref[sb]
        gate = jnp.sum(jnp.where(mine, gate_ref[...], 0.0), axis=-1, keepdims=True)
        cols = lax.broadcasted_iota(jnp.int32, (pos_ref.shape[0], y_ref.shape[0]), 1)
        onehot = jnp.where(local == cols, 1.0, 0.0).astype(BF16)
        o_ref[...] += gate * _dot(onehot, y_ref[...])


def _moe_combine(plan, pos, gates, h, ys, tc):
    t, d = h.shape
    sb = MOE_SUB
    prefetch = (plan["c_ch"], plan["c_sb"], plan["c_first"], plan["c_valid"], plan["e_sb"], plan["base_sb"])
    grid_spec = pltpu.PrefetchScalarGridSpec(
        num_scalar_prefetch=len(prefetch),
        grid=(plan["c_ch"].shape[0],),
        in_specs=[pl.BlockSpec((tc, LANES), lambda i, chl, sbl, *_: (chl[i], 0)),
                  pl.BlockSpec((tc, LANES), lambda i, chl, sbl, *_: (chl[i], 0)),
                  pl.BlockSpec((tc, d), lambda i, chl, sbl, *_: (chl[i], 0)),
                  pl.BlockSpec((sb, d), lambda i, chl, sbl, *_: (sbl[i], 0))],
        out_specs=pl.BlockSpec((tc, d), lambda i, chl, sbl, *_: (chl[i], 0)),
    )
    return pl.pallas_call(
        _moe_combine_kernel,
        grid_spec=grid_spec,
        out_shape=jax.ShapeDtypeStruct((t, d), F32),
        compiler_params=_cparams("arbitrary"),
        name="moe_combine",
    )(*prefetch, pos, gates, h, ys)


def _pair_list(overlap, length):
    n_b = overlap.shape[1]
    flat = overlap.reshape(-1)
    n = jnp.sum(flat.astype(jnp.int32))
    idx = jnp.nonzero(flat, size=length, fill_value=0)[0].astype(jnp.int32)
    step = jnp.arange(length, dtype=jnp.int32)
    valid = step < n
    idx = jnp.where(valid, idx, idx[jnp.maximum(n - 1, 0)])
    a = idx // n_b
    b = idx % n_b
    nxt_valid = jnp.concatenate([valid[1:], jnp.zeros((1,), bool)])
    first = valid & (a != jnp.concatenate([jnp.full((1,), -1, jnp.int32), a[:-1]]))
    last = valid & ((a != jnp.concatenate([a[1:], jnp.full((1,), -1, jnp.int32)])) | ~nxt_valid)
    as_i32 = lambda v: v.astype(jnp.int32)
    return a, b, as_i32(first), as_i32(last), as_i32(valid)


def _moe_plan(counts, chunk_start, n_rows, nch):
    n_e = counts.shape[0]
    tr, sb = MOE_ROW_TILE, MOE_SUB
    seg = (counts + tr - 1) // tr * tr
    seg_end = jnp.cumsum(seg)
    off = seg_end - seg
    chunk_end = jnp.concatenate([chunk_start[:, 1:], counts[:, None]], axis=1)

    def owner(start):
        return jnp.clip(jnp.searchsorted(seg_end, start, side="right"), 0, n_e - 1).astype(jnp.int32)

    sb_start = jnp.arange(n_rows // sb, dtype=jnp.int32) * sb
    e_sb = owner(sb_start)
    k0 = sb_start - off[e_sb]
    k1 = jnp.minimum(k0 + sb, counts[e_sb])
    sb_used = (sb_start < seg_end[-1]) & (k0 < counts[e_sb])
    overlap = sb_used[:, None] & (chunk_start[e_sb] < k1[:, None]) & (chunk_end[e_sb] > k0[:, None])
    length = n_rows // sb + n_e * nch
    g_sb, g_ch, g_first, g_last, g_valid = _pair_list(overlap, length)
    c_ch, c_sb, c_first, _, c_valid = _pair_list(overlap.T, length)

    tile_start = jnp.arange(n_rows // tr, dtype=jnp.int32) * tr
    e_tile = owner(tile_start)
    left = counts[e_tile] - (tile_start - off[e_tile])
    nv_tile = jnp.where(tile_start < seg_end[-1], jnp.clip((left + sb - 1) // sb, 0, tr // sb), 0)
    return dict(g_sb=g_sb, g_ch=g_ch, g_first=g_first, g_last=g_last, g_valid=g_valid,
                c_ch=c_ch, c_sb=c_sb, c_first=c_first, c_valid=c_valid,
                e_sb=e_sb, base_sb=(off[e_sb] - sb_start).astype(jnp.int32),
                e_tile=e_tile, nv_tile=nv_tile.astype(jnp.int32))


def _moe(x, h, w_router, b_router, w_gate, w_up, w_down):
    t = x.shape[0]
    n_e = w_router.shape[-1]
    tc = _pick(t, (640, 512, 384, 256, 128))
    assert tc % LANES == 0 and t % tc == 0 and n_e <= 8
    nch = t // tc
    n_rows = -(-(TOP_K * t + n_e * (MOE_ROW_TILE - 1)) // MOE_ROW_TILE) * MOE_ROW_TILE
    gates = _router(x, w_router, b_router)
    pos, pos_t, starts, totals = _moe_rank(gates, tc)
    counts = totals[0, :n_e].astype(jnp.int32)
    chunk_start = starts.reshape(nch, 8, LANES)[:, 0, :n_e].T.astype(jnp.int32)
    plan = _moe_plan(counts, chunk_start, n_rows, nch)
    xs = _moe_gather(plan, pos_t, x, n_rows, tc)
    ys = _moe_ffn(plan, xs, w_gate, w_up, w_down)
    return _moe_combine(plan, pos, gates, h, ys, tc)


def _ln_swish(hc, lg, lb):
    mu = jnp.mean(hc, axis=-1, keepdims=True)
    xc = hc - mu
    var = jnp.mean(xc * xc, axis=-1, keepdims=True)
    y = xc * lax.rsqrt(var + EPS) * lg + lb
    return y * jax.nn.sigmoid(y)


def _conv_kernel(prev_ref, main_ref, w_ref, b_ref, lg_ref, lb_ref, o_ref, ext_ref, acc_ref, *, tt, width):
    @pl.when(pl.program_id(1) == 0)
    def _():
        ext_ref[0:CONV_HALO, :] = jnp.zeros((CONV_HALO, ext_ref.shape[1]), F32)

    @pl.when(pl.program_id(1) > 0)
    def _():
        ext_ref[0:CONV_HALO, :] = prev_ref[...]

    ext_ref[CONV_HALO:CONV_HALO + tt, :] = main_ref[...]
    n_chunks = o_ref.shape[-1] // LANES
    lead = CONV_HALO - (width - 1)

    def chunk(c, carry):
        c0 = pl.multiple_of(c * LANES, LANES)
        acc = jnp.zeros((tt, LANES), F32)
        for j in range(width):
            acc = acc + ext_ref[pl.ds(lead + j, tt), pl.ds(c0, LANES)] * w_ref[pl.ds(j, 1), pl.ds(c0, LANES)]
        acc_ref[:, pl.ds(c0, LANES)] = acc + b_ref[:, pl.ds(c0, LANES)]
        return carry

    lax.fori_loop(0, n_chunks, chunk, 0)
    o_ref[...] = _ln_swish(acc_ref[...], lg_ref[...], lb_ref[...]).astype(o_ref.dtype)


def _conv_prompt(u, w, b, lg, lb, bsz, seq):
    c = u.shape[1]
    width = w.shape[0]
    tt = _pick(seq, (256, 128, 64, 32))
    nt = seq // tt
    per_halo = tt // CONV_HALO
    return pl.pallas_call(
        functools.partial(_conv_kernel, tt=tt, width=width),
        grid=(bsz, nt),
        in_specs=[pl.BlockSpec((CONV_HALO, c), lambda bb, i: (jnp.maximum((bb * nt + i) * per_halo - 1, 0), 0)),
                  pl.BlockSpec((tt, c), lambda bb, i: (bb * nt + i, 0)),
                  pl.BlockSpec((width, c), lambda bb, i: (0, 0)),
                  pl.BlockSpec((1, c), lambda bb, i: (0, 0)),
                  pl.BlockSpec((1, c), lambda bb, i: (0, 0)),
                  pl.BlockSpec((1, c), lambda bb, i: (0, 0))],
        out_specs=pl.BlockSpec((tt, c), lambda bb, i: (bb * nt + i, 0)),
        out_shape=jax.ShapeDtypeStruct((bsz * seq, c), BF16),
        scratch_shapes=[pltpu.VMEM((tt + CONV_HALO, c), F32), pltpu.VMEM((tt, c), F32)],
        compiler_params=_cparams("parallel", "parallel"),
        name="conv_prompt",
    )(u, u, w, b.reshape(1, c), lg.reshape(1, c), lb.reshape(1, c))


def _conv_step_kernel(ext_ref, w_ref, b_ref, lg_ref, lb_ref, o_ref, *, width):
    n_t = o_ref.shape[0]
    for t in range(n_t):
        acc = ext_ref[t] * w_ref[0:1, :]
        for j in range(1, width):
            acc = acc + ext_ref[t + j] * w_ref[j:j + 1, :]
        o_ref[t] = _ln_swish(acc + b_ref[...], lg_ref[...], lb_ref[...]).astype(o_ref.dtype)


def _conv_sample(ext_t, w, b, lg, lb):
    l_ext, n, c = ext_t.shape
    width = w.shape[0]
    n_t = l_ext - (width - 1)
    return pl.pallas_call(
        functools.partial(_conv_step_kernel, width=width),
        out_shape=jax.ShapeDtypeStruct((n_t, n, c), BF16),
        compiler_params=pltpu.CompilerParams(vmem_limit_bytes=VMEM_LIMIT_BYTES),
        name="conv_sample",
    )(ext_t, w, b.reshape(1, c), lg.reshape(1, c), lb.reshape(1, c))


def _mem_attn_kernel(q_ref, k_ref, v_ref, o_ref):
    for hh in range(MEM_HEADS):
        sl = slice(hh * HEAD_DIM, (hh + 1) * HEAD_DIM)
        q = q_ref[:, sl].astype(BF16)
        k = k_ref[0, hh].astype(BF16)
        v = v_ref[0, hh].astype(BF16)
        s = _dot_nt(q, k) * ATTN_SCALE
        m = jnp.max(s, axis=-1, keepdims=True)
        p = jnp.exp(s - m)
        l = jnp.sum(p, axis=-1, keepdims=True)
        o_ref[:, sl] = (_dot(p.astype(BF16), v) / l).astype(o_ref.dtype)


def _mem_attn(q, k, v, rows_per_req):
    w = q.shape[1]
    n, _, m, _ = k.shape
    tq = _pick(rows_per_req, (512, 256, 128, 64, 32, 16))
    per_req = rows_per_req // tq
    return pl.pallas_call(
        _mem_attn_kernel,
        grid=(n, per_req),
        in_specs=[pl.BlockSpec((tq, w), lambda i, j: (i * per_req + j, 0)),
                  pl.BlockSpec((1, MEM_HEADS, m, HEAD_DIM), lambda i, j: (i, 0, 0, 0)),
                  pl.BlockSpec((1, MEM_HEADS, m, HEAD_DIM), lambda i, j: (i, 0, 0, 0))],
        out_specs=pl.BlockSpec((tq, w), lambda i, j: (i * per_req + j, 0)),
        out_shape=jax.ShapeDtypeStruct((n * rows_per_req, w), BF16),
        compiler_params=_cparams("parallel", "parallel"),
        name="mem_attn",
    )(q, k, v)


def _cumsum_kernel(lf_ref, crep_ref, crow_ref, *, blk, n_heads):
    seq = lf_ref.shape[1]
    r = lax.broadcasted_iota(jnp.int32, (blk, blk), 0)
    c = lax.broadcasted_iota(jnp.int32, (blk, blk), 1)
    tri = (c <= r).astype(F32)
    carry = jnp.zeros((1, LANES), F32)
    for i in range(seq // blk):
        rows = slice(i * blk, (i + 1) * blk)
        cs = _dot_exact(tri, lf_ref[0, rows, :]) + carry
        carry = cs[blk - 1:blk, :]
        crow_ref[0, :, rows] = cs.T[0:HEAD_PAD, :]
        for hh in range(n_heads):
            crep_ref[0, hh, rows, :] = jnp.broadcast_to(cs[:, hh:hh + 1], (blk, LANES))


def _fox_cumsum(logf_pad, n_heads):
    bsz, seq, _ = logf_pad.shape
    blk = _pick(seq, (256, 128))
    return pl.pallas_call(
        functools.partial(_cumsum_kernel, blk=blk, n_heads=n_heads),
        grid=(bsz,),
        in_specs=[pl.BlockSpec((1, seq, LANES), lambda b: (b, 0, 0))],
        out_specs=[pl.BlockSpec((1, n_heads, seq, LANES), lambda b: (b, 0, 0, 0)),
                   pl.BlockSpec((1, HEAD_PAD, seq), lambda b: (b, 0, 0))],
        out_shape=[jax.ShapeDtypeStruct((bsz, n_heads, seq, LANES), F32),
                   jax.ShapeDtypeStruct((bsz, HEAD_PAD, seq), F32)],
        compiler_params=_cparams("parallel"),
        name="fox_cumsum",
    )(logf_pad)


def _fox_prompt_kernel(q_ref, k_ref, v_ref, crep_ref, crow_ref, o_ref,
                       kb_ref, vt_ref, m_ref, l_ref, acc_ref, *, tq, tk):
    hh = pl.program_id(1)
    qi = pl.program_id(2)

    @pl.when(qi == 0)
    def _():
        kb_ref[...] = k_ref[0, 0].astype(BF16)
        vt_ref[...] = v_ref[0, 0].T.astype(BF16)

    q = q_ref[...].astype(BF16)
    q0 = pl.multiple_of(qi * tq, tq)
    cq = crow_ref[0, pl.ds(hh, 1), pl.ds(q0, tq)]
    m_ref[...] = jnp.full(m_ref.shape, NEG_INF, F32)
    l_ref[...] = jnp.zeros(l_ref.shape, F32)
    acc_ref[...] = jnp.zeros(acc_ref.shape, F32)
    q_pos = q0 + lax.broadcasted_iota(jnp.int32, (tk, tq), 1)
    k_iota = lax.broadcasted_iota(jnp.int32, (tk, tq), 0)

    def body(kj, carry):
        k0 = pl.multiple_of(kj * tk, tk)
        ck = jnp.tile(crep_ref[0, 0, pl.ds(k0, tk), :], (1, tq // LANES))
        s = _dot_nt(kb_ref[pl.ds(k0, tk), :], q) * ATTN_SCALE + (cq - ck)
        s = jnp.where(k_iota + k0 <= q_pos, s, NEG_INF)
        m_old = m_ref[...]
        m_new = jnp.maximum(m_old, jnp.max(s, axis=0, keepdims=True))
        alpha = jnp.exp(m_old - m_new)
        p = jnp.exp(s - m_new)
        l_ref[...] = alpha * l_ref[...] + jnp.sum(p, axis=0, keepdims=True)
        acc_ref[...] = alpha * acc_ref[...] + _dot(vt_ref[:, pl.ds(k0, tk)], p.astype(BF16))
        m_ref[...] = m_new
        return carry

    lax.fori_loop(0, (qi + 1) * (tq // tk), body, 0)
    o_ref[...] = (acc_ref[...] / l_ref[...]).T.astype(o_ref.dtype)


def _fox_prompt(q, k, v, crep, crow):
    bsz, n_heads, seq, _ = k.shape
    tq = _pick(seq, (512, 256, 128))
    tk = _pick(tq, (256, 128))
    nq = seq // tq
    head_seq = pl.BlockSpec((1, 1, seq, HEAD_DIM), lambda b, h, i: (b, h, 0, 0))
    return pl.pallas_call(
        functools.partial(_fox_prompt_kernel, tq=tq, tk=tk),
        grid=(bsz, n_heads, nq),
        in_specs=[pl.BlockSpec((tq, HEAD_DIM), lambda b, h, i: (b * nq + i, h)),
                  head_seq,
                  head_seq,
                  pl.BlockSpec((1, 1, seq, LANES), lambda b, h, i: (b, h, 0, 0)),
                  pl.BlockSpec((1, HEAD_PAD, seq), lambda b, h, i: (b, 0, 0))],
        out_specs=pl.BlockSpec((tq, HEAD_DIM), lambda b, h, i: (b * nq + i, h)),
        out_shape=jax.ShapeDtypeStruct((bsz * seq, n_heads * HEAD_DIM), BF16),
        scratch_shapes=[pltpu.VMEM((seq, HEAD_DIM), BF16), pltpu.VMEM((HEAD_DIM, seq), BF16),
                        pltpu.VMEM((1, tq), F32), pltpu.VMEM((1, tq), F32),
                        pltpu.VMEM((HEAD_DIM, tq), F32)],
        compiler_params=_cparams("parallel", "parallel", "arbitrary"),
        name="fox_prompt",
    )(q, k, v, crep, crow)


def _fox_sample_kernel(*refs, pages_per_step, n_new, n_heads):
    pp = pages_per_step
    q_ref, knew_ref, vnew_ref, lfnew_ref, lfcol_ref = refs[1:6]
    k_refs = refs[6:6 + pp]
    v_refs = refs[6 + pp:6 + 2 * pp]
    lf_refs = refs[6 + 2 * pp:6 + 3 * pp]
    o_ref = refs[6 + 3 * pp]
    m_ref, l_ref, acc_ref, carry_ref, cn_ref = refs[7 + 3 * pp:]
    step = pl.program_id(1)
    page = k_refs[0].shape[2]
    n_rows = n_heads * Q_ROWS

    r = lax.broadcasted_iota(jnp.int32, (page, page), 0)
    c = lax.broadcasted_iota(jnp.int32, (page, page), 1)

    @pl.when(step == 0)
    def _():
        m_ref[...] = jnp.full(m_ref.shape, NEG_INF, F32)
        l_ref[...] = jnp.zeros(l_ref.shape, F32)
        acc_ref[...] = jnp.zeros(acc_ref.shape, F32)
        carry_ref[...] = jnp.zeros(carry_ref.shape, F32)
        lf_col = lfcol_ref[0]
        tok = lax.broadcasted_iota(jnp.int32, lf_col.shape, 0)
        c_col = jnp.zeros(lf_col.shape, F32)
        for j in range(n_new):
            c_col = c_col + jnp.where(tok >= j, lf_col[j:j + 1, :], 0.0)
        for hh in range(n_heads):
            cn_ref[hh * Q_ROWS:(hh + 1) * Q_ROWS, :] = c_col[:, hh:hh + 1]

    def per_head_rows(x16):
        return jnp.concatenate([jnp.broadcast_to(x16[hh:hh + 1, :], (Q_ROWS, page)) for hh in range(n_heads)],
                               axis=0)

    def attend(k_list, v_list, bias, mask=None):
        s = jnp.concatenate(
            [jnp.concatenate([_dot_nt(q_ref[0, hh], k_ref[0, hh].astype(BF16)) for hh in range(n_heads)], axis=0)
             for k_ref in k_list], axis=1)
        s = s * ATTN_SCALE + bias
        if mask is not None:
            s = jnp.where(mask, s, NEG_INF)
        m_old = m_ref[...]
        m_new = jnp.maximum(m_old, jnp.max(s, axis=-1, keepdims=True))
        alpha = jnp.exp(m_old - m_new)
        p = jnp.exp(s - m_new)
        l_ref[...] = alpha * l_ref[...] + jnp.sum(p, axis=-1, keepdims=True)
        pv = None
        for j, v_ref in enumerate(v_list):
            part = jnp.concatenate(
                [_dot(p[hh * Q_ROWS:(hh + 1) * Q_ROWS, j * page:(j + 1) * page].astype(BF16),
                      v_ref[0, hh].astype(BF16)) for hh in range(n_heads)], axis=0)
            pv = part if pv is None else pv + part
        acc_ref[...] = alpha * acc_ref[...] + pv
        m_ref[...] = m_new

    suffix = (r > c).astype(F32)
    carry = carry_ref[...]
    decay = []
    for j in range(pp):
        lf = lf_refs[j][0]
        decay.append(per_head_rows(_dot_exact(lf, suffix) + carry))
        carry = carry + jnp.sum(lf, axis=-1, keepdims=True)
    carry_ref[...] = carry
    attend(k_refs, v_refs, cn_ref[...] + jnp.concatenate(decay, axis=1))

    @pl.when(step == pl.num_programs(1) - 1)
    def _():
        c_row = _dot_exact(lfnew_ref[0], (r <= c).astype(F32))
        key = lax.broadcasted_iota(jnp.int32, (n_rows, page), 1)
        tok = lax.broadcasted_iota(jnp.int32, (n_rows, page), 0) % Q_ROWS
        attend([knew_ref], [vnew_ref], cn_ref[...] - per_head_rows(c_row), mask=key <= tok)
        o_ref[0] = acc_ref[...] / l_ref[...]


def _fox_sample(q_rows, k_new, v_new, lf_new, lf_new_col, k_pool, v_pool, lf_pool, page_table, n_new):
    n, n_heads = q_rows.shape[:2]
    page = k_pool.shape[2]
    n_pages = page_table.shape[1]
    pp = _pick(n_pages, (8, 4, 2, 1))
    n_steps = n_pages // pp
    n_rows = n_heads * Q_ROWS

    def kv_map(j):
        return lambda i, s, pt: (pt[i, n_pages - 1 - (s * pp + j)], 0, 0, 0)

    def lf_map(j):
        return lambda i, s, pt: (pt[i, n_pages - 1 - (s * pp + j)], 0, 0)

    req3 = lambda i, s, pt: (i, 0, 0)
    req4 = lambda i, s, pt: (i, 0, 0, 0)
    kv_block = (1, n_heads, page, HEAD_DIM)
    in_specs = [pl.BlockSpec((1, n_heads, Q_ROWS, HEAD_DIM), req4),
                pl.BlockSpec(kv_block, req4),
                pl.BlockSpec(kv_block, req4),
                pl.BlockSpec((1, HEAD_PAD, page), req3),
                pl.BlockSpec((1, Q_ROWS, LANES), req3)]
    in_specs += [pl.BlockSpec(kv_block, kv_map(j)) for j in range(pp)]
    in_specs += [pl.BlockSpec(kv_block, kv_map(j)) for j in range(pp)]
    in_specs += [pl.BlockSpec((1, HEAD_PAD, page), lf_map(j)) for j in range(pp)]
    grid_spec = pltpu.PrefetchScalarGridSpec(
        num_scalar_prefetch=1,
        grid=(n, n_steps),
        in_specs=in_specs,
        out_specs=pl.BlockSpec((1, n_rows, HEAD_DIM), req3),
        scratch_shapes=[pltpu.VMEM((n_rows, 1), F32), pltpu.VMEM((n_rows, 1), F32),
                        pltpu.VMEM((n_rows, HEAD_DIM), F32), pltpu.VMEM((HEAD_PAD, 1), F32),
                        pltpu.VMEM((n_rows, 1), F32)],
    )
    return pl.pallas_call(
        functools.partial(_fox_sample_kernel, pages_per_step=pp, n_new=n_new, n_heads=n_heads),
        grid_spec=grid_spec,
        out_shape=jax.ShapeDtypeStruct((n, n_rows, HEAD_DIM), F32),
        compiler_params=_cparams("parallel", "arbitrary"),
        name="fox_sample",
    )(page_table, q_rows, k_new, v_new, lf_new, lf_new_col,
      *([k_pool] * pp), *([v_pool] * pp), *([lf_pool] * pp))


def kernel(x_prompt, x_sample, cache_fox_k, cache_fox_v, cache_fox_logf, state_conv, cache_mem_k, cache_mem_v,
           page_table, mem_prompt, g_mix, g_ffn, g_mem, w_mem_kv, w_in_a, w_out_a, conv_w, conv_b, conv_ln_g,
           conv_ln_b, g_kv, w_kvf, b_f, w_in_b, w_out_b, w_ff_gate, w_ff_up, w_ff_down, w_router, b_router,
           w_e_gate, w_e_up, w_e_down, g_final):
    bsz, seq, d = x_prompt.shape
    n_dec, t_dec, _ = x_sample.shape
    depth = g_mix.shape[0]
    n_a = w_in_a.shape[0]
    conv_ch = conv_w.shape[-1]
    width = conv_w.shape[1]
    n_mem = mem_prompt.shape[1]
    fox_w = w_in_b.shape[-1] - MEM_WIDTH
    n_heads = fox_w // HEAD_DIM
    n_pool, page = cache_fox_k.shape[:2]
    tp = bsz * seq
    ts = n_dec * t_dec
    assert n_heads <= HEAD_PAD and width - 1 <= CONV_HALO and t_dec <= 8

    h = jnp.concatenate([x_prompt.reshape(tp, d), x_sample.reshape(ts, d)], axis=0)

    mem_flat = mem_prompt.reshape(bsz * n_mem, d)
    mem_k_p, mem_v_p = [], []
    for i in range(depth):
        z = _matmul([_rmsnorm(mem_flat, g_mem[i], BF16)], w_mem_kv[i], [0], 0, 2 * MEM_WIDTH)
        mem_k_p.append(z[:, :MEM_WIDTH].reshape(bsz, n_mem, MEM_HEADS, HEAD_DIM).transpose(0, 2, 1, 3))
        mem_v_p.append(z[:, MEM_WIDTH:].reshape(bsz, n_mem, MEM_HEADS, HEAD_DIM).transpose(0, 2, 1, 3))

    def mem_attend(q_mem, i):
        q_s = jnp.pad(q_mem[tp:].reshape(n_dec, t_dec, MEM_WIDTH), ((0, 0), (0, SAMPLE_Q_PAD - t_dec), (0, 0)))
        o_p = _mem_attn(q_mem, mem_k_p[i], mem_v_p[i], seq)
        o_s = _mem_attn(q_s.reshape(n_dec * SAMPLE_Q_PAD, MEM_WIDTH), cache_mem_k[i].transpose(0, 2, 1, 3),
                        cache_mem_v[i].transpose(0, 2, 1, 3), SAMPLE_Q_PAD)
        o_s = o_s.reshape(n_dec, SAMPLE_Q_PAD, MEM_WIDTH)[:, :t_dec].reshape(ts, MEM_WIDTH)
        return jnp.concatenate([o_p, o_s], axis=0)

    new_conv_p, new_conv_s = [], []
    kv = None
    for i in range(depth):
        hn = _rmsnorm(h, g_mix[i], BF16)
        if i < n_a:
            u = _matmul_glu(hn, w_in_a[i], conv_ch)
            q_mem = _matmul([hn], w_in_a[i], [0], 2 * conv_ch, MEM_WIDTH)
            u_s = u[tp:].reshape(n_dec, t_dec, conv_ch)
            ext_s = jnp.concatenate([state_conv[i], u_s], axis=1)
            tail = jnp.concatenate([jnp.zeros((bsz, width - 1, conv_ch), F32),
                                    u[:tp].reshape(bsz, seq, conv_ch)[:, max(seq - (width - 1), 0):]], axis=1)
            new_conv_p.append(tail[:, -(width - 1):])
            new_conv_s.append(ext_s[:, t_dec:])
            mix_p = _conv_prompt(u, conv_w[i], conv_b[i], conv_ln_g[i], conv_ln_b[i], bsz, seq)
            mix_s = _conv_sample(ext_s.transpose(1, 0, 2), conv_w[i], conv_b[i], conv_ln_g[i], conv_ln_b[i])
            mix = jnp.concatenate([mix_p, mix_s.transpose(1, 0, 2).reshape(ts, conv_ch)], axis=0)
            w_out = w_out_a[i]
        else:
            j = i - n_a
            if kv is None:
                hk = _rmsnorm(h, g_kv, BF16)
                k_p = _matmul([hk], w_kvf, [0], 0, fox_w, n_rows=tp, heads_of=(bsz, seq))
                v_p = _matmul([hk], w_kvf, [0], fox_w, fox_w, n_rows=tp, heads_of=(bsz, seq))
                kv_s = _matmul([hk], w_kvf, [0], 0, 2 * fox_w, row_start=tp, n_rows=ts)
                w_f = jnp.pad(w_kvf[:, 2 * fox_w:], ((0, 0), (0, LANES - n_heads)))
                b_f_pad = jnp.pad(b_f, (0, LANES - n_heads)).reshape(1, LANES)
                logf_pad = _matmul([hk], w_f, [0], 0, LANES, epilogue="logsig", extra=b_f_pad)
                crep, crow = _fox_cumsum(logf_pad[:tp].reshape(bsz, seq, LANES), n_heads)
                def new_rows(x):
                    x = x.reshape(n_dec, t_dec, n_heads, HEAD_DIM).transpose(0, 2, 1, 3)
                    return jnp.pad(x, ((0, 0), (0, 0), (0, page - t_dec), (0, 0)))

                k_new = new_rows(kv_s[:, :fox_w])
                v_new = new_rows(kv_s[:, fox_w:])
                lf_new = jnp.pad(logf_pad[tp:, :n_heads].reshape(n_dec, t_dec, n_heads).transpose(0, 2, 1),
                                 ((0, 0), (0, HEAD_PAD - n_heads), (0, page - t_dec)))
                lf_new_col = jnp.pad(logf_pad[tp:].reshape(n_dec, t_dec, LANES),
                                     ((0, 0), (0, Q_ROWS - t_dec), (0, 0)))
                lf_pool = jnp.pad(cache_fox_logf.astype(F32).transpose(0, 2, 1),
                                  ((0, 0), (0, HEAD_PAD - n_heads), (0, 0)))
                kv = (k_p, v_p, kv_s, logf_pad)
            q = _matmul([hn], w_in_b[j], [0], 0, fox_w)
            q_mem = _matmul([hn], w_in_b[j], [0], fox_w, MEM_WIDTH)
            mix_p = _fox_prompt(q, k_p, v_p, crep, crow)
            q_rows = jnp.pad(q[tp:].reshape(n_dec, t_dec, n_heads, HEAD_DIM).transpose(0, 2, 1, 3),
                             ((0, 0), (0, 0), (0, Q_ROWS - t_dec), (0, 0))).astype(BF16)
            o_rows = _fox_sample(q_rows, k_new, v_new, lf_new, lf_new_col, cache_fox_k.transpose(0, 2, 1, 3),
                                 cache_fox_v.transpose(0, 2, 1, 3), lf_pool, page_table, t_dec)
            mix_s = o_rows.reshape(n_dec, n_heads, Q_ROWS, HEAD_DIM)[:, :, :t_dec].transpose(0, 2, 1, 3)
            mix = jnp.concatenate([mix_p, mix_s.reshape(ts, fox_w).astype(BF16)], axis=0)
            w_out = w_out_b[j]
        mo = mem_attend(q_mem, i)
        h = _matmul([mix, mo], w_out, [0, mix.shape[1]], 0, d, epilogue="resid", extra=h)
        hf = _rmsnorm(h, g_ffn[i], BF16)
        jj = i // 2
        if i % 2 == 0:
            h = _ffn(hf, h, w_ff_gate[jj], w_ff_up[jj], w_ff_down[jj])
        else:
            h = _moe(hf, h, w_router[jj], b_router[jj], w_e_gate[jj], w_e_up[jj], w_e_down[jj])

    y_p = _rmsnorm(h, g_final, F32, 0, tp)
    y_s = _rmsnorm(h, g_final, F32, tp, ts)
    k_p, v_p, kv_s, logf_pad = kv
    logf = logf_pad[:, :n_heads]
    return (y_p.reshape(bsz, seq, d),
            y_s.reshape(n_dec, t_dec, d),
            k_p.transpose(0, 2, 1, 3),
            v_p.transpose(0, 2, 1, 3),
            logf[:tp].reshape(bsz, seq, n_heads),
            jnp.stack(new_conv_p),
            jnp.stack(mem_k_p).transpose(0, 1, 3, 2, 4),
            jnp.stack(mem_v_p).transpose(0, 1, 3, 2, 4),
            kv_s[:, :fox_w].reshape(n_dec, t_dec, n_heads, HEAD_DIM),
            kv_s[:, fox_w:].reshape(n_dec, t_dec, n_heads, HEAD_DIM),
            logf[tp:].reshape(n_dec, t_dec, n_heads),
            jnp.stack(new_conv_s))
```

```python
import functools
import math

import jax
import jax.numpy as jnp
from jax import lax
from jax.experimental import pallas as pl
from jax.experimental.pallas import tpu as pltpu

HEAD_DIM = 128
MEM_HEADS = 4
MEM_WIDTH = MEM_HEADS * HEAD_DIM
TOP_K = 2
EPS = 1e-6
NEG_INF = -1e30
ATTN_SCALE = HEAD_DIM ** -0.5
LOG2_E = math.log2(math.e)
LANES = 128
SUBLANES = 8
HEAD_PAD = 16
Q_ROWS = 8
SAMPLE_Q_PAD = 16
CONV_HALO = 32
VMEM_LIMIT_BYTES = 56 * 1024 * 1024
MOE_SUB = 256
MOE_ROW_TILE = 3 * MOE_SUB
MOE_FFN_SUB = 128
MOE_UNSELECTED = -1.0e9

BF16 = jnp.bfloat16
F32 = jnp.float32


def _cparams(*semantics):
    return pltpu.CompilerParams(dimension_semantics=semantics, vmem_limit_bytes=VMEM_LIMIT_BYTES)


def _pick(n, candidates):
    for c in candidates:
        if c <= n and n % c == 0:
            return c
    return n


def _row_tile(t):
    return _pick(t, (1040, 1024, 640, 512, 256, 128, 64, 32, 16))


def _dot(a, b):
    return jnp.dot(a, b, preferred_element_type=F32)


def _dot_nt(a, b):
    return lax.dot_general(a, b, (((1,), (1,)), ((), ())), preferred_element_type=F32)


def _dot_exact(a, b):
    return jnp.dot(a, b, preferred_element_type=F32, precision=lax.Precision.HIGHEST)


def _log_sigmoid(x):
    return jnp.minimum(x, 0.0) - jnp.log1p(jnp.exp(-jnp.abs(x)))


def _rmsnorm_kernel(x_ref, g_ref, o_ref):
    x = x_ref[...]
    y = x * lax.rsqrt(jnp.mean(x * x, axis=-1, keepdims=True) + EPS)
    o_ref[...] = (y * g_ref[...]).astype(o_ref.dtype)


def _rmsnorm(x, g, out_dtype, row_start=0, n_rows=None):
    d = x.shape[1]
    n_rows = x.shape[0] if n_rows is None else n_rows
    tm = _pick(math.gcd(n_rows, row_start), (1024, 640, 512, 256, 128, 64, 32, 16))
    first = row_start // tm
    return pl.pallas_call(
        _rmsnorm_kernel,
        grid=(n_rows // tm,),
        in_specs=[pl.BlockSpec((tm, d), lambda i: (first + i, 0)),
                  pl.BlockSpec((1, d), lambda i: (0, 0))],
        out_specs=pl.BlockSpec((tm, d), lambda i: (i, 0)),
        out_shape=jax.ShapeDtypeStruct((n_rows, d), out_dtype),
        compiler_params=_cparams("parallel"),
        name="rmsnorm",
    )(x, g.reshape(1, d))


def _mm_kernel(*refs, n_a, epilogue):
    a_refs = refs[:n_a]
    w_refs = refs[n_a:2 * n_a]
    pos = 2 * n_a
    extra_ref = None
    if epilogue in ("resid", "logsig"):
        extra_ref = refs[pos]
        pos += 1
    o_ref = refs[pos]
    wb_refs = refs[pos + 1:pos + 1 + n_a]

    @pl.when(pl.program_id(1) == 0)
    def _():
        for w_ref, wb_ref in zip(w_refs, wb_refs):
            wb_ref[...] = w_ref[...].astype(BF16)

    acc = None
    for a_ref, wb_ref in zip(a_refs, wb_refs):
        d = _dot(a_ref[...], wb_ref[...])
        acc = d if acc is None else acc + d
    if epilogue == "resid":
        acc = extra_ref[...] + acc
    elif epilogue == "logsig":
        acc = _log_sigmoid(acc + extra_ref[...])
    if len(o_ref.shape) == 4:
        for j in range(o_ref.shape[1]):
            o_ref[0, j] = acc[:, j * HEAD_DIM:(j + 1) * HEAD_DIM].astype(o_ref.dtype)
    else:
        o_ref[...] = acc.astype(o_ref.dtype)


def _matmul(a_list, w, row_offs, col_off, n_out, *, epilogue="none", extra=None, out_dtype=F32,
            row_start=0, n_rows=None, heads_of=None):
    n_rows = a_list[0].shape[0] if n_rows is None else n_rows
    align = math.gcd(n_rows, row_start) if heads_of is None else math.gcd(heads_of[1], row_start)
    tm = _row_tile(align)
    tn = _pick(n_out, (1024, 768, 512, 256, 128))
    first = row_start // tm
    n_a = len(a_list)
    in_specs, scratch = [], []
    for a in a_list:
        in_specs.append(pl.BlockSpec((tm, a.shape[1]), lambda n, m: (first + m, 0)))
    for a, ro in zip(a_list, row_offs):
        k = a.shape[1]
        assert ro % k == 0 and col_off % tn == 0
        in_specs.append(pl.BlockSpec((k, tn), functools.partial(
            lambda n, m, rb, cb: (rb, cb + n), rb=ro // k, cb=col_off // tn)))
        scratch.append(pltpu.VMEM((k, tn), BF16))
    args = list(a_list) + [w] * n_a
    if epilogue == "resid":
        in_specs.append(pl.BlockSpec((tm, tn), lambda n, m: (first + m, n)))
        args.append(extra)
    elif epilogue == "logsig":
        in_specs.append(pl.BlockSpec((1, tn), lambda n, m: (0, n)))
        args.append(extra)
    if heads_of is None:
        out_spec = pl.BlockSpec((tm, tn), lambda n, m: (m, n))
        out_shape = jax.ShapeDtypeStruct((n_rows, n_out), out_dtype)
    else:
        bsz, seq = heads_of
        per_seq = seq // tm
        out_spec = pl.BlockSpec((1, tn // HEAD_DIM, tm, HEAD_DIM),
                                lambda n, m: (m // per_seq, n, m % per_seq, 0))
        out_shape = jax.ShapeDtypeStruct((bsz, n_out // HEAD_DIM, seq, HEAD_DIM), out_dtype)
    return pl.pallas_call(
        functools.partial(_mm_kernel, n_a=n_a, epilogue=epilogue),
        grid=(n_out // tn, n_rows // tm),
        in_specs=in_specs,
        out_specs=out_spec,
        out_shape=out_shape,
        scratch_shapes=scratch,
        compiler_params=_cparams("arbitrary", "arbitrary"),
        name="matmul_" + epilogue,
    )(*args)


def _glu_kernel(a_ref, w1_ref, w2_ref, o_ref, wb1_ref, wb2_ref):
    @pl.when(pl.program_id(1) == 0)
    def _():
        wb1_ref[...] = w1_ref[...].astype(BF16)
        wb2_ref[...] = w2_ref[...].astype(BF16)

    a = a_ref[...]
    o_ref[...] = _dot(a, wb1_ref[...]) * jax.nn.sigmoid(_dot(a, wb2_ref[...]))


def _matmul_glu(a, w, n_out):
    t, k = a.shape
    tm = _row_tile(t)
    tn = _pick(n_out, (1024, 768, 512, 256, 128))
    nb = n_out // tn
    return pl.pallas_call(
        _glu_kernel,
        grid=(nb, t // tm),
        in_specs=[pl.BlockSpec((tm, k), lambda n, m: (m, 0)),
                  pl.BlockSpec((k, tn), lambda n, m: (0, n)),
                  pl.BlockSpec((k, tn), lambda n, m: (0, n + nb))],
        out_specs=pl.BlockSpec((tm, tn), lambda n, m: (m, n)),
        out_shape=jax.ShapeDtypeStruct((t, n_out), F32),
        scratch_shapes=[pltpu.VMEM((k, tn), BF16), pltpu.VMEM((k, tn), BF16)],
        compiler_params=_cparams("arbitrary", "arbitrary"),
        name="matmul_glu",
    )(a, w, w)


def _swiglu_partial(x, wg, wu, wd):
    g = _dot(x, wg)
    u = _dot(x, wu)
    return _dot((g * jax.nn.sigmoid(g) * u).astype(BF16), wd)


def _ffn_kernel(h_ref, g_ref, wg_ref, wu_ref, wd_ref, o_ref, x_ref):
    @pl.when(pl.program_id(1) == 0)
    def _():
        h = h_ref[...]
        o_ref[...] = h
        y = h * lax.rsqrt(jnp.mean(h * h, axis=-1, keepdims=True) + EPS)
        x_ref[...] = (y * g_ref[...]).astype(BF16)

    o_ref[...] += _swiglu_partial(x_ref[...], wg_ref[...].astype(BF16), wu_ref[...].astype(BF16),
                                  wd_ref[...].astype(BF16))


def _ffn(h, g, w_gate, w_up, w_down):
    t, d = h.shape
    f = w_gate.shape[1]
    tm = _row_tile(t)
    tf = _pick(f, (256, 128))
    return pl.pallas_call(
        _ffn_kernel,
        grid=(t // tm, f // tf),
        in_specs=[pl.BlockSpec((tm, d), lambda m, j: (m, 0), pipeline_mode=pl.Buffered(1)),
                  pl.BlockSpec((1, d), lambda m, j: (0, 0)),
                  pl.BlockSpec((d, tf), lambda m, j: (0, j)),
                  pl.BlockSpec((d, tf), lambda m, j: (0, j)),
                  pl.BlockSpec((tf, d), lambda m, j: (j, 0))],
        out_specs=pl.BlockSpec((tm, d), lambda m, j: (m, 0)),
        out_shape=jax.ShapeDtypeStruct((t, d), F32),
        scratch_shapes=[pltpu.VMEM((tm, d), BF16)],
        compiler_params=_cparams("parallel", "arbitrary"),
        name="ffn",
    )(h, g.reshape(1, d), w_gate, w_up, w_down)


def _router_kernel(x_ref, w_ref, b_ref, o_ref, *, n_experts):
    logits = _dot(x_ref[...], w_ref[...].astype(BF16)) + b_ref[...]
    lane = lax.broadcasted_iota(jnp.int32, logits.shape, 1)
    logits = jnp.where(lane < n_experts, logits, NEG_INF)
    v1 = jnp.max(logits, axis=-1, keepdims=True)
    i1 = jnp.min(jnp.where(logits == v1, lane, LANES), axis=-1, keepdims=True)
    rest = jnp.where(lane == i1, NEG_INF, logits)
    v2 = jnp.max(rest, axis=-1, keepdims=True)
    i2 = jnp.min(jnp.where(rest == v2, lane, LANES), axis=-1, keepdims=True)
    e2 = jnp.exp(v2 - v1)
    p1 = 1.0 / (1.0 + e2)
    p2 = e2 / (1.0 + e2)
    o_ref[...] = jnp.where(lane == i1, p1, jnp.where(lane == i2, p2, 0.0))


def _router(x, w_router, b_router):
    t, d = x.shape
    n_e = w_router.shape[-1]
    tm = _row_tile(t)
    w_pad = jnp.pad(w_router, ((0, 0), (0, LANES - n_e)))
    b_pad = jnp.pad(b_router, (0, LANES - n_e)).reshape(1, LANES)
    return pl.pallas_call(
        functools.partial(_router_kernel, n_experts=n_e),
        grid=(t // tm,),
        in_specs=[pl.BlockSpec((tm, d), lambda m: (m, 0)),
                  pl.BlockSpec((d, LANES), lambda m: (0, 0)),
                  pl.BlockSpec((1, LANES), lambda m: (0, 0))],
        out_specs=pl.BlockSpec((tm, LANES), lambda m: (m, 0)),
        out_shape=jax.ShapeDtypeStruct((t, LANES), F32),
        compiler_params=_cparams("parallel"),
        name="router",
    )(x, w_pad, b_pad)


def _moe_rank_kernel(g_ref, pos_ref, pos_t_ref, start_ref, cnt_ref, carry_ref):
    i = pl.program_id(0)
    tm = g_ref.shape[0]

    @pl.when(i == 0)
    def _():
        carry_ref[...] = jnp.zeros(carry_ref.shape, F32)

    sel = g_ref[...] > 0.0
    r = lax.broadcasted_iota(jnp.int32, (tm, tm), 0)
    c = lax.broadcasted_iota(jnp.int32, (tm, tm), 1)
    before = jnp.where(c < r, 1.0, 0.0).astype(BF16)
    ones = jnp.where(sel, 1.0, 0.0)
    carry = carry_ref[...]
    rank = _dot(before, ones.astype(BF16)) + carry
    start_ref[...] = jnp.broadcast_to(carry, start_ref.shape)
    carry = rank[tm - 1:tm, :] + ones[tm - 1:tm, :]
    carry_ref[...] = carry
    cnt_ref[...] = jnp.broadcast_to(carry, cnt_ref.shape)
    pos = jnp.where(sel, rank, MOE_UNSELECTED)
    pos_ref[...] = pos
    pos_t_ref[...] = pos.T[0:HEAD_PAD, :]


def _moe_rank(gates, tc):
    t = gates.shape[0]
    nch = t // tc
    return pl.pallas_call(
        _moe_rank_kernel,
        grid=(nch,),
        in_specs=[pl.BlockSpec((tc, LANES), lambda i: (i, 0))],
        out_specs=[pl.BlockSpec((tc, LANES), lambda i: (i, 0)),
                   pl.BlockSpec((HEAD_PAD, tc), lambda i: (0, i)),
                   pl.BlockSpec((8, LANES), lambda i: (i, 0)),
                   pl.BlockSpec((8, LANES), lambda i: (0, 0))],
        out_shape=[jax.ShapeDtypeStruct((t, LANES), F32),
                   jax.ShapeDtypeStruct((HEAD_PAD, t), F32),
                   jax.ShapeDtypeStruct((nch * 8, LANES), F32),
                   jax.ShapeDtypeStruct((8, LANES), F32)],
        scratch_shapes=[pltpu.VMEM((1, LANES), F32)],
        compiler_params=_cparams("arbitrary"),
        name="moe_rank",
    )(gates)


def _moe_gather_kernel(sb_ref, ch_ref, first_ref, last_ref, valid_ref, esb_ref, base_ref,
                       pos_t_ref, x_ref, o_ref, acc_ref):
    i = pl.program_id(0)

    @pl.when(valid_ref[i] == 1)
    def _():
        sb = sb_ref[i]
        local = pos_t_ref[pl.ds(esb_ref[sb], 1), :].astype(jnp.int32) + base_ref[sb]
        rows = lax.broadcasted_iota(jnp.int32, (o_ref.shape[0], local.shape[1]), 0)
        onehot = jnp.where(local == rows, 1.0, 0.0).astype(BF16)
        part = _dot(onehot, x_ref[...])

        @pl.when(first_ref[i] == 1)
        def _():
            acc_ref[...] = part

        @pl.when(first_ref[i] == 0)
        def _():
            acc_ref[...] += part

        @pl.when(last_ref[i] == 1)
        def _():
            o_ref[...] = acc_ref[...].astype(o_ref.dtype)


def _moe_gather(plan, pos_t, x, n_rows, tc):
    d = x.shape[1]
    sb = MOE_SUB
    prefetch = (plan["g_sb"], plan["g_ch"], plan["g_first"], plan["g_last"], plan["g_valid"],
                plan["e_sb"], plan["base_sb"])
    grid_spec = pltpu.PrefetchScalarGridSpec(
        num_scalar_prefetch=len(prefetch),
        grid=(plan["g_sb"].shape[0],),
        in_specs=[pl.BlockSpec((HEAD_PAD, tc), lambda i, sbl, chl, *_: (0, chl[i])),
                  pl.BlockSpec((tc, d), lambda i, sbl, chl, *_: (chl[i], 0))],
        out_specs=pl.BlockSpec((sb, d), lambda i, sbl, chl, *_: (sbl[i], 0)),
        scratch_shapes=[pltpu.VMEM((sb, d), F32)],
    )
    return pl.pallas_call(
        _moe_gather_kernel,
        grid_spec=grid_spec,
        out_shape=jax.ShapeDtypeStruct((n_rows, d), BF16),
        compiler_params=_cparams("arbitrary"),
        name="moe_gather",
    )(*prefetch, pos_t, x)


def _moe_ffn_kernel(et_ref, nv_ref, x_ref, wg_ref, wu_ref, wd_ref, o_ref, acc_ref):
    del et_ref
    r = pl.program_id(0)
    f = pl.program_id(1)
    last_f = pl.num_programs(1) - 1
    n_valid = nv_ref[r]
    n_sub = o_ref.shape[0] // MOE_FFN_SUB

    @pl.when(f == 0)
    def _():
        acc_ref[...] = jnp.zeros(acc_ref.shape, F32)

    for n in range(1, n_sub + 1):
        @pl.when(n_valid == n)
        def _():
            rows = slice(0, n * MOE_FFN_SUB)
            acc_ref[rows, :] += _swiglu_partial(x_ref[rows, :], wg_ref[0].astype(BF16), wu_ref[0].astype(BF16),
                                                wd_ref[0].astype(BF16))

    @pl.when(f == last_f)
    def _():
        o_ref[...] = acc_ref[...].astype(o_ref.dtype)


def _moe_ffn(plan, xs, w_gate, w_up, w_down):
    n_rows, d = xs.shape
    f = w_gate.shape[2]
    tr = MOE_ROW_TILE
    tf = _pick(f, (512, 256, 128))
    nf = f // tf

    def w_in_map(r, j, et, nv):
        return (et[r], 0, jnp.where(nv[r] > 0, j, nf - 1))

    def w_out_map(r, j, et, nv):
        return (et[r], jnp.where(nv[r] > 0, j, nf - 1), 0)

    grid_spec = pltpu.PrefetchScalarGridSpec(
        num_scalar_prefetch=2,
        grid=(n_rows // tr, nf),
        in_specs=[pl.BlockSpec((tr, d), lambda r, j, et, nv: (r, 0)),
                  pl.BlockSpec((1, d, tf), w_in_map),
                  pl.BlockSpec((1, d, tf), w_in_map),
                  pl.BlockSpec((1, tf, d), w_out_map)],
        out_specs=pl.BlockSpec((tr, d), lambda r, j, et, nv: (r, 0)),
        scratch_shapes=[pltpu.VMEM((tr, d), F32)],
    )
    return pl.pallas_call(
        _moe_ffn_kernel,
        grid_spec=grid_spec,
        out_shape=jax.ShapeDtypeStruct((n_rows, d), BF16),
        compiler_params=_cparams("arbitrary", "arbitrary"),
        name="moe_ffn",
    )(plan["e_tile"], plan["nv_tile"], xs, w_gate, w_up, w_down)


def _moe_combine_kernel(ch_ref, sb_ref, first_ref, valid_ref, esb_ref, base_ref,
                        pos_ref, gate_ref, h_ref, y_ref, o_ref):
    i = pl.program_id(0)

    @pl.when(first_ref[i] == 1)
    def _():
        o_ref[...] = h_ref[...]

    @pl.when(valid_ref[i] == 1)
    def _():
        sb = sb_ref[i]
        lane = lax.broadcasted_iota(jnp.int32, pos_ref.shape, 1)
        mine = lane == esb_ref[sb]
        rank = jnp.sum(jnp.where(mine, pos_ref[...], 0.0), axis=-1, keepdims=True)
        local = rank.astype(jnp.int32) + base_ref[sb]
        gate = jnp.sum(jnp.where(mine, gate_ref[...], 0.0), axis=-1, keepdims=True)
        cols = lax.broadcasted_iota(jnp.int32, (pos_ref.shape[0], y_ref.shape[0]), 1)
        onehot = jnp.where(local == cols, 1.0, 0.0).astype(BF16)
        o_ref[...] += gate * _dot(onehot, y_ref[...])


def _moe_combine(plan, pos, gates, h, ys, tc):
    t, d = h.shape
    sb = MOE_SUB
    prefetch = (plan["c_ch"], plan["c_sb"], plan["c_first"], plan["c_valid"], plan["e_sb"], plan["base_sb"])
    grid_spec = pltpu.PrefetchScalarGridSpec(
        num_scalar_prefetch=len(prefetch),
        grid=(plan["c_ch"].shape[0],),
        in_specs=[pl.BlockSpec((tc, LANES), lambda i, chl, sbl, *_: (chl[i], 0)),
                  pl.BlockSpec((tc, LANES), lambda i, chl, sbl, *_: (chl[i], 0)),
                  pl.BlockSpec((tc, d), lambda i, chl, sbl, *_: (chl[i], 0)),
                  pl.BlockSpec((sb, d), lambda i, chl, sbl, *_: (sbl[i], 0))],
        out_specs=pl.BlockSpec((tc, d), lambda i, chl, sbl, *_: (chl[i], 0)),
    )
    return pl.pallas_call(
        _moe_combine_kernel,
        grid_spec=grid_spec,
        out_shape=jax.ShapeDtypeStruct((t, d), F32),
        compiler_params=_cparams("arbitrary"),
        name="moe_combine",
    )(*prefetch, pos, gates, h, ys)


def _pair_list(overlap, length):
    n_b = overlap.shape[1]
    flat = overlap.reshape(-1)
    n = jnp.sum(flat.astype(jnp.int32))
    idx = jnp.nonzero(flat, size=length, fill_value=0)[0].astype(jnp.int32)
    step = jnp.arange(length, dtype=jnp.int32)
    valid = step < n
    idx = jnp.where(valid, idx, idx[jnp.maximum(n - 1, 0)])
    a = idx // n_b
    b = idx % n_b
    nxt_valid = jnp.concatenate([valid[1:], jnp.zeros((1,), bool)])
    first = valid & (a != jnp.concatenate([jnp.full((1,), -1, jnp.int32), a[:-1]]))
    last = valid & ((a != jnp.concatenate([a[1:], jnp.full((1,), -1, jnp.int32)])) | ~nxt_valid)
    as_i32 = lambda v: v.astype(jnp.int32)
    return a, b, as_i32(first), as_i32(last), as_i32(valid)


def _moe_plan(counts, chunk_start, n_rows, nch):
    n_e = counts.shape[0]
    tr, sb = MOE_ROW_TILE, MOE_SUB
    seg = (counts + tr - 1) // tr * tr
    seg_end = jnp.cumsum(seg)
    off = seg_end - seg
    chunk_end = jnp.concatenate([chunk_start[:, 1:], counts[:, None]], axis=1)

    def owner(start):
        return jnp.minimum(jnp.sum(seg_end[None, :] <= start[:, None], axis=1), n_e - 1).astype(jnp.int32)

    sb_start = jnp.arange(n_rows // sb, dtype=jnp.int32) * sb
    e_sb = owner(sb_start)
    k0 = sb_start - off[e_sb]
    k1 = jnp.minimum(k0 + sb, counts[e_sb])
    sb_used = (sb_start < seg_end[-1]) & (k0 < counts[e_sb])
    overlap = sb_used[:, None] & (chunk_start[e_sb] < k1[:, None]) & (chunk_end[e_sb] > k0[:, None])
    length = n_rows // sb + n_e * nch
    g_sb, g_ch, g_first, g_last, g_valid = _pair_list(overlap, length)
    c_ch, c_sb, c_first, _, c_valid = _pair_list(overlap.T, length)

    tile_start = jnp.arange(n_rows // tr, dtype=jnp.int32) * tr
    e_tile = owner(tile_start)
    left = counts[e_tile] - (tile_start - off[e_tile])
    step = MOE_FFN_SUB
    nv_tile = jnp.where(tile_start < seg_end[-1], jnp.clip((left + step - 1) // step, 0, tr // step), 0)
    return dict(g_sb=g_sb, g_ch=g_ch, g_first=g_first, g_last=g_last, g_valid=g_valid,
                c_ch=c_ch, c_sb=c_sb, c_first=c_first, c_valid=c_valid,
                e_sb=e_sb, base_sb=(off[e_sb] - sb_start).astype(jnp.int32),
                e_tile=e_tile, nv_tile=nv_tile.astype(jnp.int32))


def _moe(x, h, w_router, b_router, w_gate, w_up, w_down):
    t = x.shape[0]
    n_e = w_router.shape[-1]
    tc = _pick(t, (640, 512, 384, 256, 128))
    assert tc % LANES == 0 and t % tc == 0 and n_e <= 8
    nch = t // tc
    n_rows = -(-(TOP_K * t + n_e * (MOE_ROW_TILE - 1)) // MOE_ROW_TILE) * MOE_ROW_TILE
    gates = _router(x, w_router, b_router)
    pos, pos_t, starts, totals = _moe_rank(gates, tc)
    counts = totals[0, :n_e].astype(jnp.int32)
    chunk_start = starts.reshape(nch, 8, LANES)[:, 0, :n_e].T.astype(jnp.int32)
    plan = _moe_plan(counts, chunk_start, n_rows, nch)
    xs = _moe_gather(plan, pos_t, x, n_rows, tc)
    ys = _moe_ffn(plan, xs, w_gate, w_up, w_down)
    return _moe_combine(plan, pos, gates, h, ys, tc)


def _ln_swish(hc, lg, lb):
    mu = jnp.mean(hc, axis=-1, keepdims=True)
    xc = hc - mu
    var = jnp.mean(xc * xc, axis=-1, keepdims=True)
    y = xc * lax.rsqrt(var + EPS) * lg + lb
    return y * jax.nn.sigmoid(y)


def _conv_kernel(prev_ref, main_ref, w_ref, b_ref, lg_ref, lb_ref, o_ref, ext_ref, acc_ref, sh_ref, *, tt, width):
    @pl.when(pl.program_id(1) == 0)
    def _():
        ext_ref[0:CONV_HALO, :] = jnp.zeros((CONV_HALO, ext_ref.shape[1]), F32)

    @pl.when(pl.program_id(1) > 0)
    def _():
        ext_ref[0:CONV_HALO, :] = prev_ref[...]

    ext_ref[CONV_HALO:CONV_HALO + tt, :] = main_ref[...]
    n_chunks = o_ref.shape[-1] // LANES
    lead = CONV_HALO - (width - 1)

    span = tt + CONV_HALO - SUBLANES

    def chunk(c, carry):
        c0 = pl.multiple_of(c * LANES, LANES)
        for r in range(1, SUBLANES):
            sh_ref[r - 1] = ext_ref[pl.ds(r, span), pl.ds(c0, LANES)]
        acc = jnp.zeros((tt, LANES), F32)
        for j in range(width):
            phase = (lead + j) % SUBLANES
            base = lead + j - phase
            if phase == 0:
                rows = ext_ref[pl.ds(base, tt), pl.ds(c0, LANES)]
            else:
                rows = sh_ref[phase - 1, pl.ds(base, tt), :]
            acc = acc + rows * w_ref[pl.ds(j, 1), pl.ds(c0, LANES)]
        acc_ref[:, pl.ds(c0, LANES)] = acc + b_ref[:, pl.ds(c0, LANES)]
        return carry

    lax.fori_loop(0, n_chunks, chunk, 0)
    o_ref[...] = _ln_swish(acc_ref[...], lg_ref[...], lb_ref[...]).astype(o_ref.dtype)


def _conv_prompt(u, w, b, lg, lb, bsz, seq):
    c = u.shape[1]
    width = w.shape[0]
    tt = _pick(seq, (256, 128, 64, 32))
    nt = seq // tt
    per_halo = tt // CONV_HALO
    return pl.pallas_call(
        functools.partial(_conv_kernel, tt=tt, width=width),
        grid=(bsz, nt),
        in_specs=[pl.BlockSpec((CONV_HALO, c), lambda bb, i: (jnp.maximum((bb * nt + i) * per_halo - 1, 0), 0)),
                  pl.BlockSpec((tt, c), lambda bb, i: (bb * nt + i, 0)),
                  pl.BlockSpec((width, c), lambda bb, i: (0, 0)),
                  pl.BlockSpec((1, c), lambda bb, i: (0, 0)),
                  pl.BlockSpec((1, c), lambda bb, i: (0, 0)),
                  pl.BlockSpec((1, c), lambda bb, i: (0, 0))],
        out_specs=pl.BlockSpec((tt, c), lambda bb, i: (bb * nt + i, 0)),
        out_shape=jax.ShapeDtypeStruct((bsz * seq, c), BF16),
        scratch_shapes=[pltpu.VMEM((tt + CONV_HALO, c), F32), pltpu.VMEM((tt, c), F32),
                        pltpu.VMEM((SUBLANES - 1, tt + CONV_HALO - SUBLANES, LANES), F32)],
        compiler_params=_cparams("parallel", "parallel"),
        name="conv_prompt",
    )(u, u, w, b.reshape(1, c), lg.reshape(1, c), lb.reshape(1, c))


def _conv_step_kernel(ext_ref, w_ref, b_ref, lg_ref, lb_ref, o_ref, *, width):
    n_t = o_ref.shape[0]
    for t in range(n_t):
        acc = ext_ref[t] * w_ref[0:1, :]
        for j in range(1, width):
            acc = acc + ext_ref[t + j] * w_ref[j:j + 1, :]
        o_ref[t] = _ln_swish(acc + b_ref[...], lg_ref[...], lb_ref[...]).astype(o_ref.dtype)


def _conv_sample(ext_t, w, b, lg, lb):
    l_ext, n, c = ext_t.shape
    width = w.shape[0]
    n_t = l_ext - (width - 1)
    return pl.pallas_call(
        functools.partial(_conv_step_kernel, width=width),
        out_shape=jax.ShapeDtypeStruct((n_t, n, c), BF16),
        compiler_params=pltpu.CompilerParams(vmem_limit_bytes=VMEM_LIMIT_BYTES),
        name="conv_sample",
    )(ext_t, w, b.reshape(1, c), lg.reshape(1, c), lb.reshape(1, c))


def _mem_attn_kernel(q_ref, k_ref, v_ref, o_ref):
    for hh in range(MEM_HEADS):
        sl = slice(hh * HEAD_DIM, (hh + 1) * HEAD_DIM)
        q = q_ref[:, sl].astype(BF16)
        k = k_ref[0, hh].astype(BF16)
        v = v_ref[0, hh].astype(BF16)
        s = _dot_nt(q, k) * ATTN_SCALE
        m = jnp.max(s, axis=-1, keepdims=True)
        p = jnp.exp(s - m)
        l = jnp.sum(p, axis=-1, keepdims=True)
        o_ref[:, sl] = (_dot(p.astype(BF16), v) / l).astype(o_ref.dtype)


def _mem_attn(q, k, v, rows_per_req):
    w = q.shape[1]
    n, _, m, _ = k.shape
    tq = _pick(rows_per_req, (512, 256, 128, 64, 32, 16))
    per_req = rows_per_req // tq
    return pl.pallas_call(
        _mem_attn_kernel,
        grid=(n, per_req),
        in_specs=[pl.BlockSpec((tq, w), lambda i, j: (i * per_req + j, 0)),
                  pl.BlockSpec((1, MEM_HEADS, m, HEAD_DIM), lambda i, j: (i, 0, 0, 0)),
                  pl.BlockSpec((1, MEM_HEADS, m, HEAD_DIM), lambda i, j: (i, 0, 0, 0))],
        out_specs=pl.BlockSpec((tq, w), lambda i, j: (i * per_req + j, 0)),
        out_shape=jax.ShapeDtypeStruct((n * rows_per_req, w), BF16),
        compiler_params=_cparams("parallel", "parallel"),
        name="mem_attn",
    )(q, k, v)


def _cumsum_kernel(lf_ref, crep_ref, crow_ref, *, blk, n_heads):
    seq = lf_ref.shape[1]
    r = lax.broadcasted_iota(jnp.int32, (blk, blk), 0)
    c = lax.broadcasted_iota(jnp.int32, (blk, blk), 1)
    tri = (c <= r).astype(F32)
    carry = jnp.zeros((1, LANES), F32)
    for i in range(seq // blk):
        rows = slice(i * blk, (i + 1) * blk)
        cs = _dot_exact(tri, lf_ref[0, rows, :]) + carry
        carry = cs[blk - 1:blk, :]
        cs2 = cs * LOG2_E
        crow_ref[0, :, rows] = cs2.T[0:HEAD_PAD, :]
        for hh in range(n_heads):
            crep_ref[0, hh, rows, :] = jnp.broadcast_to(cs2[:, hh:hh + 1], (blk, LANES))


def _fox_cumsum(logf_pad, n_heads):
    bsz, seq, _ = logf_pad.shape
    blk = _pick(seq, (256, 128))
    return pl.pallas_call(
        functools.partial(_cumsum_kernel, blk=blk, n_heads=n_heads),
        grid=(bsz,),
        in_specs=[pl.BlockSpec((1, seq, LANES), lambda b: (b, 0, 0))],
        out_specs=[pl.BlockSpec((1, n_heads, seq, LANES), lambda b: (b, 0, 0, 0)),
                   pl.BlockSpec((1, HEAD_PAD, seq), lambda b: (b, 0, 0))],
        out_shape=[jax.ShapeDtypeStruct((bsz, n_heads, seq, LANES), F32),
                   jax.ShapeDtypeStruct((bsz, HEAD_PAD, seq), F32)],
        compiler_params=_cparams("parallel"),
        name="fox_cumsum",
    )(logf_pad)


def _fox_prompt_kernel(q_ref, k_ref, v_ref, crep_ref, crow_ref, o_ref,
                       kb_ref, vt_ref, m_ref, l_ref, acc_ref, *, tq, tk):
    hh = pl.program_id(1)
    qi = pl.program_id(2)

    @pl.when(qi == 0)
    def _():
        kb_ref[...] = k_ref[0, 0].astype(BF16)
        vt_ref[...] = v_ref[0, 0].T.astype(BF16)

    q = q_ref[...].astype(BF16)
    q0 = pl.multiple_of(qi * tq, tq)
    cq = crow_ref[0, pl.ds(hh, 1), pl.ds(q0, tq)]
    m_ref[...] = jnp.full(m_ref.shape, NEG_INF, F32)
    l_ref[...] = jnp.zeros(l_ref.shape, F32)
    acc_ref[...] = jnp.zeros(acc_ref.shape, F32)

    def update(k0, on_diagonal):
        ck = jnp.tile(crep_ref[0, 0, pl.ds(k0, tk), :], (1, tq // LANES))
        z = _dot_nt(kb_ref[pl.ds(k0, tk), :], q) * (ATTN_SCALE * LOG2_E) - ck
        if on_diagonal:
            z = jnp.where(lax.broadcasted_iota(jnp.int32, (tk, tq), 0) <= lax.broadcasted_iota(jnp.int32, (tk, tq), 1),
                          z, NEG_INF)
        m_old = m_ref[...]
        m_new = jnp.maximum(m_old, jnp.max(z, axis=0, keepdims=True) + cq)
        alpha = jnp.exp2(m_old - m_new)
        p = jnp.exp2(z - (m_new - cq))
        l_ref[...] = alpha * l_ref[...] + jnp.sum(p, axis=0, keepdims=True)
        acc_ref[...] = alpha * acc_ref[...] + _dot(vt_ref[:, pl.ds(k0, tk)], p.astype(BF16))
        m_ref[...] = m_new

    def body(kj, carry):
        update(pl.multiple_of(kj * tk, tk), False)
        return carry

    lax.fori_loop(0, qi, body, 0)
    update(q0, True)
    o_ref[...] = (acc_ref[...] / l_ref[...]).T.astype(o_ref.dtype)


def _fox_prompt(q, k, v, crep, crow):
    bsz, n_heads, seq, _ = k.shape
    tq = _pick(seq, (512, 256, 128))
    tk = tq
    nq = seq // tq
    head_seq = pl.BlockSpec((1, 1, seq, HEAD_DIM), lambda b, h, i: (b, h, 0, 0))
    return pl.pallas_call(
        functools.partial(_fox_prompt_kernel, tq=tq, tk=tk),
        grid=(bsz, n_heads, nq),
        in_specs=[pl.BlockSpec((tq, HEAD_DIM), lambda b, h, i: (b * nq + i, h)),
                  head_seq,
                  head_seq,
                  pl.BlockSpec((1, 1, seq, LANES), lambda b, h, i: (b, h, 0, 0)),
                  pl.BlockSpec((1, HEAD_PAD, seq), lambda b, h, i: (b, 0, 0))],
        out_specs=pl.BlockSpec((tq, HEAD_DIM), lambda b, h, i: (b * nq + i, h)),
        out_shape=jax.ShapeDtypeStruct((bsz * seq, n_heads * HEAD_DIM), BF16),
        scratch_shapes=[pltpu.VMEM((seq, HEAD_DIM), BF16), pltpu.VMEM((HEAD_DIM, seq), BF16),
                        pltpu.VMEM((1, tq), F32), pltpu.VMEM((1, tq), F32),
                        pltpu.VMEM((HEAD_DIM, tq), F32)],
        compiler_params=_cparams("parallel", "parallel", "arbitrary"),
        name="fox_prompt",
    )(q, k, v, crep, crow)


def _fox_sample_kernel(*refs, pages_per_step, n_new, n_heads):
    pp = pages_per_step
    q_ref, knew_ref, vnew_ref, lfnew_ref, lfcol_ref = refs[1:6]
    k_refs = refs[6:6 + pp]
    v_refs = refs[6 + pp:6 + 2 * pp]
    lf_refs = refs[6 + 2 * pp:6 + 3 * pp]
    o_ref = refs[6 + 3 * pp]
    m_ref, l_ref, acc_ref, carry_ref, cn_ref = refs[7 + 3 * pp:]
    step = pl.program_id(1)
    page = k_refs[0].shape[2]
    n_rows = n_heads * Q_ROWS

    r = lax.broadcasted_iota(jnp.int32, (page, page), 0)
    c = lax.broadcasted_iota(jnp.int32, (page, page), 1)

    @pl.when(step == 0)
    def _():
        m_ref[...] = jnp.full(m_ref.shape, NEG_INF, F32)
        l_ref[...] = jnp.zeros(l_ref.shape, F32)
        acc_ref[...] = jnp.zeros(acc_ref.shape, F32)
        carry_ref[...] = jnp.zeros(carry_ref.shape, F32)
        lf_col = lfcol_ref[0]
        tok = lax.broadcasted_iota(jnp.int32, lf_col.shape, 0)
        c_col = jnp.zeros(lf_col.shape, F32)
        for j in range(n_new):
            c_col = c_col + jnp.where(tok >= j, lf_col[j:j + 1, :], 0.0)
        for hh in range(n_heads):
            cn_ref[hh * Q_ROWS:(hh + 1) * Q_ROWS, :] = c_col[:, hh:hh + 1]

    def per_head_rows(x16):
        return jnp.concatenate([jnp.broadcast_to(x16[hh:hh + 1, :], (Q_ROWS, page)) for hh in range(n_heads)],
                               axis=0)

    def attend(k_list, v_list, bias, mask=None):
        s = jnp.concatenate(
            [jnp.concatenate([_dot_nt(q_ref[0, hh], k_ref[0, hh].astype(BF16)) for hh in range(n_heads)], axis=0)
             for k_ref in k_list], axis=1)
        s = s * ATTN_SCALE + bias
        if mask is not None:
            s = jnp.where(mask, s, NEG_INF)
        m_old = m_ref[...]
        m_new = jnp.maximum(m_old, jnp.max(s, axis=-1, keepdims=True))
        alpha = jnp.exp(m_old - m_new)
        p = jnp.exp(s - m_new)
        l_ref[...] = alpha * l_ref[...] + jnp.sum(p, axis=-1, keepdims=True)
        pv = None
        for j, v_ref in enumerate(v_list):
            part = jnp.concatenate(
                [_dot(p[hh * Q_ROWS:(hh + 1) * Q_ROWS, j * page:(j + 1) * page].astype(BF16),
                      v_ref[0, hh].astype(BF16)) for hh in range(n_heads)], axis=0)
            pv = part if pv is None else pv + part
        acc_ref[...] = alpha * acc_ref[...] + pv
        m_ref[...] = m_new

    suffix = (r > c).astype(F32)
    carry = carry_ref[...]
    decay = []
    for j in range(pp):
        lf = lf_refs[j][0]
        decay.append(per_head_rows(_dot_exact(lf, suffix) + carry))
        carry = carry + jnp.sum(lf, axis=-1, keepdims=True)
    carry_ref[...] = carry
    attend(k_refs, v_refs, cn_ref[...] + jnp.concatenate(decay, axis=1))

    @pl.when(step == pl.num_programs(1) - 1)
    def _():
        c_row = _dot_exact(lfnew_ref[0], (r <= c).astype(F32))
        key = lax.broadcasted_iota(jnp.int32, (n_rows, page), 1)
        tok = lax.broadcasted_iota(jnp.int32, (n_rows, page), 0) % Q_ROWS
        attend([knew_ref], [vnew_ref], cn_ref[...] - per_head_rows(c_row), mask=key <= tok)
        o_ref[0] = acc_ref[...] / l_ref[...]


def _fox_sample(q_rows, k_new, v_new, lf_new, lf_new_col, k_pool, v_pool, lf_pool, page_table, n_new):
    n, n_heads = q_rows.shape[:2]
    page = k_pool.shape[2]
    n_pages = page_table.shape[1]
    pp = _pick(n_pages, (8, 4, 2, 1))
    n_steps = n_pages // pp
    n_rows = n_heads * Q_ROWS

    def kv_map(j):
        return lambda i, s, pt: (pt[i, n_pages - 1 - (s * pp + j)], 0, 0, 0)

    def lf_map(j):
        return lambda i, s, pt: (pt[i, n_pages - 1 - (s * pp + j)], 0, 0)

    req3 = lambda i, s, pt: (i, 0, 0)
    req4 = lambda i, s, pt: (i, 0, 0, 0)
    kv_block = (1, n_heads, page, HEAD_DIM)
    in_specs = [pl.BlockSpec((1, n_heads, Q_ROWS, HEAD_DIM), req4),
                pl.BlockSpec(kv_block, req4),
                pl.BlockSpec(kv_block, req4),
                pl.BlockSpec((1, HEAD_PAD, page), req3),
                pl.BlockSpec((1, Q_ROWS, LANES), req3)]
    in_specs += [pl.BlockSpec(kv_block, kv_map(j)) for j in range(pp)]
    in_specs += [pl.BlockSpec(kv_block, kv_map(j)) for j in range(pp)]
    in_specs += [pl.BlockSpec((1, HEAD_PAD, page), lf_map(j)) for j in range(pp)]
    grid_spec = pltpu.PrefetchScalarGridSpec(
        num_scalar_prefetch=1,
        grid=(n, n_steps),
        in_specs=in_specs,
        out_specs=pl.BlockSpec((1, n_rows, HEAD_DIM), req3),
        scratch_shapes=[pltpu.VMEM((n_rows, 1), F32), pltpu.VMEM((n_rows, 1), F32),
                        pltpu.VMEM((n_rows, HEAD_DIM), F32), pltpu.VMEM((HEAD_PAD, 1), F32),
                        pltpu.VMEM((n_rows, 1), F32)],
    )
    return pl.pallas_call(
        functools.partial(_fox_sample_kernel, pages_per_step=pp, n_new=n_new, n_heads=n_heads),
        grid_spec=grid_spec,
        out_shape=jax.ShapeDtypeStruct((n, n_rows, HEAD_DIM), F32),
        compiler_params=_cparams("parallel", "arbitrary"),
        name="fox_sample",
    )(page_table, q_rows, k_new, v_new, lf_new, lf_new_col,
      *([k_pool] * pp), *([v_pool] * pp), *([lf_pool] * pp))


def kernel(x_prompt, x_sample, cache_fox_k, cache_fox_v, cache_fox_logf, state_conv, cache_mem_k, cache_mem_v,
           page_table, mem_prompt, g_mix, g_ffn, g_mem, w_mem_kv, w_in_a, w_out_a, conv_w, conv_b, conv_ln_g,
           conv_ln_b, g_kv, w_kvf, b_f, w_in_b, w_out_b, w_ff_gate, w_ff_up, w_ff_down, w_router, b_router,
           w_e_gate, w_e_up, w_e_down, g_final):
    bsz, seq, d = x_prompt.shape
    n_dec, t_dec, _ = x_sample.shape
    depth = g_mix.shape[0]
    n_a = w_in_a.shape[0]
    conv_ch = conv_w.shape[-1]
    width = conv_w.shape[1]
    n_mem = mem_prompt.shape[1]
    fox_w = w_in_b.shape[-1] - MEM_WIDTH
    n_heads = fox_w // HEAD_DIM
    n_pool, page = cache_fox_k.shape[:2]
    tp = bsz * seq
    ts = n_dec * t_dec
    assert n_heads <= HEAD_PAD and width - 1 <= CONV_HALO and t_dec <= 8

    h = jnp.concatenate([x_prompt.reshape(tp, d), x_sample.reshape(ts, d)], axis=0)

    mem_flat = mem_prompt.reshape(bsz * n_mem, d)
    mem_k_p, mem_v_p = [], []
    for i in range(depth):
        z = _matmul([_rmsnorm(mem_flat, g_mem[i], BF16)], w_mem_kv[i], [0], 0, 2 * MEM_WIDTH)
        mem_k_p.append(z[:, :MEM_WIDTH].reshape(bsz, n_mem, MEM_HEADS, HEAD_DIM).transpose(0, 2, 1, 3))
        mem_v_p.append(z[:, MEM_WIDTH:].reshape(bsz, n_mem, MEM_HEADS, HEAD_DIM).transpose(0, 2, 1, 3))

    def mem_attend(q_mem, i):
        q_s = jnp.pad(q_mem[tp:].reshape(n_dec, t_dec, MEM_WIDTH), ((0, 0), (0, SAMPLE_Q_PAD - t_dec), (0, 0)))
        o_p = _mem_attn(q_mem, mem_k_p[i], mem_v_p[i], seq)
        o_s = _mem_attn(q_s.reshape(n_dec * SAMPLE_Q_PAD, MEM_WIDTH), cache_mem_k[i].transpose(0, 2, 1, 3),
                        cache_mem_v[i].transpose(0, 2, 1, 3), SAMPLE_Q_PAD)
        o_s = o_s.reshape(n_dec, SAMPLE_Q_PAD, MEM_WIDTH)[:, :t_dec].reshape(ts, MEM_WIDTH)
        return jnp.concatenate([o_p, o_s], axis=0)

    new_conv_p, new_conv_s = [], []
    kv = None
    for i in range(depth):
        hn = _rmsnorm(h, g_mix[i], BF16)
        if i < n_a:
            u = _matmul_glu(hn, w_in_a[i], conv_ch)
            q_mem = _matmul([hn], w_in_a[i], [0], 2 * conv_ch, MEM_WIDTH)
            u_s = u[tp:].reshape(n_dec, t_dec, conv_ch)
            ext_s = jnp.concatenate([state_conv[i], u_s], axis=1)
            tail = jnp.concatenate([jnp.zeros((bsz, width - 1, conv_ch), F32),
                                    u[:tp].reshape(bsz, seq, conv_ch)[:, max(seq - (width - 1), 0):]], axis=1)
            new_conv_p.append(tail[:, -(width - 1):])
            new_conv_s.append(ext_s[:, t_dec:])
            mix_p = _conv_prompt(u, conv_w[i], conv_b[i], conv_ln_g[i], conv_ln_b[i], bsz, seq)
            mix_s = _conv_sample(ext_s.transpose(1, 0, 2), conv_w[i], conv_b[i], conv_ln_g[i], conv_ln_b[i])
            mix = jnp.concatenate([mix_p, mix_s.transpose(1, 0, 2).reshape(ts, conv_ch)], axis=0)
            w_out = w_out_a[i]
        else:
            j = i - n_a
            if kv is None:
                hk = _rmsnorm(h, g_kv, BF16)
                k_p = _matmul([hk], w_kvf, [0], 0, fox_w, n_rows=tp, heads_of=(bsz, seq))
                v_p = _matmul([hk], w_kvf, [0], fox_w, fox_w, n_rows=tp, heads_of=(bsz, seq))
                kv_s = _matmul([hk], w_kvf, [0], 0, 2 * fox_w, row_start=tp, n_rows=ts)
                w_f = jnp.pad(w_kvf[:, 2 * fox_w:], ((0, 0), (0, LANES - n_heads)))
                b_f_pad = jnp.pad(b_f, (0, LANES - n_heads)).reshape(1, LANES)
                logf_pad = _matmul([hk], w_f, [0], 0, LANES, epilogue="logsig", extra=b_f_pad)
                crep, crow = _fox_cumsum(logf_pad[:tp].reshape(bsz, seq, LANES), n_heads)
                def new_rows(x):
                    x = x.reshape(n_dec, t_dec, n_heads, HEAD_DIM).transpose(0, 2, 1, 3)
                    return jnp.pad(x, ((0, 0), (0, 0), (0, page - t_dec), (0, 0)))

                k_new = new_rows(kv_s[:, :fox_w])
                v_new = new_rows(kv_s[:, fox_w:])
                lf_new = jnp.pad(logf_pad[tp:, :n_heads].reshape(n_dec, t_dec, n_heads).transpose(0, 2, 1),
                                 ((0, 0), (0, HEAD_PAD - n_heads), (0, page - t_dec)))
                lf_new_col = jnp.pad(logf_pad[tp:].reshape(n_dec, t_dec, LANES),
                                     ((0, 0), (0, Q_ROWS - t_dec), (0, 0)))
                lf_pool = jnp.pad(cache_fox_logf.astype(F32).transpose(0, 2, 1),
                                  ((0, 0), (0, HEAD_PAD - n_heads), (0, 0)))
                kv = (k_p, v_p, kv_s, logf_pad)
            q = _matmul([hn], w_in_b[j], [0], 0, fox_w)
            q_mem = _matmul([hn], w_in_b[j], [0], fox_w, MEM_WIDTH)
            mix_p = _fox_prompt(q, k_p, v_p, crep, crow)
            q_rows = jnp.pad(q[tp:].reshape(n_dec, t_dec, n_heads, HEAD_DIM).transpose(0, 2, 1, 3),
                             ((0, 0), (0, 0), (0, Q_ROWS - t_dec), (0, 0))).astype(BF16)
            o_rows = _fox_sample(q_rows, k_new, v_new, lf_new, lf_new_col, cache_fox_k.transpose(0, 2, 1, 3),
                                 cache_fox_v.transpose(0, 2, 1, 3), lf_pool, page_table, t_dec)
            mix_s = o_rows.reshape(n_dec, n_heads, Q_ROWS, HEAD_DIM)[:, :, :t_dec].transpose(0, 2, 1, 3)
            mix = jnp.concatenate([mix_p, mix_s.reshape(ts, fox_w).astype(BF16)], axis=0)
            w_out = w_out_b[j]
        mo = mem_attend(q_mem, i)
        h = _matmul([mix, mo], w_out, [0, mix.shape[1]], 0, d, epilogue="resid", extra=h)
        jj = i // 2
        if i % 2 == 0:
            h = _ffn(h, g_ffn[i], w_ff_gate[jj], w_ff_up[jj], w_ff_down[jj])
        else:
            hf = _rmsnorm(h, g_ffn[i], BF16)
            h = _moe(hf, h, w_router[jj], b_router[jj], w_e_gate[jj], w_e_up[jj], w_e_down[jj])

    y_p = _rmsnorm(h, g_final, F32, 0, tp)
    y_s = _rmsnorm(h, g_final, F32, tp, ts)
    k_p, v_p, kv_s, logf_pad = kv
    logf = logf_pad[:, :n_heads]
    return (y_p.reshape(bsz, seq, d),
            y_s.reshape(n_dec, t_dec, d),
            k_p.transpose(0, 2, 1, 3),
            v_p.transpose(0, 2, 1, 3),
            logf[:tp].reshape(bsz, seq, n_heads),
            jnp.stack(new_conv_p),
            jnp.stack(mem_k_p).transpose(0, 1, 3, 2, 4),
            jnp.stack(mem_v_p).transpose(0, 1, 3, 2, 4),
            kv_s[:, :fox_w].reshape(n_dec, t_dec, n_heads, HEAD_DIM),
            kv_s[:, fox_w:].reshape(n_dec, t_dec, n_heads, HEAD_DIM),
            logf[tp:].reshape(n_dec, t_dec, n_heads),
            jnp.stack(new_conv_s))
```

```python
import functools
import math

import jax
import jax.numpy as jnp
from jax import lax
from jax.experimental import pallas as pl
from jax.experimental.pallas import tpu as pltpu

HEAD_DIM = 128
MEM_HEADS = 4
MEM_WIDTH = MEM_HEADS * HEAD_DIM
TOP_K = 2
EPS = 1e-6
NEG_INF = -1e30
ATTN_SCALE = HEAD_DIM ** -0.5
LOG2_E = math.log2(math.e)
LANES = 128
SUBLANES = 8
HEAD_PAD = 16
Q_ROWS = 8
SAMPLE_Q_PAD = 16
CONV_HALO = 32
VMEM_LIMIT_BYTES = 56 * 1024 * 1024
MOE_SUB = 256
MOE_ROW_TILE = 3 * MOE_SUB
MOE_FFN_SUB = 128
MOE_UNSELECTED = -1.0e9

BF16 = jnp.bfloat16
F32 = jnp.float32


def _cparams(*semantics):
    return pltpu.CompilerParams(dimension_semantics=semantics, vmem_limit_bytes=VMEM_LIMIT_BYTES)


def _pick(n, candidates):
    for c in candidates:
        if c <= n and n % c == 0:
            return c
    return n


def _row_tile(t):
    return _pick(t, (1040, 1024, 640, 512, 256, 128, 64, 32, 16))


def _dot(a, b):
    return jnp.dot(a, b, preferred_element_type=F32)


def _dot_nt(a, b):
    return lax.dot_general(a, b, (((1,), (1,)), ((), ())), preferred_element_type=F32)


def _dot_exact(a, b):
    return jnp.dot(a, b, preferred_element_type=F32, precision=lax.Precision.HIGHEST)


def _log_sigmoid(x):
    return jnp.minimum(x, 0.0) - jnp.log1p(jnp.exp(-jnp.abs(x)))


def _rmsnorm_kernel(x_ref, g_ref, o_ref):
    x = x_ref[...]
    y = x * lax.rsqrt(jnp.mean(x * x, axis=-1, keepdims=True) + EPS)
    o_ref[...] = (y * g_ref[...]).astype(o_ref.dtype)


def _rmsnorm(x, g, out_dtype, row_start=0, n_rows=None):
    d = x.shape[1]
    n_rows = x.shape[0] if n_rows is None else n_rows
    tm = _pick(math.gcd(n_rows, row_start), (1024, 640, 512, 256, 128, 64, 32, 16))
    first = row_start // tm
    return pl.pallas_call(
        _rmsnorm_kernel,
        grid=(n_rows // tm,),
        in_specs=[pl.BlockSpec((tm, d), lambda i: (first + i, 0)),
                  pl.BlockSpec((1, d), lambda i: (0, 0))],
        out_specs=pl.BlockSpec((tm, d), lambda i: (i, 0)),
        out_shape=jax.ShapeDtypeStruct((n_rows, d), out_dtype),
        compiler_params=_cparams("parallel"),
        name="rmsnorm",
    )(x, g.reshape(1, d))


def _mm_kernel(*refs, n_a, epilogue):
    a_refs = refs[:n_a]
    w_refs = refs[n_a:2 * n_a]
    pos = 2 * n_a
    extra_ref = None
    if epilogue in ("resid", "logsig"):
        extra_ref = refs[pos]
        pos += 1
    o_ref = refs[pos]
    wb_refs = refs[pos + 1:pos + 1 + n_a]

    @pl.when(pl.program_id(1) == 0)
    def _():
        for w_ref, wb_ref in zip(w_refs, wb_refs):
            wb_ref[...] = w_ref[...].astype(BF16)

    acc = None
    for a_ref, wb_ref in zip(a_refs, wb_refs):
        d = _dot(a_ref[...], wb_ref[...])
        acc = d if acc is None else acc + d
    if epilogue == "resid":
        acc = extra_ref[...] + acc
    elif epilogue == "logsig":
        acc = _log_sigmoid(acc + extra_ref[...])
    if len(o_ref.shape) == 4:
        for j in range(o_ref.shape[1]):
            o_ref[0, j] = acc[:, j * HEAD_DIM:(j + 1) * HEAD_DIM].astype(o_ref.dtype)
    else:
        o_ref[...] = acc.astype(o_ref.dtype)


def _matmul(a_list, w, row_offs, col_off, n_out, *, epilogue="none", extra=None, out_dtype=F32,
            row_start=0, n_rows=None, heads_of=None):
    n_rows = a_list[0].shape[0] if n_rows is None else n_rows
    align = math.gcd(n_rows, row_start) if heads_of is None else math.gcd(heads_of[1], row_start)
    tm = _row_tile(align)
    tn = _pick(n_out, (1024, 768, 512, 256, 128))
    first = row_start // tm
    n_a = len(a_list)
    in_specs, scratch = [], []
    for a in a_list:
        in_specs.append(pl.BlockSpec((tm, a.shape[1]), lambda n, m: (first + m, 0)))
    for a, ro in zip(a_list, row_offs):
        k = a.shape[1]
        assert ro % k == 0 and col_off % tn == 0
        in_specs.append(pl.BlockSpec((k, tn), functools.partial(
            lambda n, m, rb, cb: (rb, cb + n), rb=ro // k, cb=col_off // tn)))
        scratch.append(pltpu.VMEM((k, tn), BF16))
    args = list(a_list) + [w] * n_a
    if epilogue == "resid":
        in_specs.append(pl.BlockSpec((tm, tn), lambda n, m: (first + m, n)))
        args.append(extra)
    elif epilogue == "logsig":
        in_specs.append(pl.BlockSpec((1, tn), lambda n, m: (0, n)))
        args.append(extra)
    if heads_of is None:
        out_spec = pl.BlockSpec((tm, tn), lambda n, m: (m, n))
        out_shape = jax.ShapeDtypeStruct((n_rows, n_out), out_dtype)
    else:
        bsz, seq = heads_of
        per_seq = seq // tm
        out_spec = pl.BlockSpec((1, tn // HEAD_DIM, tm, HEAD_DIM),
                                lambda n, m: (m // per_seq, n, m % per_seq, 0))
        out_shape = jax.ShapeDtypeStruct((bsz, n_out // HEAD_DIM, seq, HEAD_DIM), out_dtype)
    return pl.pallas_call(
        functools.partial(_mm_kernel, n_a=n_a, epilogue=epilogue),
        grid=(n_out // tn, n_rows // tm),
        in_specs=in_specs,
        out_specs=out_spec,
        out_shape=out_shape,
        scratch_shapes=scratch,
        compiler_params=_cparams("arbitrary", "arbitrary"),
        name="matmul_" + epilogue,
    )(*args)


def _glu_kernel(a_ref, w1_ref, w2_ref, o_ref, wb1_ref, wb2_ref):
    @pl.when(pl.program_id(1) == 0)
    def _():
        wb1_ref[...] = w1_ref[...].astype(BF16)
        wb2_ref[...] = w2_ref[...].astype(BF16)

    a = a_ref[...]
    o_ref[...] = _dot(a, wb1_ref[...]) * jax.nn.sigmoid(_dot(a, wb2_ref[...]))


def _matmul_glu(a, w, n_out):
    t, k = a.shape
    tm = _row_tile(t)
    tn = _pick(n_out, (1024, 768, 512, 256, 128))
    nb = n_out // tn
    return pl.pallas_call(
        _glu_kernel,
        grid=(nb, t // tm),
        in_specs=[pl.BlockSpec((tm, k), lambda n, m: (m, 0)),
                  pl.BlockSpec((k, tn), lambda n, m: (0, n)),
                  pl.BlockSpec((k, tn), lambda n, m: (0, n + nb))],
        out_specs=pl.BlockSpec((tm, tn), lambda n, m: (m, n)),
        out_shape=jax.ShapeDtypeStruct((t, n_out), F32),
        scratch_shapes=[pltpu.VMEM((k, tn), BF16), pltpu.VMEM((k, tn), BF16)],
        compiler_params=_cparams("arbitrary", "arbitrary"),
        name="matmul_glu",
    )(a, w, w)


def _swiglu_partial(x, wg, wu, wd):
    g = _dot(x, wg)
    u = _dot(x, wu)
    return _dot((g * jax.nn.sigmoid(g) * u).astype(BF16), wd)


def _ffn_kernel(h_ref, g_ref, wg_ref, wu_ref, wd_ref, o_ref, x_ref):
    @pl.when(pl.program_id(1) == 0)
    def _():
        h = h_ref[...]
        o_ref[...] = h
        y = h * lax.rsqrt(jnp.mean(h * h, axis=-1, keepdims=True) + EPS)
        x_ref[...] = (y * g_ref[...]).astype(BF16)

    o_ref[...] += _swiglu_partial(x_ref[...], wg_ref[...].astype(BF16), wu_ref[...].astype(BF16),
                                  wd_ref[...].astype(BF16))


def _ffn(h, g, w_gate, w_up, w_down):
    t, d = h.shape
    f = w_gate.shape[1]
    tm = _row_tile(t)
    tf = _pick(f, (512, 256, 128))
    return pl.pallas_call(
        _ffn_kernel,
        grid=(t // tm, f // tf),
        in_specs=[pl.BlockSpec((tm, d), lambda m, j: (m, 0), pipeline_mode=pl.Buffered(1)),
                  pl.BlockSpec((1, d), lambda m, j: (0, 0)),
                  pl.BlockSpec((d, tf), lambda m, j: (0, j)),
                  pl.BlockSpec((d, tf), lambda m, j: (0, j)),
                  pl.BlockSpec((tf, d), lambda m, j: (j, 0))],
        out_specs=pl.BlockSpec((tm, d), lambda m, j: (m, 0), pipeline_mode=pl.Buffered(1)),
        out_shape=jax.ShapeDtypeStruct((t, d), F32),
        scratch_shapes=[pltpu.VMEM((tm, d), BF16)],
        compiler_params=_cparams("parallel", "arbitrary"),
        name="ffn",
    )(h, g.reshape(1, d), w_gate, w_up, w_down)


def _router_kernel(x_ref, w_ref, b_ref, o_ref, *, n_experts):
    logits = _dot(x_ref[...], w_ref[...].astype(BF16)) + b_ref[...]
    lane = lax.broadcasted_iota(jnp.int32, logits.shape, 1)
    logits = jnp.where(lane < n_experts, logits, NEG_INF)
    v1 = jnp.max(logits, axis=-1, keepdims=True)
    i1 = jnp.min(jnp.where(logits == v1, lane, LANES), axis=-1, keepdims=True)
    rest = jnp.where(lane == i1, NEG_INF, logits)
    v2 = jnp.max(rest, axis=-1, keepdims=True)
    i2 = jnp.min(jnp.where(rest == v2, lane, LANES), axis=-1, keepdims=True)
    e2 = jnp.exp(v2 - v1)
    p1 = 1.0 / (1.0 + e2)
    p2 = e2 / (1.0 + e2)
    o_ref[...] = jnp.where(lane == i1, p1, jnp.where(lane == i2, p2, 0.0))


def _router(x, w_router, b_router):
    t, d = x.shape
    n_e = w_router.shape[-1]
    tm = _row_tile(t)
    w_pad = jnp.pad(w_router, ((0, 0), (0, LANES - n_e)))
    b_pad = jnp.pad(b_router, (0, LANES - n_e)).reshape(1, LANES)
    return pl.pallas_call(
        functools.partial(_router_kernel, n_experts=n_e),
        grid=(t // tm,),
        in_specs=[pl.BlockSpec((tm, d), lambda m: (m, 0)),
                  pl.BlockSpec((d, LANES), lambda m: (0, 0)),
                  pl.BlockSpec((1, LANES), lambda m: (0, 0))],
        out_specs=pl.BlockSpec((tm, LANES), lambda m: (m, 0)),
        out_shape=jax.ShapeDtypeStruct((t, LANES), F32),
        compiler_params=_cparams("parallel"),
        name="router",
    )(x, w_pad, b_pad)


def _moe_rank_kernel(g_ref, pos_ref, pos_t_ref, start_ref, cnt_ref, carry_ref):
    i = pl.program_id(0)
    tm = g_ref.shape[0]

    @pl.when(i == 0)
    def _():
        carry_ref[...] = jnp.zeros(carry_ref.shape, F32)

    sel = g_ref[...] > 0.0
    r = lax.broadcasted_iota(jnp.int32, (tm, tm), 0)
    c = lax.broadcasted_iota(jnp.int32, (tm, tm), 1)
    before = jnp.where(c < r, 1.0, 0.0).astype(BF16)
    ones = jnp.where(sel, 1.0, 0.0)
    carry = carry_ref[...]
    rank = _dot(before, ones.astype(BF16)) + carry
    start_ref[...] = jnp.broadcast_to(carry, start_ref.shape)
    carry = rank[tm - 1:tm, :] + ones[tm - 1:tm, :]
    carry_ref[...] = carry
    cnt_ref[...] = jnp.broadcast_to(carry, cnt_ref.shape)
    pos = jnp.where(sel, rank, MOE_UNSELECTED)
    pos_ref[...] = pos
    pos_t_ref[...] = pos.T[0:HEAD_PAD, :]


def _moe_rank(gates, tc):
    t = gates.shape[0]
    nch = t // tc
    return pl.pallas_call(
        _moe_rank_kernel,
        grid=(nch,),
        in_specs=[pl.BlockSpec((tc, LANES), lambda i: (i, 0))],
        out_specs=[pl.BlockSpec((tc, LANES), lambda i: (i, 0)),
                   pl.BlockSpec((HEAD_PAD, tc), lambda i: (0, i)),
                   pl.BlockSpec((8, LANES), lambda i: (i, 0)),
                   pl.BlockSpec((8, LANES), lambda i: (0, 0))],
        out_shape=[jax.ShapeDtypeStruct((t, LANES), F32),
                   jax.ShapeDtypeStruct((HEAD_PAD, t), F32),
                   jax.ShapeDtypeStruct((nch * 8, LANES), F32),
                   jax.ShapeDtypeStruct((8, LANES), F32)],
        scratch_shapes=[pltpu.VMEM((1, LANES), F32)],
        compiler_params=_cparams("arbitrary"),
        name="moe_rank",
    )(gates)


def _moe_gather_kernel(sb_ref, ch_ref, first_ref, last_ref, valid_ref, esb_ref, base_ref,
                       pos_t_ref, x_ref, o_ref, acc_ref):
    i = pl.program_id(0)

    @pl.when(valid_ref[i] == 1)
    def _():
        sb = sb_ref[i]
        local = pos_t_ref[pl.ds(esb_ref[sb], 1), :].astype(jnp.int32) + base_ref[sb]
        rows = lax.broadcasted_iota(jnp.int32, (o_ref.shape[0], local.shape[1]), 0)
        onehot = jnp.where(local == rows, 1.0, 0.0).astype(BF16)
        part = _dot(onehot, x_ref[...])

        @pl.when(first_ref[i] == 1)
        def _():
            acc_ref[...] = part

        @pl.when(first_ref[i] == 0)
        def _():
            acc_ref[...] += part

        @pl.when(last_ref[i] == 1)
        def _():
            o_ref[...] = acc_ref[...].astype(o_ref.dtype)


def _moe_gather(plan, pos_t, x, n_rows, tc):
    d = x.shape[1]
    sb = MOE_SUB
    prefetch = (plan["g_sb"], plan["g_ch"], plan["g_first"], plan["g_last"], plan["g_valid"],
                plan["e_sb"], plan["base_sb"])
    grid_spec = pltpu.PrefetchScalarGridSpec(
        num_scalar_prefetch=len(prefetch),
        grid=(plan["g_sb"].shape[0],),
        in_specs=[pl.BlockSpec((HEAD_PAD, tc), lambda i, sbl, chl, *_: (0, chl[i])),
                  pl.BlockSpec((tc, d), lambda i, sbl, chl, *_: (chl[i], 0))],
        out_specs=pl.BlockSpec((sb, d), lambda i, sbl, chl, *_: (sbl[i], 0)),
        scratch_shapes=[pltpu.VMEM((sb, d), F32)],
    )
    return pl.pallas_call(
        _moe_gather_kernel,
        grid_spec=grid_spec,
        out_shape=jax.ShapeDtypeStruct((n_rows, d), BF16),
        compiler_params=_cparams("arbitrary"),
        name="moe_gather",
    )(*prefetch, pos_t, x)


def _moe_ffn_kernel(et_ref, nv_ref, x_ref, wg_ref, wu_ref, wd_ref, o_ref, acc_ref):
    del et_ref
    r = pl.program_id(0)
    f = pl.program_id(1)
    last_f = pl.num_programs(1) - 1
    n_valid = nv_ref[r]
    n_sub = o_ref.shape[0] // MOE_FFN_SUB

    @pl.when(f == 0)
    def _():
        acc_ref[...] = jnp.zeros(acc_ref.shape, F32)

    for n in range(1, n_sub + 1):
        @pl.when(n_valid == n)
        def _():
            rows = slice(0, n * MOE_FFN_SUB)
            acc_ref[rows, :] += _swiglu_partial(x_ref[rows, :], wg_ref[0].astype(BF16), wu_ref[0].astype(BF16),
                                                wd_ref[0].astype(BF16))

    @pl.when(f == last_f)
    def _():
        o_ref[...] = acc_ref[...].astype(o_ref.dtype)


def _moe_ffn(plan, xs, w_gate, w_up, w_down):
    n_rows, d = xs.shape
    f = w_gate.shape[2]
    tr = MOE_ROW_TILE
    tf = _pick(f, (512, 256, 128))
    nf = f // tf

    def w_in_map(r, j, et, nv):
        return (et[r], 0, jnp.where(nv[r] > 0, j, nf - 1))

    def w_out_map(r, j, et, nv):
        return (et[r], jnp.where(nv[r] > 0, j, nf - 1), 0)

    grid_spec = pltpu.PrefetchScalarGridSpec(
        num_scalar_prefetch=2,
        grid=(n_rows // tr, nf),
        in_specs=[pl.BlockSpec((tr, d), lambda r, j, et, nv: (r, 0)),
                  pl.BlockSpec((1, d, tf), w_in_map),
                  pl.BlockSpec((1, d, tf), w_in_map),
                  pl.BlockSpec((1, tf, d), w_out_map)],
        out_specs=pl.BlockSpec((tr, d), lambda r, j, et, nv: (r, 0)),
        scratch_shapes=[pltpu.VMEM((tr, d), F32)],
    )
    return pl.pallas_call(
        _moe_ffn_kernel,
        grid_spec=grid_spec,
        out_shape=jax.ShapeDtypeStruct((n_rows, d), BF16),
        compiler_params=_cparams("arbitrary", "arbitrary"),
        name="moe_ffn",
    )(plan["e_tile"], plan["nv_tile"], xs, w_gate, w_up, w_down)


def _moe_combine_kernel(ch_ref, sb_ref, first_ref, valid_ref, esb_ref, base_ref,
                        pos_ref, gate_ref, h_ref, y_ref, o_ref):
    i = pl.program_id(0)

    @pl.when(first_ref[i] == 1)
    def _():
        o_ref[...] = h_ref[...]

    @pl.when(valid_ref[i] == 1)
    def _():
        sb = sb_ref[i]
        lane = lax.broadcasted_iota(jnp.int32, pos_ref.shape, 1)
        mine = lane == esb_ref[sb]
        rank = jnp.sum(jnp.where(mine, pos_ref[...], 0.0), axis=-1, keepdims=True)
        local = rank.astype(jnp.int32) + base_ref[sb]
        gate = jnp.sum(jnp.where(mine, gate_ref[...], 0.0), axis=-1, keepdims=True)
        cols = lax.broadcasted_iota(jnp.int32, (pos_ref.shape[0], y_ref.shape[0]), 1)
        onehot = jnp.where(local == cols, 1.0, 0.0).astype(BF16)
        o_ref[...] += gate * _dot(onehot, y_ref[...])


def _moe_combine(plan, pos, gates, h, ys, tc):
    t, d = h.shape
    sb = MOE_SUB
    prefetch = (plan["c_ch"], plan["c_sb"], plan["c_first"], plan["c_valid"], plan["e_sb"], plan["base_sb"])
    grid_spec = pltpu.PrefetchScalarGridSpec(
        num_scalar_prefetch=len(prefetch),
        grid=(plan["c_ch"].shape[0],),
        in_specs=[pl.BlockSpec((tc, LANES), lambda i, chl, sbl, *_: (chl[i], 0)),
                  pl.BlockSpec((tc, LANES), lambda i, chl, sbl, *_: (chl[i], 0)),
                  pl.BlockSpec((tc, d), lambda i, chl, sbl, *_: (chl[i], 0)),
                  pl.BlockSpec((sb, d), lambda i, chl, sbl, *_: (sbl[i], 0))],
        out_specs=pl.BlockSpec((tc, d), lambda i, chl, sbl, *_: (chl[i], 0)),
    )
    return pl.pallas_call(
        _moe_combine_kernel,
        grid_spec=grid_spec,
        out_shape=jax.ShapeDtypeStruct((t, d), F32),
        compiler_params=_cparams("arbitrary"),
        name="moe_combine",
    )(*prefetch, pos, gates, h, ys)


def _pair_list(overlap, length):
    n_b = overlap.shape[1]
    flat = overlap.reshape(-1)
    n = jnp.sum(flat.astype(jnp.int32))
    idx = jnp.nonzero(flat, size=length, fill_value=0)[0].astype(jnp.int32)
    step = jnp.arange(length, dtype=jnp.int32)
    valid = step < n
    idx = jnp.where(valid, idx, idx[jnp.maximum(n - 1, 0)])
    a = idx // n_b
    b = idx % n_b
    nxt_valid = jnp.concatenate([valid[1:], jnp.zeros((1,), bool)])
    first = valid & (a != jnp.concatenate([jnp.full((1,), -1, jnp.int32), a[:-1]]))
    last = valid & ((a != jnp.concatenate([a[1:], jnp.full((1,), -1, jnp.int32)])) | ~nxt_valid)
    as_i32 = lambda v: v.astype(jnp.int32)
    return a, b, as_i32(first), as_i32(last), as_i32(valid)


def _moe_plan(counts, chunk_start, n_rows, nch):
    n_e = counts.shape[0]
    tr, sb = MOE_ROW_TILE, MOE_SUB
    seg = (counts + tr - 1) // tr * tr
    seg_end = jnp.cumsum(seg)
    off = seg_end - seg
    chunk_end = jnp.concatenate([chunk_start[:, 1:], counts[:, None]], axis=1)

    def owner(start):
        return jnp.minimum(jnp.sum(seg_end[None, :] <= start[:, None], axis=1), n_e - 1).astype(jnp.int32)

    sb_start = jnp.arange(n_rows // sb, dtype=jnp.int32) * sb
    e_sb = owner(sb_start)
    k0 = sb_start - off[e_sb]
    k1 = jnp.minimum(k0 + sb, counts[e_sb])
    sb_used = (sb_start < seg_end[-1]) & (k0 < counts[e_sb])
    overlap = sb_used[:, None] & (chunk_start[e_sb] < k1[:, None]) & (chunk_end[e_sb] > k0[:, None])
    length = n_rows // sb + n_e * nch
    g_sb, g_ch, g_first, g_last, g_valid = _pair_list(overlap, length)
    c_ch, c_sb, c_first, _, c_valid = _pair_list(overlap.T, length)

    tile_start = jnp.arange(n_rows // tr, dtype=jnp.int32) * tr
    e_tile = owner(tile_start)
    left = counts[e_tile] - (tile_start - off[e_tile])
    step = MOE_FFN_SUB
    nv_tile = jnp.where(tile_start < seg_end[-1], jnp.clip((left + step - 1) // step, 0, tr // step), 0)
    return dict(g_sb=g_sb, g_ch=g_ch, g_first=g_first, g_last=g_last, g_valid=g_valid,
                c_ch=c_ch, c_sb=c_sb, c_first=c_first, c_valid=c_valid,
                e_sb=e_sb, base_sb=(off[e_sb] - sb_start).astype(jnp.int32),
                e_tile=e_tile, nv_tile=nv_tile.astype(jnp.int32))


def _moe(x, h, w_router, b_router, w_gate, w_up, w_down):
    t = x.shape[0]
    n_e = w_router.shape[-1]
    tc = _pick(t, (640, 512, 384, 256, 128))
    assert tc % LANES == 0 and t % tc == 0 and n_e <= 8
    nch = t // tc
    n_rows = -(-(TOP_K * t + n_e * (MOE_ROW_TILE - 1)) // MOE_ROW_TILE) * MOE_ROW_TILE
    gates = _router(x, w_router, b_router)
    pos, pos_t, starts, totals = _moe_rank(gates, tc)
    counts = totals[0, :n_e].astype(jnp.int32)
    chunk_start = starts.reshape(nch, 8, LANES)[:, 0, :n_e].T.astype(jnp.int32)
    plan = _moe_plan(counts, chunk_start, n_rows, nch)
    xs = _moe_gather(plan, pos_t, x, n_rows, tc)
    ys = _moe_ffn(plan, xs, w_gate, w_up, w_down)
    return _moe_combine(plan, pos, gates, h, ys, tc)


def _ln_swish(hc, lg, lb):
    mu = jnp.mean(hc, axis=-1, keepdims=True)
    xc = hc - mu
    var = jnp.mean(xc * xc, axis=-1, keepdims=True)
    y = xc * lax.rsqrt(var + EPS) * lg + lb
    return y * jax.nn.sigmoid(y)


def _conv_kernel(prev_ref, main_ref, w_ref, b_ref, lg_ref, lb_ref, o_ref, ext_ref, acc_ref, sh_ref, *, tt, width):
    @pl.when(pl.program_id(1) == 0)
    def _():
        ext_ref[0:CONV_HALO, :] = jnp.zeros((CONV_HALO, ext_ref.shape[1]), F32)

    @pl.when(pl.program_id(1) > 0)
    def _():
        ext_ref[0:CONV_HALO, :] = prev_ref[...]

    ext_ref[CONV_HALO:CONV_HALO + tt, :] = main_ref[...]
    n_chunks = o_ref.shape[-1] // LANES
    lead = CONV_HALO - (width - 1)

    span = tt + CONV_HALO - SUBLANES

    def chunk(c, carry):
        c0 = pl.multiple_of(c * LANES, LANES)
        for r in range(1, SUBLANES):
            sh_ref[r - 1] = ext_ref[pl.ds(r, span), pl.ds(c0, LANES)]
        acc = jnp.zeros((tt, LANES), F32)
        for j in range(width):
            phase = (lead + j) % SUBLANES
            base = lead + j - phase
            if phase == 0:
                rows = ext_ref[pl.ds(base, tt), pl.ds(c0, LANES)]
            else:
                rows = sh_ref[phase - 1, pl.ds(base, tt), :]
            acc = acc + rows * w_ref[pl.ds(j, 1), pl.ds(c0, LANES)]
        acc_ref[:, pl.ds(c0, LANES)] = acc + b_ref[:, pl.ds(c0, LANES)]
        return carry

    lax.fori_loop(0, n_chunks, chunk, 0)
    o_ref[...] = _ln_swish(acc_ref[...], lg_ref[...], lb_ref[...]).astype(o_ref.dtype)


def _conv_prompt(u, w, b, lg, lb, bsz, seq):
    c = u.shape[1]
    width = w.shape[0]
    tt = _pick(seq, (256, 128, 64, 32))
    nt = seq // tt
    per_halo = tt // CONV_HALO
    return pl.pallas_call(
        functools.partial(_conv_kernel, tt=tt, width=width),
        grid=(bsz, nt),
        in_specs=[pl.BlockSpec((CONV_HALO, c), lambda bb, i: (jnp.maximum((bb * nt + i) * per_halo - 1, 0), 0)),
                  pl.BlockSpec((tt, c), lambda bb, i: (bb * nt + i, 0)),
                  pl.BlockSpec((width, c), lambda bb, i: (0, 0)),
                  pl.BlockSpec((1, c), lambda bb, i: (0, 0)),
                  pl.BlockSpec((1, c), lambda bb, i: (0, 0)),
                  pl.BlockSpec((1, c), lambda bb, i: (0, 0))],
        out_specs=pl.BlockSpec((tt, c), lambda bb, i: (bb * nt + i, 0)),
        out_shape=jax.ShapeDtypeStruct((u.shape[0], c), BF16),
        scratch_shapes=[pltpu.VMEM((tt + CONV_HALO, c), F32), pltpu.VMEM((tt, c), F32),
                        pltpu.VMEM((SUBLANES - 1, tt + CONV_HALO - SUBLANES, LANES), F32)],
        compiler_params=_cparams("parallel", "parallel"),
        name="conv_prompt",
    )(u, u, w, b.reshape(1, c), lg.reshape(1, c), lb.reshape(1, c))


def _conv_step_kernel(ext_ref, w_ref, b_ref, lg_ref, lb_ref, o_ref, *, width):
    n_t = o_ref.shape[0]
    for t in range(n_t):
        acc = ext_ref[t] * w_ref[0:1, :]
        for j in range(1, width):
            acc = acc + ext_ref[t + j] * w_ref[j:j + 1, :]
        o_ref[t] = _ln_swish(acc + b_ref[...], lg_ref[...], lb_ref[...]).astype(o_ref.dtype)


def _conv_sample(ext_t, w, b, lg, lb):
    l_ext, n, c = ext_t.shape
    width = w.shape[0]
    n_t = l_ext - (width - 1)
    return pl.pallas_call(
        functools.partial(_conv_step_kernel, width=width),
        out_shape=jax.ShapeDtypeStruct((n_t, n, c), BF16),
        compiler_params=pltpu.CompilerParams(vmem_limit_bytes=VMEM_LIMIT_BYTES),
        name="conv_sample",
    )(ext_t, w, b.reshape(1, c), lg.reshape(1, c), lb.reshape(1, c))


def _mem_attn_kernel(q_ref, k_ref, v_ref, o_ref):
    keys_on_sublanes = q_ref.shape[0] % LANES == 0
    for hh in range(MEM_HEADS):
        sl = slice(hh * HEAD_DIM, (hh + 1) * HEAD_DIM)
        q = q_ref[:, sl].astype(BF16)
        k = k_ref[0, hh].astype(BF16)
        if keys_on_sublanes:
            s = _dot_nt(k, q) * ATTN_SCALE
            p = jnp.exp(s - jnp.max(s, axis=0, keepdims=True))
            l = jnp.sum(p, axis=0, keepdims=True)
            o = _dot(v_ref[0, hh].T.astype(BF16), p.astype(BF16)) / l
            o_ref[:, sl] = o.T.astype(o_ref.dtype)
        else:
            s = _dot_nt(q, k) * ATTN_SCALE
            p = jnp.exp(s - jnp.max(s, axis=-1, keepdims=True))
            l = jnp.sum(p, axis=-1, keepdims=True)
            o_ref[:, sl] = (_dot(p.astype(BF16), v_ref[0, hh].astype(BF16)) / l).astype(o_ref.dtype)


def _mem_attn(q, k, v, rows_per_req):
    w = q.shape[1]
    n, _, m, _ = k.shape
    tq = _pick(rows_per_req, (512, 256, 128, 64, 32, 16))
    per_req = rows_per_req // tq
    return pl.pallas_call(
        _mem_attn_kernel,
        grid=(n, per_req),
        in_specs=[pl.BlockSpec((tq, w), lambda i, j: (i * per_req + j, 0)),
                  pl.BlockSpec((1, MEM_HEADS, m, HEAD_DIM), lambda i, j: (i, 0, 0, 0)),
                  pl.BlockSpec((1, MEM_HEADS, m, HEAD_DIM), lambda i, j: (i, 0, 0, 0))],
        out_specs=pl.BlockSpec((tq, w), lambda i, j: (i * per_req + j, 0)),
        out_shape=jax.ShapeDtypeStruct((max(q.shape[0], n * rows_per_req), w), BF16),
        compiler_params=_cparams("parallel", "parallel"),
        name="mem_attn",
    )(q, k, v)


def _cumsum_kernel(lf_ref, crep_ref, crow_ref, *, blk, n_heads):
    seq = lf_ref.shape[1]
    r = lax.broadcasted_iota(jnp.int32, (blk, blk), 0)
    c = lax.broadcasted_iota(jnp.int32, (blk, blk), 1)
    tri = (c <= r).astype(F32)
    carry = jnp.zeros((1, LANES), F32)
    for i in range(seq // blk):
        rows = slice(i * blk, (i + 1) * blk)
        cs = _dot_exact(tri, lf_ref[0, rows, :]) + carry
        carry = cs[blk - 1:blk, :]
        cs2 = cs * LOG2_E
        crow_ref[0, :, rows] = cs2.T[0:HEAD_PAD, :]
        for hh in range(n_heads):
            crep_ref[0, hh, rows, :] = jnp.broadcast_to(cs2[:, hh:hh + 1], (blk, LANES))


def _fox_cumsum(logf_pad, n_heads):
    bsz, seq, _ = logf_pad.shape
    blk = _pick(seq, (256, 128))
    return pl.pallas_call(
        functools.partial(_cumsum_kernel, blk=blk, n_heads=n_heads),
        grid=(bsz,),
        in_specs=[pl.BlockSpec((1, seq, LANES), lambda b: (b, 0, 0))],
        out_specs=[pl.BlockSpec((1, n_heads, seq, LANES), lambda b: (b, 0, 0, 0)),
                   pl.BlockSpec((1, HEAD_PAD, seq), lambda b: (b, 0, 0))],
        out_shape=[jax.ShapeDtypeStruct((bsz, n_heads, seq, LANES), F32),
                   jax.ShapeDtypeStruct((bsz, HEAD_PAD, seq), F32)],
        compiler_params=_cparams("parallel"),
        name="fox_cumsum",
    )(logf_pad)


def _fox_prompt_kernel(q_ref, k_ref, v_ref, crep_ref, crow_ref, o_ref,
                       kb_ref, vt_ref, m_ref, l_ref, acc_ref, *, tq, tk):
    hh = pl.program_id(1)
    qi = pl.program_id(2)

    @pl.when(qi == 0)
    def _():
        kb_ref[...] = k_ref[0, 0].astype(BF16)
        vt_ref[...] = v_ref[0, 0].T.astype(BF16)

    q = q_ref[...].astype(BF16)
    q0 = pl.multiple_of(qi * tq, tq)
    cq = crow_ref[0, pl.ds(hh, 1), pl.ds(q0, tq)]
    m_ref[...] = jnp.full(m_ref.shape, NEG_INF, F32)
    l_ref[...] = jnp.zeros(l_ref.shape, F32)
    acc_ref[...] = jnp.zeros(acc_ref.shape, F32)

    def update(k0, on_diagonal):
        ck = jnp.tile(crep_ref[0, 0, pl.ds(k0, tk), :], (1, tq // LANES))
        z = _dot_nt(kb_ref[pl.ds(k0, tk), :], q) * (ATTN_SCALE * LOG2_E) - ck
        if on_diagonal:
            z = jnp.where(lax.broadcasted_iota(jnp.int32, (tk, tq), 0) <= lax.broadcasted_iota(jnp.int32, (tk, tq), 1),
                          z, NEG_INF)
        m_old = m_ref[...]
        m_new = jnp.maximum(m_old, jnp.max(z, axis=0, keepdims=True) + cq)
        alpha = jnp.exp2(m_old - m_new)
        p = jnp.exp2(z - (m_new - cq))
        l_ref[...] = alpha * l_ref[...] + jnp.sum(p, axis=0, keepdims=True)
        acc_ref[...] = alpha * acc_ref[...] + _dot(vt_ref[:, pl.ds(k0, tk)], p.astype(BF16))
        m_ref[...] = m_new

    def body(kj, carry):
        update(pl.multiple_of(kj * tk, tk), False)
        return carry

    lax.fori_loop(0, qi, body, 0)
    update(q0, True)
    o_ref[...] = (acc_ref[...] / l_ref[...]).T.astype(o_ref.dtype)


def _fox_prompt(q, k, v, crep, crow):
    bsz, n_heads, seq, _ = k.shape
    tq = _pick(seq, (512, 256, 128))
    tk = tq
    nq = seq // tq
    head_seq = pl.BlockSpec((1, 1, seq, HEAD_DIM), lambda b, h, i: (b, h, 0, 0))
    return pl.pallas_call(
        functools.partial(_fox_prompt_kernel, tq=tq, tk=tk),
        grid=(bsz, n_heads, nq),
        in_specs=[pl.BlockSpec((tq, HEAD_DIM), lambda b, h, i: (b * nq + i, h)),
                  head_seq,
                  head_seq,
                  pl.BlockSpec((1, 1, seq, LANES), lambda b, h, i: (b, h, 0, 0)),
                  pl.BlockSpec((1, HEAD_PAD, seq), lambda b, h, i: (b, 0, 0))],
        out_specs=pl.BlockSpec((tq, HEAD_DIM), lambda b, h, i: (b * nq + i, h)),
        out_shape=jax.ShapeDtypeStruct((q.shape[0], n_heads * HEAD_DIM), BF16),
        scratch_shapes=[pltpu.VMEM((seq, HEAD_DIM), BF16), pltpu.VMEM((HEAD_DIM, seq), BF16),
                        pltpu.VMEM((1, tq), F32), pltpu.VMEM((1, tq), F32),
                        pltpu.VMEM((HEAD_DIM, tq), F32)],
        compiler_params=_cparams("parallel", "parallel", "arbitrary"),
        name="fox_prompt",
    )(q, k, v, crep, crow)


def _fox_sample_kernel(*refs, pages_per_step, n_new, n_heads):
    pp = pages_per_step
    q_ref, knew_ref, vnew_ref, lfnew_ref, lfcol_ref = refs[1:6]
    k_refs = refs[6:6 + pp]
    v_refs = refs[6 + pp:6 + 2 * pp]
    lf_refs = refs[6 + 2 * pp:6 + 3 * pp]
    o_ref = refs[6 + 3 * pp]
    m_ref, l_ref, acc_ref, carry_ref, cn_ref, knp_ref, vnp_ref = refs[7 + 3 * pp:]
    step = pl.program_id(1)
    page = k_refs[0].shape[2]
    n_rows = n_heads * Q_ROWS

    r = lax.broadcasted_iota(jnp.int32, (page, page), 0)
    c = lax.broadcasted_iota(jnp.int32, (page, page), 1)

    @pl.when(step == 0)
    def _():
        m_ref[...] = jnp.full(m_ref.shape, NEG_INF, F32)
        l_ref[...] = jnp.zeros(l_ref.shape, F32)
        acc_ref[...] = jnp.zeros(acc_ref.shape, F32)
        carry_ref[...] = jnp.zeros(carry_ref.shape, F32)
        lf_col = lfcol_ref[0]
        tok = lax.broadcasted_iota(jnp.int32, lf_col.shape, 0)
        c_col = jnp.zeros(lf_col.shape, F32)
        for j in range(n_new):
            c_col = c_col + jnp.where(tok >= j, lf_col[j:j + 1, :], 0.0)
        for hh in range(n_heads):
            cn_ref[hh * Q_ROWS:(hh + 1) * Q_ROWS, :] = c_col[:, hh:hh + 1]

    def per_head_rows(x16):
        return jnp.concatenate([jnp.broadcast_to(x16[hh:hh + 1, :], (Q_ROWS, page)) for hh in range(n_heads)],
                               axis=0)

    def attend(k_list, v_list, bias, mask=None):
        s = jnp.concatenate(
            [jnp.concatenate([_dot_nt(q_ref[0, hh], k_ref[0, hh].astype(BF16)) for hh in range(n_heads)], axis=0)
             for k_ref in k_list], axis=1)
        s = s * ATTN_SCALE + bias
        if mask is not None:
            s = jnp.where(mask, s, NEG_INF)
        m_old = m_ref[...]
        m_new = jnp.maximum(m_old, jnp.max(s, axis=-1, keepdims=True))
        alpha = jnp.exp(m_old - m_new)
        p = jnp.exp(s - m_new)
        l_ref[...] = alpha * l_ref[...] + jnp.sum(p, axis=-1, keepdims=True)
        pv = None
        for j, v_ref in enumerate(v_list):
            part = jnp.concatenate(
                [_dot(p[hh * Q_ROWS:(hh + 1) * Q_ROWS, j * page:(j + 1) * page].astype(BF16),
                      v_ref[0, hh].astype(BF16)) for hh in range(n_heads)], axis=0)
            pv = part if pv is None else pv + part
        acc_ref[...] = alpha * acc_ref[...] + pv
        m_ref[...] = m_new

    suffix = (r > c).astype(F32)
    carry = carry_ref[...]
    decay = []
    for j in range(pp):
        lf = lf_refs[j][0]
        decay.append(per_head_rows(_dot_exact(lf, suffix) + carry))
        carry = carry + jnp.sum(lf, axis=-1, keepdims=True)
    carry_ref[...] = carry
    attend(k_refs, v_refs, cn_ref[...] + jnp.concatenate(decay, axis=1))

    @pl.when(step == pl.num_programs(1) - 1)
    def _():
        c_row = _dot_exact(lfnew_ref[0], (r <= c).astype(F32))
        key = lax.broadcasted_iota(jnp.int32, (n_rows, page), 1)
        tok = lax.broadcasted_iota(jnp.int32, (n_rows, page), 0) % Q_ROWS
        knp_ref[...] = jnp.zeros(knp_ref.shape, F32)
        vnp_ref[...] = jnp.zeros(vnp_ref.shape, F32)
        for hh in range(n_heads):
            knp_ref[0, hh, 0:Q_ROWS, :] = knew_ref[0, hh]
            vnp_ref[0, hh, 0:Q_ROWS, :] = vnew_ref[0, hh]
        attend([knp_ref], [vnp_ref], cn_ref[...] - per_head_rows(c_row), mask=key <= tok)
        o_ref[0] = acc_ref[...] / l_ref[...]


def _fox_sample(q_rows, k_new, v_new, lf_new, lf_new_col, k_pool, v_pool, lf_pool, page_table, n_new):
    n, n_heads = q_rows.shape[:2]
    page = k_pool.shape[2]
    n_pages = page_table.shape[1]
    pp = _pick(n_pages, (8, 4, 2, 1))
    n_steps = n_pages // pp
    n_rows = n_heads * Q_ROWS

    def kv_map(j):
        return lambda i, s, pt: (pt[i, n_pages - 1 - (s * pp + j)], 0, 0, 0)

    def lf_map(j):
        return lambda i, s, pt: (pt[i, n_pages - 1 - (s * pp + j)], 0, 0)

    req3 = lambda i, s, pt: (i, 0, 0)
    req4 = lambda i, s, pt: (i, 0, 0, 0)
    kv_block = (1, n_heads, page, HEAD_DIM)
    new_block = pl.BlockSpec((1, n_heads, Q_ROWS, HEAD_DIM), req4)
    in_specs = [new_block, new_block, new_block,
                pl.BlockSpec((1, HEAD_PAD, page), req3),
                pl.BlockSpec((1, Q_ROWS, LANES), req3)]
    in_specs += [pl.BlockSpec(kv_block, kv_map(j)) for j in range(pp)]
    in_specs += [pl.BlockSpec(kv_block, kv_map(j)) for j in range(pp)]
    in_specs += [pl.BlockSpec((1, HEAD_PAD, page), lf_map(j)) for j in range(pp)]
    grid_spec = pltpu.PrefetchScalarGridSpec(
        num_scalar_prefetch=1,
        grid=(n, n_steps),
        in_specs=in_specs,
        out_specs=pl.BlockSpec((1, n_rows, HEAD_DIM), req3),
        scratch_shapes=[pltpu.VMEM((n_rows, 1), F32), pltpu.VMEM((n_rows, 1), F32),
                        pltpu.VMEM((n_rows, HEAD_DIM), F32), pltpu.VMEM((HEAD_PAD, 1), F32),
                        pltpu.VMEM((n_rows, 1), F32), pltpu.VMEM(kv_block, F32), pltpu.VMEM(kv_block, F32)],
    )
    return pl.pallas_call(
        functools.partial(_fox_sample_kernel, pages_per_step=pp, n_new=n_new, n_heads=n_heads),
        grid_spec=grid_spec,
        out_shape=jax.ShapeDtypeStruct((n, n_rows, HEAD_DIM), F32),
        compiler_params=_cparams("parallel", "arbitrary"),
        name="fox_sample",
    )(page_table, q_rows, k_new, v_new, lf_new, lf_new_col,
      *([k_pool] * pp), *([v_pool] * pp), *([lf_pool] * pp))


def kernel(x_prompt, x_sample, cache_fox_k, cache_fox_v, cache_fox_logf, state_conv, cache_mem_k, cache_mem_v,
           page_table, mem_prompt, g_mix, g_ffn, g_mem, w_mem_kv, w_in_a, w_out_a, conv_w, conv_b, conv_ln_g,
           conv_ln_b, g_kv, w_kvf, b_f, w_in_b, w_out_b, w_ff_gate, w_ff_up, w_ff_down, w_router, b_router,
           w_e_gate, w_e_up, w_e_down, g_final):
    bsz, seq, d = x_prompt.shape
    n_dec, t_dec, _ = x_sample.shape
    depth = g_mix.shape[0]
    n_a = w_in_a.shape[0]
    conv_ch = conv_w.shape[-1]
    width = conv_w.shape[1]
    n_mem = mem_prompt.shape[1]
    fox_w = w_in_b.shape[-1] - MEM_WIDTH
    n_heads = fox_w // HEAD_DIM
    n_pool, page = cache_fox_k.shape[:2]
    tp = bsz * seq
    ts = n_dec * t_dec
    assert n_heads <= HEAD_PAD and width - 1 <= CONV_HALO and t_dec <= 8

    h = jnp.concatenate([x_prompt.reshape(tp, d), x_sample.reshape(ts, d)], axis=0)

    mem_flat = mem_prompt.reshape(bsz * n_mem, d)
    mem_k_p, mem_v_p = [], []
    for i in range(depth):
        z = _matmul([_rmsnorm(mem_flat, g_mem[i], BF16)], w_mem_kv[i], [0], 0, 2 * MEM_WIDTH)
        mem_k_p.append(z[:, :MEM_WIDTH].reshape(bsz, n_mem, MEM_HEADS, HEAD_DIM).transpose(0, 2, 1, 3))
        mem_v_p.append(z[:, MEM_WIDTH:].reshape(bsz, n_mem, MEM_HEADS, HEAD_DIM).transpose(0, 2, 1, 3))

    def mem_attend(q_mem, i):
        q_s = jnp.pad(q_mem[tp:].reshape(n_dec, t_dec, MEM_WIDTH), ((0, 0), (0, SAMPLE_Q_PAD - t_dec), (0, 0)))
        o_p = _mem_attn(q_mem, mem_k_p[i], mem_v_p[i], seq)
        o_s = _mem_attn(q_s.reshape(n_dec * SAMPLE_Q_PAD, MEM_WIDTH), cache_mem_k[i].transpose(0, 2, 1, 3),
                        cache_mem_v[i].transpose(0, 2, 1, 3), SAMPLE_Q_PAD)
        o_s = o_s.reshape(n_dec, SAMPLE_Q_PAD, MEM_WIDTH)[:, :t_dec].reshape(ts, MEM_WIDTH)
        return o_p.at[tp:].set(o_s)

    new_conv_p, new_conv_s = [], []
    kv = None
    for i in range(depth):
        hn = _rmsnorm(h, g_mix[i], BF16)
        if i < n_a:
            u = _matmul_glu(hn, w_in_a[i], conv_ch)
            q_mem = _matmul([hn], w_in_a[i], [0], 2 * conv_ch, MEM_WIDTH)
            u_s = u[tp:].reshape(n_dec, t_dec, conv_ch)
            ext_s = jnp.concatenate([state_conv[i], u_s], axis=1)
            keep = min(seq, width - 1)
            tail = jnp.stack([u[(b + 1) * seq - keep:(b + 1) * seq] for b in range(bsz)])
            new_conv_p.append(jnp.pad(tail, ((0, 0), (width - 1 - keep, 0), (0, 0))))
            new_conv_s.append(ext_s[:, t_dec:])
            mix_p = _conv_prompt(u, conv_w[i], conv_b[i], conv_ln_g[i], conv_ln_b[i], bsz, seq)
            mix_s = _conv_sample(ext_s.transpose(1, 0, 2), conv_w[i], conv_b[i], conv_ln_g[i], conv_ln_b[i])
            mix = mix_p.at[tp:].set(mix_s.transpose(1, 0, 2).reshape(ts, conv_ch))
            w_out = w_out_a[i]
        else:
            j = i - n_a
            if kv is None:
                hk = _rmsnorm(h, g_kv, BF16)
                k_p = _matmul([hk], w_kvf, [0], 0, fox_w, n_rows=tp, heads_of=(bsz, seq))
                v_p = _matmul([hk], w_kvf, [0], fox_w, fox_w, n_rows=tp, heads_of=(bsz, seq))
                kv_s = _matmul([hk], w_kvf, [0], 0, 2 * fox_w, row_start=tp, n_rows=ts)
                w_f = jnp.pad(w_kvf[:, 2 * fox_w:], ((0, 0), (0, LANES - n_heads)))
                b_f_pad = jnp.pad(b_f, (0, LANES - n_heads)).reshape(1, LANES)
                logf_pad = _matmul([hk], w_f, [0], 0, LANES, epilogue="logsig", extra=b_f_pad)
                crep, crow = _fox_cumsum(logf_pad[:tp].reshape(bsz, seq, LANES), n_heads)
                def new_rows(x):
                    x = x.reshape(n_dec, t_dec, n_heads, HEAD_DIM).transpose(0, 2, 1, 3)
                    return jnp.pad(x, ((0, 0), (0, 0), (0, Q_ROWS - t_dec), (0, 0)))

                k_new = new_rows(kv_s[:, :fox_w])
                v_new = new_rows(kv_s[:, fox_w:])
                lf_new = jnp.pad(logf_pad[tp:, :n_heads].reshape(n_dec, t_dec, n_heads).transpose(0, 2, 1),
                                 ((0, 0), (0, HEAD_PAD - n_heads), (0, page - t_dec)))
                lf_new_col = jnp.pad(logf_pad[tp:].reshape(n_dec, t_dec, LANES),
                                     ((0, 0), (0, Q_ROWS - t_dec), (0, 0)))
                lf_pool = jnp.pad(cache_fox_logf.astype(F32).transpose(0, 2, 1),
                                  ((0, 0), (0, HEAD_PAD - n_heads), (0, 0)))
                kv = (k_p, v_p, kv_s, logf_pad)
            q = _matmul([hn], w_in_b[j], [0], 0, fox_w)
            q_mem = _matmul([hn], w_in_b[j], [0], fox_w, MEM_WIDTH)
            mix_p = _fox_prompt(q, k_p, v_p, crep, crow)
            q_rows = jnp.pad(q[tp:].reshape(n_dec, t_dec, n_heads, HEAD_DIM).transpose(0, 2, 1, 3),
                             ((0, 0), (0, 0), (0, Q_ROWS - t_dec), (0, 0))).astype(BF16)
            o_rows = _fox_sample(q_rows, k_new, v_new, lf_new, lf_new_col, cache_fox_k.transpose(0, 2, 1, 3),
                                 cache_fox_v.transpose(0, 2, 1, 3), lf_pool, page_table, t_dec)
            mix_s = o_rows.reshape(n_dec, n_heads, Q_ROWS, HEAD_DIM)[:, :, :t_dec].transpose(0, 2, 1, 3)
            mix = mix_p.at[tp:].set(mix_s.reshape(ts, fox_w).astype(BF16))
            w_out = w_out_b[j]
        mo = mem_attend(q_mem, i)
        h = _matmul([mix, mo], w_out, [0, mix.shape[1]], 0, d, epilogue="resid", extra=h)
        jj = i // 2
        if i % 2 == 0:
            h = _ffn(h, g_ffn[i], w_ff_gate[jj], w_ff_up[jj], w_ff_down[jj])
        else:
            hf = _rmsnorm(h, g_ffn[i], BF16)
            h = _moe(hf, h, w_router[jj], b_router[jj], w_e_gate[jj], w_e_up[jj], w_e_down[jj])

    y_p = _rmsnorm(h, g_final, F32, 0, tp)
    y_s = _rmsnorm(h, g_final, F32, tp, ts)
    k_p, v_p, kv_s, logf_pad = kv
    logf = logf_pad[:, :n_heads]
    return (y_p.reshape(bsz, seq, d),
            y_s.reshape(n_dec, t_dec, d),
            k_p.transpose(0, 2, 1, 3),
            v_p.transpose(0, 2, 1, 3),
            logf[:tp].reshape(bsz, seq, n_heads),
            jnp.stack(new_conv_p),
            jnp.stack(mem_k_p).transpose(0, 1, 3, 2, 4),
            jnp.stack(mem_v_p).transpose(0, 1, 3, 2, 4),
            kv_s[:, :fox_w].reshape(n_dec, t_dec, n_heads, HEAD_DIM),
            kv_s[:, fox_w:].reshape(n_dec, t_dec, n_heads, HEAD_DIM),
            logf[tp:].reshape(n_dec, t_dec, n_heads),
            jnp.stack(new_conv_s))
```

```python
import functools
import math

import jax
import jax.numpy as jnp
from jax import lax
from jax.experimental import pallas as pl
from jax.experimental.pallas import tpu as pltpu

HEAD_DIM = 128
MEM_HEADS = 4
MEM_WIDTH = MEM_HEADS * HEAD_DIM
TOP_K = 2
EPS = 1e-6
NEG_INF = -1e30
ATTN_SCALE = HEAD_DIM ** -0.5
LOG2_E = math.log2(math.e)
LANES = 128
SUBLANES = 8
HEAD_PAD = 16
Q_ROWS = 8
LF_PAGE_GROUP = 8
SAMPLE_Q_PAD = 16
CONV_HALO = 32
VMEM_LIMIT_BYTES = 56 * 1024 * 1024
MOE_SUB = 256
MOE_ROW_TILE = 3 * MOE_SUB
MOE_FFN_SUB = 128
MOE_UNSELECTED = -1.0e9

BF16 = jnp.bfloat16
F32 = jnp.float32


def _cparams(*semantics):
    return pltpu.CompilerParams(dimension_semantics=semantics, vmem_limit_bytes=VMEM_LIMIT_BYTES)


def _pick(n, candidates):
    for c in candidates:
        if c <= n and n % c == 0:
            return c
    return n


def _row_tile(t):
    return _pick(t, (1040, 1024, 640, 512, 256, 128, 64, 32, 16))


def _dot(a, b):
    return jnp.dot(a, b, preferred_element_type=F32)


def _dot_nt(a, b):
    return lax.dot_general(a, b, (((1,), (1,)), ((), ())), preferred_element_type=F32)


def _dot_exact(a, b):
    return jnp.dot(a, b, preferred_element_type=F32, precision=lax.Precision.HIGHEST)


def _log_sigmoid(x):
    return jnp.minimum(x, 0.0) - jnp.log1p(jnp.exp(-jnp.abs(x)))


def _rmsnorm_kernel(x_ref, g_ref, o_ref):
    x = x_ref[...]
    y = x * lax.rsqrt(jnp.mean(x * x, axis=-1, keepdims=True) + EPS)
    o_ref[...] = (y * g_ref[...]).astype(o_ref.dtype)


def _rmsnorm(x, g, out_dtype, row_start=0, n_rows=None):
    d = x.shape[1]
    n_rows = x.shape[0] if n_rows is None else n_rows
    tm = _pick(math.gcd(n_rows, row_start), (1024, 640, 512, 256, 128, 64, 32, 16))
    first = row_start // tm
    return pl.pallas_call(
        _rmsnorm_kernel,
        grid=(n_rows // tm,),
        in_specs=[pl.BlockSpec((tm, d), lambda i: (first + i, 0)),
                  pl.BlockSpec((1, d), lambda i: (0, 0))],
        out_specs=pl.BlockSpec((tm, d), lambda i: (i, 0)),
        out_shape=jax.ShapeDtypeStruct((n_rows, d), out_dtype),
        compiler_params=_cparams("parallel"),
        name="rmsnorm",
    )(x, g.reshape(1, d))


def _mm_kernel(*refs, n_a, epilogue):
    a_refs = refs[:n_a]
    w_refs = refs[n_a:2 * n_a]
    pos = 2 * n_a
    extra_ref = None
    if epilogue in ("resid", "logsig"):
        extra_ref = refs[pos]
        pos += 1
    o_ref = refs[pos]
    wb_refs = refs[pos + 1:pos + 1 + n_a]

    @pl.when(pl.program_id(1) == 0)
    def _():
        for w_ref, wb_ref in zip(w_refs, wb_refs):
            wb_ref[...] = w_ref[...].astype(BF16)

    acc = None
    for a_ref, wb_ref in zip(a_refs, wb_refs):
        d = _dot(a_ref[...], wb_ref[...])
        acc = d if acc is None else acc + d
    if epilogue == "resid":
        acc = extra_ref[...] + acc
    elif epilogue == "logsig":
        acc = _log_sigmoid(acc + extra_ref[...])
    if len(o_ref.shape) == 4:
        for j in range(o_ref.shape[1]):
            o_ref[0, j] = acc[:, j * HEAD_DIM:(j + 1) * HEAD_DIM].astype(o_ref.dtype)
    else:
        o_ref[...] = acc.astype(o_ref.dtype)


def _matmul(a_list, w, row_offs, col_off, n_out, *, epilogue="none", extra=None, out_dtype=F32,
            row_start=0, n_rows=None, heads_of=None):
    n_rows = a_list[0].shape[0] if n_rows is None else n_rows
    align = math.gcd(n_rows, row_start) if heads_of is None else math.gcd(heads_of[1], row_start)
    tm = _row_tile(align)
    tn = _pick(n_out, (1024, 768, 512, 256, 128))
    first = row_start // tm
    n_a = len(a_list)
    in_specs, scratch = [], []
    for a in a_list:
        in_specs.append(pl.BlockSpec((tm, a.shape[1]), lambda n, m: (first + m, 0)))
    for a, ro in zip(a_list, row_offs):
        k = a.shape[1]
        assert ro % k == 0 and col_off % tn == 0
        in_specs.append(pl.BlockSpec((k, tn), functools.partial(
            lambda n, m, rb, cb: (rb, cb + n), rb=ro // k, cb=col_off // tn)))
        scratch.append(pltpu.VMEM((k, tn), BF16))
    args = list(a_list) + [w] * n_a
    if epilogue == "resid":
        in_specs.append(pl.BlockSpec((tm, tn), lambda n, m: (first + m, n)))
        args.append(extra)
    elif epilogue == "logsig":
        in_specs.append(pl.BlockSpec((1, tn), lambda n, m: (0, n)))
        args.append(extra)
    if heads_of is None:
        out_spec = pl.BlockSpec((tm, tn), lambda n, m: (m, n))
        out_shape = jax.ShapeDtypeStruct((n_rows, n_out), out_dtype)
    else:
        bsz, seq = heads_of
        per_seq = seq // tm
        out_spec = pl.BlockSpec((1, tn // HEAD_DIM, tm, HEAD_DIM),
                                lambda n, m: (m // per_seq, n, m % per_seq, 0))
        out_shape = jax.ShapeDtypeStruct((bsz, n_out // HEAD_DIM, seq, HEAD_DIM), out_dtype)
    return pl.pallas_call(
        functools.partial(_mm_kernel, n_a=n_a, epilogue=epilogue),
        grid=(n_out // tn, n_rows // tm),
        in_specs=in_specs,
        out_specs=out_spec,
        out_shape=out_shape,
        scratch_shapes=scratch,
        compiler_params=_cparams("arbitrary", "arbitrary"),
        name="matmul_" + epilogue,
    )(*args)


def _glu_kernel(a_ref, w1_ref, w2_ref, o_ref, wb1_ref, wb2_ref):
    @pl.when(pl.program_id(1) == 0)
    def _():
        wb1_ref[...] = w1_ref[...].astype(BF16)
        wb2_ref[...] = w2_ref[...].astype(BF16)

    a = a_ref[...]
    o_ref[...] = _dot(a, wb1_ref[...]) * jax.nn.sigmoid(_dot(a, wb2_ref[...]))


def _matmul_glu(a, w, n_out):
    t, k = a.shape
    tm = _row_tile(t)
    tn = _pick(n_out, (1024, 768, 512, 256, 128))
    nb = n_out // tn
    return pl.pallas_call(
        _glu_kernel,
        grid=(nb, t // tm),
        in_specs=[pl.BlockSpec((tm, k), lambda n, m: (m, 0)),
                  pl.BlockSpec((k, tn), lambda n, m: (0, n)),
                  pl.BlockSpec((k, tn), lambda n, m: (0, n + nb))],
        out_specs=pl.BlockSpec((tm, tn), lambda n, m: (m, n)),
        out_shape=jax.ShapeDtypeStruct((t, n_out), F32),
        scratch_shapes=[pltpu.VMEM((k, tn), BF16), pltpu.VMEM((k, tn), BF16)],
        compiler_params=_cparams("arbitrary", "arbitrary"),
        name="matmul_glu",
    )(a, w, w)


def _swiglu_partial(x, wg, wu, wd):
    g = _dot(x, wg)
    u = _dot(x, wu)
    return _dot((g * jax.nn.sigmoid(g) * u).astype(BF16), wd)


def _ffn_kernel(h_ref, g_ref, wg_ref, wu_ref, wd_ref, o_ref, x_ref):
    @pl.when(pl.program_id(1) == 0)
    def _():
        h = h_ref[...]
        o_ref[...] = h
        y = h * lax.rsqrt(jnp.mean(h * h, axis=-1, keepdims=True) + EPS)
        x_ref[...] = (y * g_ref[...]).astype(BF16)

    o_ref[...] += _swiglu_partial(x_ref[...], wg_ref[...].astype(BF16), wu_ref[...].astype(BF16),
                                  wd_ref[...].astype(BF16))


def _ffn(h, g, w_gate, w_up, w_down):
    t, d = h.shape
    f = w_gate.shape[1]
    tm = _row_tile(t)
    tf = _pick(f, (512, 256, 128))
    return pl.pallas_call(
        _ffn_kernel,
        grid=(t // tm, f // tf),
        in_specs=[pl.BlockSpec((tm, d), lambda m, j: (m, 0), pipeline_mode=pl.Buffered(1)),
                  pl.BlockSpec((1, d), lambda m, j: (0, 0)),
                  pl.BlockSpec((d, tf), lambda m, j: (0, j)),
                  pl.BlockSpec((d, tf), lambda m, j: (0, j)),
                  pl.BlockSpec((tf, d), lambda m, j: (j, 0))],
        out_specs=pl.BlockSpec((tm, d), lambda m, j: (m, 0), pipeline_mode=pl.Buffered(1)),
        out_shape=jax.ShapeDtypeStruct((t, d), F32),
        scratch_shapes=[pltpu.VMEM((tm, d), BF16)],
        compiler_params=_cparams("parallel", "arbitrary"),
        name="ffn",
    )(h, g.reshape(1, d), w_gate, w_up, w_down)


def _router_kernel(x_ref, w_ref, b_ref, o_ref, *, n_experts):
    logits = _dot(x_ref[...], w_ref[...].astype(BF16)) + b_ref[...]
    lane = lax.broadcasted_iota(jnp.int32, logits.shape, 1)
    logits = jnp.where(lane < n_experts, logits, NEG_INF)
    v1 = jnp.max(logits, axis=-1, keepdims=True)
    i1 = jnp.min(jnp.where(logits == v1, lane, LANES), axis=-1, keepdims=True)
    rest = jnp.where(lane == i1, NEG_INF, logits)
    v2 = jnp.max(rest, axis=-1, keepdims=True)
    i2 = jnp.min(jnp.where(rest == v2, lane, LANES), axis=-1, keepdims=True)
    e2 = jnp.exp(v2 - v1)
    p1 = 1.0 / (1.0 + e2)
    p2 = e2 / (1.0 + e2)
    o_ref[...] = jnp.where(lane == i1, p1, jnp.where(lane == i2, p2, 0.0))


def _router(x, w_router, b_router):
    t, d = x.shape
    n_e = w_router.shape[-1]
    tm = _row_tile(t)
    w_pad = jnp.pad(w_router, ((0, 0), (0, LANES - n_e)))
    b_pad = jnp.pad(b_router, (0, LANES - n_e)).reshape(1, LANES)
    return pl.pallas_call(
        functools.partial(_router_kernel, n_experts=n_e),
        grid=(t // tm,),
        in_specs=[pl.BlockSpec((tm, d), lambda m: (m, 0)),
                  pl.BlockSpec((d, LANES), lambda m: (0, 0)),
                  pl.BlockSpec((1, LANES), lambda m: (0, 0))],
        out_specs=pl.BlockSpec((tm, LANES), lambda m: (m, 0)),
        out_shape=jax.ShapeDtypeStruct((t, LANES), F32),
        compiler_params=_cparams("parallel"),
        name="router",
    )(x, w_pad, b_pad)


def _moe_rank_kernel(g_ref, pos_ref, pos_t_ref, start_ref, cnt_ref, carry_ref):
    i = pl.program_id(0)
    tm = g_ref.shape[0]

    @pl.when(i == 0)
    def _():
        carry_ref[...] = jnp.zeros(carry_ref.shape, F32)

    sel = g_ref[...] > 0.0
    r = lax.broadcasted_iota(jnp.int32, (tm, tm), 0)
    c = lax.broadcasted_iota(jnp.int32, (tm, tm), 1)
    before = jnp.where(c < r, 1.0, 0.0).astype(BF16)
    ones = jnp.where(sel, 1.0, 0.0)
    carry = carry_ref[...]
    rank = _dot(before, ones.astype(BF16)) + carry
    start_ref[...] = jnp.broadcast_to(carry, start_ref.shape)
    carry = rank[tm - 1:tm, :] + ones[tm - 1:tm, :]
    carry_ref[...] = carry
    cnt_ref[...] = jnp.broadcast_to(carry, cnt_ref.shape)
    pos = jnp.where(sel, rank, MOE_UNSELECTED)
    pos_ref[...] = pos
    pos_t_ref[...] = pos.T[0:HEAD_PAD, :]


def _moe_rank(gates, tc):
    t = gates.shape[0]
    nch = t // tc
    return pl.pallas_call(
        _moe_rank_kernel,
        grid=(nch,),
        in_specs=[pl.BlockSpec((tc, LANES), lambda i: (i, 0))],
        out_specs=[pl.BlockSpec((tc, LANES), lambda i: (i, 0)),
                   pl.BlockSpec((HEAD_PAD, tc), lambda i: (0, i)),
                   pl.BlockSpec((8, LANES), lambda i: (i, 0)),
                   pl.BlockSpec((8, LANES), lambda i: (0, 0))],
        out_shape=[jax.ShapeDtypeStruct((t, LANES), F32),
                   jax.ShapeDtypeStruct((HEAD_PAD, t), F32),
                   jax.ShapeDtypeStruct((nch * 8, LANES), F32),
                   jax.ShapeDtypeStruct((8, LANES), F32)],
        scratch_shapes=[pltpu.VMEM((1, LANES), F32)],
        compiler_params=_cparams("arbitrary"),
        name="moe_rank",
    )(gates)


def _moe_gather_kernel(sb_ref, ch_ref, first_ref, last_ref, valid_ref, esb_ref, base_ref,
                       pos_t_ref, x_ref, o_ref, acc_ref):
    i = pl.program_id(0)

    @pl.when(valid_ref[i] == 1)
    def _():
        sb = sb_ref[i]
        local = pos_t_ref[pl.ds(esb_ref[sb], 1), :].astype(jnp.int32) + base_ref[sb]
        rows = lax.broadcasted_iota(jnp.int32, (o_ref.shape[0], local.shape[1]), 0)
        onehot = jnp.where(local == rows, 1.0, 0.0).astype(BF16)
        part = _dot(onehot, x_ref[...])

        @pl.when(first_ref[i] == 1)
        def _():
            acc_ref[...] = part

        @pl.when(first_ref[i] == 0)
        def _():
            acc_ref[...] += part

        @pl.when(last_ref[i] == 1)
        def _():
            o_ref[...] = acc_ref[...].astype(o_ref.dtype)


def _moe_gather(plan, pos_t, x, n_rows, tc):
    d = x.shape[1]
    sb = MOE_SUB
    prefetch = (plan["g_sb"], plan["g_ch"], plan["g_first"], plan["g_last"], plan["g_valid"],
                plan["e_sb"], plan["base_sb"])
    grid_spec = pltpu.PrefetchScalarGridSpec(
        num_scalar_prefetch=len(prefetch),
        grid=(plan["g_sb"].shape[0],),
        in_specs=[pl.BlockSpec((HEAD_PAD, tc), lambda i, sbl, chl, *_: (0, chl[i])),
                  pl.BlockSpec((tc, d), lambda i, sbl, chl, *_: (chl[i], 0))],
        out_specs=pl.BlockSpec((sb, d), lambda i, sbl, chl, *_: (sbl[i], 0)),
        scratch_shapes=[pltpu.VMEM((sb, d), F32)],
    )
    return pl.pallas_call(
        _moe_gather_kernel,
        grid_spec=grid_spec,
        out_shape=jax.ShapeDtypeStruct((n_rows, d), BF16),
        compiler_params=_cparams("arbitrary"),
        name="moe_gather",
    )(*prefetch, pos_t, x)


def _moe_ffn_kernel(et_ref, nv_ref, x_ref, wg_ref, wu_ref, wd_ref, o_ref, acc_ref):
    del et_ref
    r = pl.program_id(0)
    f = pl.program_id(1)
    last_f = pl.num_programs(1) - 1
    n_valid = nv_ref[r]
    n_sub = o_ref.shape[0] // MOE_FFN_SUB

    @pl.when(f == 0)
    def _():
        acc_ref[...] = jnp.zeros(acc_ref.shape, F32)

    for n in range(1, n_sub + 1):
        @pl.when(n_valid == n)
        def _():
            rows = slice(0, n * MOE_FFN_SUB)
            acc_ref[rows, :] += _swiglu_partial(x_ref[rows, :], wg_ref[0].astype(BF16), wu_ref[0].astype(BF16),
                                                wd_ref[0].astype(BF16))

    @pl.when(f == last_f)
    def _():
        o_ref[...] = acc_ref[...].astype(o_ref.dtype)


def _moe_ffn(plan, xs, w_gate, w_up, w_down):
    n_rows, d = xs.shape
    f = w_gate.shape[2]
    tr = MOE_ROW_TILE
    tf = _pick(f, (512, 256, 128))
    nf = f // tf

    def w_in_map(r, j, et, nv):
        return (et[r], 0, jnp.where(nv[r] > 0, j, nf - 1))

    def w_out_map(r, j, et, nv):
        return (et[r], jnp.where(nv[r] > 0, j, nf - 1), 0)

    grid_spec = pltpu.PrefetchScalarGridSpec(
        num_scalar_prefetch=2,
        grid=(n_rows // tr, nf),
        in_specs=[pl.BlockSpec((tr, d), lambda r, j, et, nv: (r, 0)),
                  pl.BlockSpec((1, d, tf), w_in_map),
                  pl.BlockSpec((1, d, tf), w_in_map),
                  pl.BlockSpec((1, tf, d), w_out_map)],
        out_specs=pl.BlockSpec((tr, d), lambda r, j, et, nv: (r, 0)),
        scratch_shapes=[pltpu.VMEM((tr, d), F32)],
    )
    return pl.pallas_call(
        _moe_ffn_kernel,
        grid_spec=grid_spec,
        out_shape=jax.ShapeDtypeStruct((n_rows, d), BF16),
        compiler_params=_cparams("arbitrary", "arbitrary"),
        name="moe_ffn",
    )(plan["e_tile"], plan["nv_tile"], xs, w_gate, w_up, w_down)


def _moe_combine_kernel(ch_ref, sb_ref, first_ref, valid_ref, esb_ref, base_ref,
                        pos_ref, gate_ref, h_ref, y_ref, o_ref):
    i = pl.program_id(0)

    @pl.when(first_ref[i] == 1)
    def _():
        o_ref[...] = h_ref[...]

    @pl.when(valid_ref[i] == 1)
    def _():
        sb = sb_ref[i]
        lane = lax.broadcasted_iota(jnp.int32, pos_ref.shape, 1)
        mine = lane == esb_ref[sb]
        rank = jnp.sum(jnp.where(mine, pos_ref[...], 0.0), axis=-1, keepdims=True)
        local = rank.astype(jnp.int32) + base_ref[sb]
        gate = jnp.sum(jnp.where(mine, gate_ref[...], 0.0), axis=-1, keepdims=True)
        cols = lax.broadcasted_iota(jnp.int32, (pos_ref.shape[0], y_ref.shape[0]), 1)
        onehot = jnp.where(local == cols, 1.0, 0.0).astype(BF16)
        o_ref[...] += gate * _dot(onehot, y_ref[...])


def _moe_combine(plan, pos, gates, h, ys, tc):
    t, d = h.shape
    sb = MOE_SUB
    prefetch = (plan["c_ch"], plan["c_sb"], plan["c_first"], plan["c_valid"], plan["e_sb"], plan["base_sb"])
    grid_spec = pltpu.PrefetchScalarGridSpec(
        num_scalar_prefetch=len(prefetch),
        grid=(plan["c_ch"].shape[0],),
        in_specs=[pl.BlockSpec((tc, LANES), lambda i, chl, sbl, *_: (chl[i], 0)),
                  pl.BlockSpec((tc, LANES), lambda i, chl, sbl, *_: (chl[i], 0)),
                  pl.BlockSpec((tc, d), lambda i, chl, sbl, *_: (chl[i], 0)),
                  pl.BlockSpec((sb, d), lambda i, chl, sbl, *_: (sbl[i], 0))],
        out_specs=pl.BlockSpec((tc, d), lambda i, chl, sbl, *_: (chl[i], 0)),
    )
    return pl.pallas_call(
        _moe_combine_kernel,
        grid_spec=grid_spec,
        out_shape=jax.ShapeDtypeStruct((t, d), F32),
        compiler_params=_cparams("arbitrary"),
        name="moe_combine",
    )(*prefetch, pos, gates, h, ys)


def _pair_list(overlap, length):
    n_b = overlap.shape[1]
    flat = overlap.reshape(-1)
    n = jnp.sum(flat.astype(jnp.int32))
    idx = jnp.nonzero(flat, size=length, fill_value=0)[0].astype(jnp.int32)
    step = jnp.arange(length, dtype=jnp.int32)
    valid = step < n
    idx = jnp.where(valid, idx, idx[jnp.maximum(n - 1, 0)])
    a = idx // n_b
    b = idx % n_b
    nxt_valid = jnp.concatenate([valid[1:], jnp.zeros((1,), bool)])
    first = valid & (a != jnp.concatenate([jnp.full((1,), -1, jnp.int32), a[:-1]]))
    last = valid & ((a != jnp.concatenate([a[1:], jnp.full((1,), -1, jnp.int32)])) | ~nxt_valid)
    as_i32 = lambda v: v.astype(jnp.int32)
    return a, b, as_i32(first), as_i32(last), as_i32(valid)


def _moe_plan(counts, chunk_start, n_rows, nch):
    n_e = counts.shape[0]
    tr, sb = MOE_ROW_TILE, MOE_SUB
    seg = (counts + tr - 1) // tr * tr
    seg_end = jnp.cumsum(seg)
    off = seg_end - seg
    chunk_end = jnp.concatenate([chunk_start[:, 1:], counts[:, None]], axis=1)

    def owner(start):
        return jnp.minimum(jnp.sum(seg_end[None, :] <= start[:, None], axis=1), n_e - 1).astype(jnp.int32)

    sb_start = jnp.arange(n_rows // sb, dtype=jnp.int32) * sb
    e_sb = owner(sb_start)
    k0 = sb_start - off[e_sb]
    k1 = jnp.minimum(k0 + sb, counts[e_sb])
    sb_used = (sb_start < seg_end[-1]) & (k0 < counts[e_sb])
    overlap = sb_used[:, None] & (chunk_start[e_sb] < k1[:, None]) & (chunk_end[e_sb] > k0[:, None])
    length = n_rows // sb + n_e * nch
    g_sb, g_ch, g_first, g_last, g_valid = _pair_list(overlap, length)
    c_ch, c_sb, c_first, _, c_valid = _pair_list(overlap.T, length)

    tile_start = jnp.arange(n_rows // tr, dtype=jnp.int32) * tr
    e_tile = owner(tile_start)
    left = counts[e_tile] - (tile_start - off[e_tile])
    step = MOE_FFN_SUB
    nv_tile = jnp.where(tile_start < seg_end[-1], jnp.clip((left + step - 1) // step, 0, tr // step), 0)
    return dict(g_sb=g_sb, g_ch=g_ch, g_first=g_first, g_last=g_last, g_valid=g_valid,
                c_ch=c_ch, c_sb=c_sb, c_first=c_first, c_valid=c_valid,
                e_sb=e_sb, base_sb=(off[e_sb] - sb_start).astype(jnp.int32),
                e_tile=e_tile, nv_tile=nv_tile.astype(jnp.int32))


def _moe(x, h, w_router, b_router, w_gate, w_up, w_down):
    t = x.shape[0]
    n_e = w_router.shape[-1]
    tc = _pick(t, (640, 512, 384, 256, 128))
    assert tc % LANES == 0 and t % tc == 0 and n_e <= 8
    nch = t // tc
    n_rows = -(-(TOP_K * t + n_e * (MOE_ROW_TILE - 1)) // MOE_ROW_TILE) * MOE_ROW_TILE
    gates = _router(x, w_router, b_router)
    pos, pos_t, starts, totals = _moe_rank(gates, tc)
    counts = totals[0, :n_e].astype(jnp.int32)
    chunk_start = starts.reshape(nch, 8, LANES)[:, 0, :n_e].T.astype(jnp.int32)
    plan = _moe_plan(counts, chunk_start, n_rows, nch)
    xs = _moe_gather(plan, pos_t, x, n_rows, tc)
    ys = _moe_ffn(plan, xs, w_gate, w_up, w_down)
    return _moe_combine(plan, pos, gates, h, ys, tc)


def _ln_swish(hc, lg, lb):
    mu = jnp.mean(hc, axis=-1, keepdims=True)
    xc = hc - mu
    var = jnp.mean(xc * xc, axis=-1, keepdims=True)
    y = xc * lax.rsqrt(var + EPS) * lg + lb
    return y * jax.nn.sigmoid(y)


def _conv_kernel(prev_ref, main_ref, w_ref, b_ref, lg_ref, lb_ref, o_ref, ext_ref, acc_ref, sh_ref, *, tt, width):
    @pl.when(pl.program_id(1) == 0)
    def _():
        ext_ref[0:CONV_HALO, :] = jnp.zeros((CONV_HALO, ext_ref.shape[1]), F32)

    @pl.when(pl.program_id(1) > 0)
    def _():
        ext_ref[0:CONV_HALO, :] = prev_ref[...]

    ext_ref[CONV_HALO:CONV_HALO + tt, :] = main_ref[...]
    n_chunks = o_ref.shape[-1] // LANES
    lead = CONV_HALO - (width - 1)

    span = tt + CONV_HALO - SUBLANES

    def chunk(c, carry):
        c0 = pl.multiple_of(c * LANES, LANES)
        for r in range(1, SUBLANES):
            sh_ref[r - 1] = ext_ref[pl.ds(r, span), pl.ds(c0, LANES)]
        acc = jnp.zeros((tt, LANES), F32)
        for j in range(width):
            phase = (lead + j) % SUBLANES
            base = lead + j - phase
            if phase == 0:
                rows = ext_ref[pl.ds(base, tt), pl.ds(c0, LANES)]
            else:
                rows = sh_ref[phase - 1, pl.ds(base, tt), :]
            acc = acc + rows * w_ref[pl.ds(j, 1), pl.ds(c0, LANES)]
        acc_ref[:, pl.ds(c0, LANES)] = acc + b_ref[:, pl.ds(c0, LANES)]
        return carry

    lax.fori_loop(0, n_chunks, chunk, 0)
    o_ref[...] = _ln_swish(acc_ref[...], lg_ref[...], lb_ref[...]).astype(o_ref.dtype)


def _conv_prompt(u, w, b, lg, lb, bsz, seq):
    c = u.shape[1]
    width = w.shape[0]
    tt = _pick(seq, (256, 128, 64, 32))
    nt = seq // tt
    per_halo = tt // CONV_HALO
    return pl.pallas_call(
        functools.partial(_conv_kernel, tt=tt, width=width),
        grid=(bsz, nt),
        in_specs=[pl.BlockSpec((CONV_HALO, c), lambda bb, i: (jnp.maximum((bb * nt + i) * per_halo - 1, 0), 0)),
                  pl.BlockSpec((tt, c), lambda bb, i: (bb * nt + i, 0)),
                  pl.BlockSpec((width, c), lambda bb, i: (0, 0)),
                  pl.BlockSpec((1, c), lambda bb, i: (0, 0)),
                  pl.BlockSpec((1, c), lambda bb, i: (0, 0)),
                  pl.BlockSpec((1, c), lambda bb, i: (0, 0))],
        out_specs=pl.BlockSpec((tt, c), lambda bb, i: (bb * nt + i, 0)),
        out_shape=jax.ShapeDtypeStruct((u.shape[0], c), BF16),
        scratch_shapes=[pltpu.VMEM((tt + CONV_HALO, c), F32), pltpu.VMEM((tt, c), F32),
                        pltpu.VMEM((SUBLANES - 1, tt + CONV_HALO - SUBLANES, LANES), F32)],
        compiler_params=_cparams("parallel", "parallel"),
        name="conv_prompt",
    )(u, u, w, b.reshape(1, c), lg.reshape(1, c), lb.reshape(1, c))


def _conv_step_kernel(ext_ref, w_ref, b_ref, lg_ref, lb_ref, o_ref, *, width):
    n_t = o_ref.shape[0]
    for t in range(n_t):
        acc = ext_ref[t] * w_ref[0:1, :]
        for j in range(1, width):
            acc = acc + ext_ref[t + j] * w_ref[j:j + 1, :]
        o_ref[t] = _ln_swish(acc + b_ref[...], lg_ref[...], lb_ref[...]).astype(o_ref.dtype)


def _conv_sample(ext_t, w, b, lg, lb):
    l_ext, n, c = ext_t.shape
    width = w.shape[0]
    n_t = l_ext - (width - 1)
    return pl.pallas_call(
        functools.partial(_conv_step_kernel, width=width),
        out_shape=jax.ShapeDtypeStruct((n_t, n, c), BF16),
        compiler_params=pltpu.CompilerParams(vmem_limit_bytes=VMEM_LIMIT_BYTES),
        name="conv_sample",
    )(ext_t, w, b.reshape(1, c), lg.reshape(1, c), lb.reshape(1, c))


def _softmax_attend(q, k, v):
    s = _dot_nt(q, k) * ATTN_SCALE
    p = jnp.exp(s - jnp.max(s, axis=-1, keepdims=True))
    l = jnp.sum(p, axis=-1, keepdims=True)
    return _dot(p.astype(BF16), v) / l


def _mem_attn_kernel(q_ref, k_ref, v_ref, o_ref):
    for hh in range(MEM_HEADS):
        sl = slice(hh * HEAD_DIM, (hh + 1) * HEAD_DIM)
        o_ref[:, sl] = _softmax_attend(q_ref[:, sl].astype(BF16), k_ref[0, hh].astype(BF16),
                                       v_ref[0, hh].astype(BF16)).astype(o_ref.dtype)


def _mem_attn_cache_kernel(q_ref, k_ref, v_ref, o_ref, *, rows_per_req):
    n_mem = k_ref.shape[2] // MEM_HEADS
    for r in range(k_ref.shape[1]):
        rows = slice(r * rows_per_req, (r + 1) * rows_per_req)
        for hh in range(MEM_HEADS):
            sl = slice(hh * HEAD_DIM, (hh + 1) * HEAD_DIM)
            head_rows = pl.ds(hh, n_mem, stride=MEM_HEADS)
            o_ref[rows, sl] = _softmax_attend(q_ref[rows, sl].astype(BF16),
                                              k_ref[0, r, head_rows, :].astype(BF16),
                                              v_ref[0, r, head_rows, :].astype(BF16)).astype(o_ref.dtype)


def _mem_attn_cache(q, k_cache, v_cache, layer, rows_per_req):
    n_layers, n, m, _, _ = k_cache.shape
    w = q.shape[1]
    group = _pick(n, (4, 2, 1))
    flat = (n_layers, n, m * MEM_HEADS, HEAD_DIM)
    kv_spec = pl.BlockSpec((1, group, m * MEM_HEADS, HEAD_DIM), lambda j: (layer, j, 0, 0))
    return pl.pallas_call(
        functools.partial(_mem_attn_cache_kernel, rows_per_req=rows_per_req),
        grid=(n // group,),
        in_specs=[pl.BlockSpec((group * rows_per_req, w), lambda j: (j, 0)), kv_spec, kv_spec],
        out_specs=pl.BlockSpec((group * rows_per_req, w), lambda j: (j, 0)),
        out_shape=jax.ShapeDtypeStruct((n * rows_per_req, w), BF16),
        compiler_params=_cparams("parallel"),
        name="mem_attn_cache",
    )(q, k_cache.reshape(flat), v_cache.reshape(flat))


def _mem_attn(q, k, v, rows_per_req):
    w = q.shape[1]
    n, _, m, _ = k.shape
    tq = _pick(rows_per_req, (512, 256, 128, 64, 32, 16))
    per_req = rows_per_req // tq
    return pl.pallas_call(
        _mem_attn_kernel,
        grid=(n, per_req),
        in_specs=[pl.BlockSpec((tq, w), lambda i, j: (i * per_req + j, 0)),
                  pl.BlockSpec((1, MEM_HEADS, m, HEAD_DIM), lambda i, j: (i, 0, 0, 0)),
                  pl.BlockSpec((1, MEM_HEADS, m, HEAD_DIM), lambda i, j: (i, 0, 0, 0))],
        out_specs=pl.BlockSpec((tq, w), lambda i, j: (i * per_req + j, 0)),
        out_shape=jax.ShapeDtypeStruct((max(q.shape[0], n * rows_per_req), w), BF16),
        compiler_params=_cparams("parallel", "parallel"),
        name="mem_attn",
    )(q, k, v)


def _cumsum_kernel(lf_ref, crep_ref, crow_ref, *, blk, n_heads):
    seq = lf_ref.shape[1]
    r = lax.broadcasted_iota(jnp.int32, (blk, blk), 0)
    c = lax.broadcasted_iota(jnp.int32, (blk, blk), 1)
    tri = (c <= r).astype(F32)
    carry = jnp.zeros((1, LANES), F32)
    for i in range(seq // blk):
        rows = slice(i * blk, (i + 1) * blk)
        cs = _dot_exact(tri, lf_ref[0, rows, :]) + carry
        carry = cs[blk - 1:blk, :]
        cs2 = cs * LOG2_E
        crow_ref[0, :, rows] = cs2.T[0:HEAD_PAD, :]
        for hh in range(n_heads):
            crep_ref[0, hh, rows, :] = jnp.broadcast_to(cs2[:, hh:hh + 1], (blk, LANES))


def _fox_cumsum(logf_pad, n_heads):
    bsz, seq, _ = logf_pad.shape
    blk = _pick(seq, (256, 128))
    return pl.pallas_call(
        functools.partial(_cumsum_kernel, blk=blk, n_heads=n_heads),
        grid=(bsz,),
        in_specs=[pl.BlockSpec((1, seq, LANES), lambda b: (b, 0, 0))],
        out_specs=[pl.BlockSpec((1, n_heads, seq, LANES), lambda b: (b, 0, 0, 0)),
                   pl.BlockSpec((1, HEAD_PAD, seq), lambda b: (b, 0, 0))],
        out_shape=[jax.ShapeDtypeStruct((bsz, n_heads, seq, LANES), F32),
                   jax.ShapeDtypeStruct((bsz, HEAD_PAD, seq), F32)],
        compiler_params=_cparams("parallel"),
        name="fox_cumsum",
    )(logf_pad)


def _fox_prompt_kernel(q_ref, k_ref, v_ref, crep_ref, crow_ref, o_ref,
                       kb_ref, vt_ref, m_ref, l_ref, acc_ref, *, tq, tk):
    hh = pl.program_id(1)
    qi = pl.program_id(2)

    @pl.when(qi == 0)
    def _():
        kb_ref[...] = k_ref[0, 0].astype(BF16)
        vt_ref[...] = v_ref[0, 0].T.astype(BF16)

    q = q_ref[...].astype(BF16)
    q0 = pl.multiple_of(qi * tq, tq)
    cq = crow_ref[0, pl.ds(hh, 1), pl.ds(q0, tq)]
    m_ref[...] = jnp.full(m_ref.shape, NEG_INF, F32)
    l_ref[...] = jnp.zeros(l_ref.shape, F32)
    acc_ref[...] = jnp.zeros(acc_ref.shape, F32)

    def update(k0, on_diagonal):
        ck = jnp.tile(crep_ref[0, 0, pl.ds(k0, tk), :], (1, tq // LANES))
        z = _dot_nt(kb_ref[pl.ds(k0, tk), :], q) * (ATTN_SCALE * LOG2_E) - ck
        if on_diagonal:
            z = jnp.where(lax.broadcasted_iota(jnp.int32, (tk, tq), 0) <= lax.broadcasted_iota(jnp.int32, (tk, tq), 1),
                          z, NEG_INF)
        m_old = m_ref[...]
        m_new = jnp.maximum(m_old, jnp.max(z, axis=0, keepdims=True) + cq)
        alpha = jnp.exp2(m_old - m_new)
        p = jnp.exp2(z - (m_new - cq))
        l_ref[...] = alpha * l_ref[...] + jnp.sum(p, axis=0, keepdims=True)
        acc_ref[...] = alpha * acc_ref[...] + _dot(vt_ref[:, pl.ds(k0, tk)], p.astype(BF16))
        m_ref[...] = m_new

    def body(kj, carry):
        update(pl.multiple_of(kj * tk, tk), False)
        return carry

    lax.fori_loop(0, qi, body, 0)
    update(q0, True)
    o_ref[...] = (acc_ref[...] / l_ref[...]).T.astype(o_ref.dtype)


def _fox_prompt(q, k, v, crep, crow):
    bsz, n_heads, seq, _ = k.shape
    tq = _pick(seq, (512, 256, 128))
    tk = tq
    nq = seq // tq
    head_seq = pl.BlockSpec((1, 1, seq, HEAD_DIM), lambda b, h, i: (b, h, 0, 0))
    return pl.pallas_call(
        functools.partial(_fox_prompt_kernel, tq=tq, tk=tk),
        grid=(bsz, n_heads, nq),
        in_specs=[pl.BlockSpec((tq, HEAD_DIM), lambda b, h, i: (b * nq + i, h)),
                  head_seq,
                  head_seq,
                  pl.BlockSpec((1, 1, seq, LANES), lambda b, h, i: (b, h, 0, 0)),
                  pl.BlockSpec((1, HEAD_PAD, seq), lambda b, h, i: (b, 0, 0))],
        out_specs=pl.BlockSpec((tq, HEAD_DIM), lambda b, h, i: (b * nq + i, h)),
        out_shape=jax.ShapeDtypeStruct((q.shape[0], n_heads * HEAD_DIM), BF16),
        scratch_shapes=[pltpu.VMEM((seq, HEAD_DIM), BF16), pltpu.VMEM((HEAD_DIM, seq), BF16),
                        pltpu.VMEM((1, tq), F32), pltpu.VMEM((1, tq), F32),
                        pltpu.VMEM((HEAD_DIM, tq), F32)],
        compiler_params=_cparams("parallel", "parallel", "arbitrary"),
        name="fox_prompt",
    )(q, k, v, crep, crow)


def _fox_sample_kernel(*refs, pages_per_step, n_new, n_heads):
    pp = pages_per_step
    pt_ref = refs[0]
    q_ref, knew_ref, vnew_ref, lfnew_ref, lfcol_ref = refs[1:6]
    k_refs = refs[6:6 + pp]
    v_refs = refs[6 + pp:6 + 2 * pp]
    lf_refs = refs[6 + 2 * pp:6 + 3 * pp]
    o_ref = refs[6 + 3 * pp]
    m_ref, l_ref, acc_ref, carry_ref, cn_ref, knp_ref, vnp_ref = refs[7 + 3 * pp:]
    step = pl.program_id(1)
    page = k_refs[0].shape[2]
    n_rows = n_heads * Q_ROWS

    r = lax.broadcasted_iota(jnp.int32, (page, page), 0)
    c = lax.broadcasted_iota(jnp.int32, (page, page), 1)

    @pl.when(step == 0)
    def _():
        m_ref[...] = jnp.full(m_ref.shape, NEG_INF, F32)
        l_ref[...] = jnp.zeros(l_ref.shape, F32)
        acc_ref[...] = jnp.zeros(acc_ref.shape, F32)
        carry_ref[...] = jnp.zeros(carry_ref.shape, F32)
        lf_col = lfcol_ref[0]
        tok = lax.broadcasted_iota(jnp.int32, lf_col.shape, 0)
        c_col = jnp.zeros(lf_col.shape, F32)
        for j in range(n_new):
            c_col = c_col + jnp.where(tok >= j, lf_col[j:j + 1, :], 0.0)
        for hh in range(n_heads):
            cn_ref[hh * Q_ROWS:(hh + 1) * Q_ROWS, :] = c_col[:, hh:hh + 1]

    def per_head_rows(x16):
        return jnp.concatenate([jnp.broadcast_to(x16[hh:hh + 1, :], (Q_ROWS, page)) for hh in range(n_heads)],
                               axis=0)

    def attend(k_list, v_list, bias, mask=None):
        s = jnp.concatenate(
            [jnp.concatenate([_dot_nt(q_ref[0, hh], k_ref[0, hh].astype(BF16)) for hh in range(n_heads)], axis=0)
             for k_ref in k_list], axis=1)
        s = s * ATTN_SCALE + bias
        if mask is not None:
            s = jnp.where(mask, s, NEG_INF)
        m_old = m_ref[...]
        m_new = jnp.maximum(m_old, jnp.max(s, axis=-1, keepdims=True))
        alpha = jnp.exp(m_old - m_new)
        p = jnp.exp(s - m_new)
        l_ref[...] = alpha * l_ref[...] + jnp.sum(p, axis=-1, keepdims=True)
        pv = None
        for j, v_ref in enumerate(v_list):
            part = jnp.concatenate(
                [_dot(p[hh * Q_ROWS:(hh + 1) * Q_ROWS, j * page:(j + 1) * page].astype(BF16),
                      v_ref[0, hh].astype(BF16)) for hh in range(n_heads)], axis=0)
            pv = part if pv is None else pv + part
        acc_ref[...] = alpha * acc_ref[...] + pv
        m_ref[...] = m_new

    suffix = (r > c).astype(F32)
    carry = carry_ref[...]
    decay = []
    n_pages = pl.num_programs(1) * pp
    for j in range(pp):
        page_id = pt_ref[pl.program_id(0), n_pages - 1 - (step * pp + j)]
        lf = lf_refs[j][:, pl.ds(page_id % LF_PAGE_GROUP, 1), :].reshape(n_heads, page)
        decay.append(per_head_rows(_dot_exact(lf, suffix) + carry))
        carry = carry + jnp.sum(lf, axis=-1, keepdims=True)
    carry_ref[...] = carry
    attend(k_refs, v_refs, cn_ref[...] + jnp.concatenate(decay, axis=1))

    @pl.when(step == pl.num_programs(1) - 1)
    def _():
        c_row = _dot_exact(lfnew_ref[0], (r <= c).astype(F32))
        key = lax.broadcasted_iota(jnp.int32, (n_rows, page), 1)
        tok = lax.broadcasted_iota(jnp.int32, (n_rows, page), 0) % Q_ROWS
        knp_ref[...] = jnp.zeros(knp_ref.shape, F32)
        vnp_ref[...] = jnp.zeros(vnp_ref.shape, F32)
        for hh in range(n_heads):
            knp_ref[0, hh, 0:Q_ROWS, :] = knew_ref[0, hh]
            vnp_ref[0, hh, 0:Q_ROWS, :] = vnew_ref[0, hh]
        attend([knp_ref], [vnp_ref], cn_ref[...] - per_head_rows(c_row), mask=key <= tok)
        o_ref[0] = acc_ref[...] / l_ref[...]


def _fox_sample(q_rows, k_new, v_new, lf_new, lf_new_col, k_pool, v_pool, lf_pool, page_table, n_new):
    n, n_heads = q_rows.shape[:2]
    page = k_pool.shape[2]
    n_pages = page_table.shape[1]
    pp = _pick(n_pages, (8, 4, 2, 1))
    n_steps = n_pages // pp
    n_rows = n_heads * Q_ROWS

    def kv_map(j):
        return lambda i, s, pt: (pt[i, n_pages - 1 - (s * pp + j)], 0, 0, 0)

    def lf_map(j):
        return lambda i, s, pt: (0, pt[i, n_pages - 1 - (s * pp + j)] // LF_PAGE_GROUP, 0)

    req3 = lambda i, s, pt: (i, 0, 0)
    req4 = lambda i, s, pt: (i, 0, 0, 0)
    kv_block = (1, n_heads, page, HEAD_DIM)
    new_block = pl.BlockSpec((1, n_heads, Q_ROWS, HEAD_DIM), req4)
    in_specs = [new_block, new_block, new_block,
                pl.BlockSpec((1, HEAD_PAD, page), req3),
                pl.BlockSpec((1, Q_ROWS, LANES), req3)]
    in_specs += [pl.BlockSpec(kv_block, kv_map(j)) for j in range(pp)]
    in_specs += [pl.BlockSpec(kv_block, kv_map(j)) for j in range(pp)]
    in_specs += [pl.BlockSpec((n_heads, LF_PAGE_GROUP, page), lf_map(j)) for j in range(pp)]
    grid_spec = pltpu.PrefetchScalarGridSpec(
        num_scalar_prefetch=1,
        grid=(n, n_steps),
        in_specs=in_specs,
        out_specs=pl.BlockSpec((1, n_rows, HEAD_DIM), req3),
        scratch_shapes=[pltpu.VMEM((n_rows, 1), F32), pltpu.VMEM((n_rows, 1), F32),
                        pltpu.VMEM((n_rows, HEAD_DIM), F32), pltpu.VMEM((n_heads, 1), F32),
                        pltpu.VMEM((n_rows, 1), F32), pltpu.VMEM(kv_block, F32), pltpu.VMEM(kv_block, F32)],
    )
    return pl.pallas_call(
        functools.partial(_fox_sample_kernel, pages_per_step=pp, n_new=n_new, n_heads=n_heads),
        grid_spec=grid_spec,
        out_shape=jax.ShapeDtypeStruct((n, n_rows, HEAD_DIM), F32),
        compiler_params=_cparams("parallel", "arbitrary"),
        name="fox_sample",
    )(page_table, q_rows, k_new, v_new, lf_new, lf_new_col,
      *([k_pool] * pp), *([v_pool] * pp), *([lf_pool] * pp))


def kernel(x_prompt, x_sample, cache_fox_k, cache_fox_v, cache_fox_logf, state_conv, cache_mem_k, cache_mem_v,
           page_table, mem_prompt, g_mix, g_ffn, g_mem, w_mem_kv, w_in_a, w_out_a, conv_w, conv_b, conv_ln_g,
           conv_ln_b, g_kv, w_kvf, b_f, w_in_b, w_out_b, w_ff_gate, w_ff_up, w_ff_down, w_router, b_router,
           w_e_gate, w_e_up, w_e_down, g_final):
    bsz, seq, d = x_prompt.shape
    n_dec, t_dec, _ = x_sample.shape
    depth = g_mix.shape[0]
    n_a = w_in_a.shape[0]
    conv_ch = conv_w.shape[-1]
    width = conv_w.shape[1]
    n_mem = mem_prompt.shape[1]
    fox_w = w_in_b.shape[-1] - MEM_WIDTH
    n_heads = fox_w // HEAD_DIM
    n_pool, page = cache_fox_k.shape[:2]
    tp = bsz * seq
    ts = n_dec * t_dec
    assert n_heads <= HEAD_PAD and width - 1 <= CONV_HALO and t_dec <= 8

    h = jnp.concatenate([x_prompt.reshape(tp, d), x_sample.reshape(ts, d)], axis=0)

    mem_flat = mem_prompt.reshape(bsz * n_mem, d)
    mem_k_p, mem_v_p = [], []
    for i in range(depth):
        z = _matmul([_rmsnorm(mem_flat, g_mem[i], BF16)], w_mem_kv[i], [0], 0, 2 * MEM_WIDTH)
        mem_k_p.append(z[:, :MEM_WIDTH].reshape(bsz, n_mem, MEM_HEADS, HEAD_DIM).transpose(0, 2, 1, 3))
        mem_v_p.append(z[:, MEM_WIDTH:].reshape(bsz, n_mem, MEM_HEADS, HEAD_DIM).transpose(0, 2, 1, 3))

    def mem_attend(q_mem, i):
        q_s = jnp.pad(q_mem[tp:].reshape(n_dec, t_dec, MEM_WIDTH), ((0, 0), (0, SAMPLE_Q_PAD - t_dec), (0, 0)))
        o_p = _mem_attn(q_mem, mem_k_p[i], mem_v_p[i], seq)
        o_s = _mem_attn_cache(q_s.reshape(n_dec * SAMPLE_Q_PAD, MEM_WIDTH), cache_mem_k, cache_mem_v, i,
                              SAMPLE_Q_PAD)
        o_s = o_s.reshape(n_dec, SAMPLE_Q_PAD, MEM_WIDTH)[:, :t_dec].reshape(ts, MEM_WIDTH)
        return o_p.at[tp:].set(o_s)

    new_conv_p, new_conv_s = [], []
    kv = None
    for i in range(depth):
        hn = _rmsnorm(h, g_mix[i], BF16)
        if i < n_a:
            u = _matmul_glu(hn, w_in_a[i], conv_ch)
            q_mem = _matmul([hn], w_in_a[i], [0], 2 * conv_ch, MEM_WIDTH)
            u_s = u[tp:].reshape(n_dec, t_dec, conv_ch)
            ext_s = jnp.concatenate([state_conv[i], u_s], axis=1)
            keep = min(seq, width - 1)
            tail = jnp.stack([u[(b + 1) * seq - keep:(b + 1) * seq] for b in range(bsz)])
            new_conv_p.append(jnp.pad(tail, ((0, 0), (width - 1 - keep, 0), (0, 0))))
            new_conv_s.append(ext_s[:, t_dec:])
            mix_p = _conv_prompt(u, conv_w[i], conv_b[i], conv_ln_g[i], conv_ln_b[i], bsz, seq)
            mix_s = _conv_sample(ext_s.transpose(1, 0, 2), conv_w[i], conv_b[i], conv_ln_g[i], conv_ln_b[i])
            mix = mix_p.at[tp:].set(mix_s.transpose(1, 0, 2).reshape(ts, conv_ch))
            w_out = w_out_a[i]
        else:
            j = i - n_a
            if kv is None:
                hk = _rmsnorm(h, g_kv, BF16)
                k_p = _matmul([hk], w_kvf, [0], 0, fox_w, n_rows=tp, heads_of=(bsz, seq))
                v_p = _matmul([hk], w_kvf, [0], fox_w, fox_w, n_rows=tp, heads_of=(bsz, seq))
                kv_s = _matmul([hk], w_kvf, [0], 0, 2 * fox_w, row_start=tp, n_rows=ts)
                w_f = jnp.pad(w_kvf[:, 2 * fox_w:], ((0, 0), (0, LANES - n_heads)))
                b_f_pad = jnp.pad(b_f, (0, LANES - n_heads)).reshape(1, LANES)
                logf_pad = _matmul([hk], w_f, [0], 0, LANES, epilogue="logsig", extra=b_f_pad)
                crep, crow = _fox_cumsum(logf_pad[:tp].reshape(bsz, seq, LANES), n_heads)
                def new_rows(x):
                    x = x.reshape(n_dec, t_dec, n_heads, HEAD_DIM).transpose(0, 2, 1, 3)
                    return jnp.pad(x, ((0, 0), (0, 0), (0, Q_ROWS - t_dec), (0, 0)))

                k_new = new_rows(kv_s[:, :fox_w])
                v_new = new_rows(kv_s[:, fox_w:])
                lf_new = jnp.pad(logf_pad[tp:, :n_heads].reshape(n_dec, t_dec, n_heads).transpose(0, 2, 1),
                                 ((0, 0), (0, HEAD_PAD - n_heads), (0, page - t_dec)))
                lf_new_col = jnp.pad(logf_pad[tp:].reshape(n_dec, t_dec, LANES),
                                     ((0, 0), (0, Q_ROWS - t_dec), (0, 0)))
                lf_pool = cache_fox_logf.astype(F32).transpose(2, 0, 1)
                kv = (k_p, v_p, kv_s, logf_pad)
            q = _matmul([hn], w_in_b[j], [0], 0, fox_w)
            q_mem = _matmul([hn], w_in_b[j], [0], fox_w, MEM_WIDTH)
            mix_p = _fox_prompt(q, k_p, v_p, crep, crow)
            q_rows = jnp.pad(q[tp:].reshape(n_dec, t_dec, n_heads, HEAD_DIM).transpose(0, 2, 1, 3),
                             ((0, 0), (0, 0), (0, Q_ROWS - t_dec), (0, 0))).astype(BF16)
            o_rows = _fox_sample(q_rows, k_new, v_new, lf_new, lf_new_col, cache_fox_k.transpose(0, 2, 1, 3),
                                 cache_fox_v.transpose(0, 2, 1, 3), lf_pool, page_table, t_dec)
            mix_s = o_rows.reshape(n_dec, n_heads, Q_ROWS, HEAD_DIM)[:, :, :t_dec].transpose(0, 2, 1, 3)
            mix = mix_p.at[tp:].set(mix_s.reshape(ts, fox_w).astype(BF16))
            w_out = w_out_b[j]
        mo = mem_attend(q_mem, i)
        h = _matmul([mix, mo], w_out, [0, mix.shape[1]], 0, d, epilogue="resid", extra=h)
        jj = i // 2
        if i % 2 == 0:
            h = _ffn(h, g_ffn[i], w_ff_gate[jj], w_ff_up[jj], w_ff_down[jj])
        else:
            hf = _rmsnorm(h, g_ffn[i], BF16)
            h = _moe(hf, h, w_router[jj], b_router[jj], w_e_gate[jj], w_e_up[jj], w_e_down[jj])

    y_p = _rmsnorm(h, g_final, F32, 0, tp)
    y_s = _rmsnorm(h, g_final, F32, tp, ts)
    k_p, v_p, kv_s, logf_pad = kv
    logf = logf_pad[:, :n_heads]
    return (y_p.reshape(bsz, seq, d),
            y_s.reshape(n_dec, t_dec, d),
            k_p.transpose(0, 2, 1, 3),
            v_p.transpose(0, 2, 1, 3),
            logf[:tp].reshape(bsz, seq, n_heads),
            jnp.stack(new_conv_p),
            jnp.stack(mem_k_p).transpose(0, 1, 3, 2, 4),
            jnp.stack(mem_v_p).transpose(0, 1, 3, 2, 4),
            kv_s[:, :fox_w].reshape(n_dec, t_dec, n_heads, HEAD_DIM),
            kv_s[:, fox_w:].reshape(n_dec, t_dec, n_heads, HEAD_DIM),
            logf[tp:].reshape(n_dec, t_dec, n_heads),
            jnp.stack(new_conv_s))
```

```python
import functools
import math

import jax
import jax.numpy as jnp
from jax import lax
from jax.experimental import pallas as pl
from jax.experimental.pallas import tpu as pltpu

HEAD_DIM = 128
MEM_HEADS = 4
MEM_WIDTH = MEM_HEADS * HEAD_DIM
TOP_K = 2
EPS = 1e-6
NEG_INF = -1e30
ATTN_SCALE = HEAD_DIM ** -0.5
LOG2_E = math.log2(math.e)
LANES = 128
SUBLANES = 8
HEAD_PAD = 16
Q_ROWS = 8
LF_PAGE_GROUP = 8
SAMPLE_Q_PAD = 16
CONV_HALO = 32
VMEM_LIMIT_BYTES = 56 * 1024 * 1024
MOE_SUB = 256
MOE_ROW_TILE = 3 * MOE_SUB
MOE_FFN_SUB = 128
MOE_UNSELECTED = -1.0e9

BF16 = jnp.bfloat16
F32 = jnp.float32


def _cparams(*semantics):
    return pltpu.CompilerParams(dimension_semantics=semantics, vmem_limit_bytes=VMEM_LIMIT_BYTES)


def _pick(n, candidates):
    for c in candidates:
        if c <= n and n % c == 0:
            return c
    return n


def _row_tile(t):
    return _pick(t, (1040, 1024, 640, 512, 256, 128, 64, 32, 16))


def _dot(a, b):
    return jnp.dot(a, b, preferred_element_type=F32)


def _dot_nt(a, b):
    return lax.dot_general(a, b, (((1,), (1,)), ((), ())), preferred_element_type=F32)


def _dot_exact(a, b):
    return jnp.dot(a, b, preferred_element_type=F32, precision=lax.Precision.HIGHEST)


def _log_sigmoid(x):
    return jnp.minimum(x, 0.0) - jnp.log1p(jnp.exp(-jnp.abs(x)))


def _rmsnorm_kernel(x_ref, g_ref, o_ref):
    x = x_ref[...]
    y = x * lax.rsqrt(jnp.mean(x * x, axis=-1, keepdims=True) + EPS)
    o_ref[...] = (y * g_ref[...]).astype(o_ref.dtype)


def _rmsnorm(x, g, out_dtype, row_start=0, n_rows=None):
    d = x.shape[1]
    n_rows = x.shape[0] if n_rows is None else n_rows
    tm = _pick(math.gcd(n_rows, row_start), (1024, 640, 512, 256, 128, 64, 32, 16))
    first = row_start // tm
    return pl.pallas_call(
        _rmsnorm_kernel,
        grid=(n_rows // tm,),
        in_specs=[pl.BlockSpec((tm, d), lambda i: (first + i, 0)),
                  pl.BlockSpec((1, d), lambda i: (0, 0))],
        out_specs=pl.BlockSpec((tm, d), lambda i: (i, 0)),
        out_shape=jax.ShapeDtypeStruct((n_rows, d), out_dtype),
        compiler_params=_cparams("parallel"),
        name="rmsnorm",
    )(x, g.reshape(1, d))


def _rmsnorm_pair_kernel(x_ref, g1_ref, g2_ref, o1_ref, o2_ref):
    x = x_ref[...]
    y = x * lax.rsqrt(jnp.mean(x * x, axis=-1, keepdims=True) + EPS)
    o1_ref[...] = (y * g1_ref[...]).astype(o1_ref.dtype)
    o2_ref[...] = (y * g2_ref[...]).astype(o2_ref.dtype)


def _rmsnorm_pair(x, g1, g2):
    t, d = x.shape
    tm = _pick(t, (640, 512, 256, 128, 64, 32, 16))
    rows = pl.BlockSpec((tm, d), lambda i: (i, 0))
    gain = pl.BlockSpec((1, d), lambda i: (0, 0))
    return pl.pallas_call(
        _rmsnorm_pair_kernel,
        grid=(t // tm,),
        in_specs=[rows, gain, gain],
        out_specs=[rows, rows],
        out_shape=[jax.ShapeDtypeStruct((t, d), BF16)] * 2,
        compiler_params=_cparams("parallel"),
        name="rmsnorm_pair",
    )(x, g1.reshape(1, d), g2.reshape(1, d))


def _mm_kernel(*refs, n_a, epilogue):
    a_refs = refs[:n_a]
    w_refs = refs[n_a:2 * n_a]
    pos = 2 * n_a
    extra_ref = None
    if epilogue in ("resid", "logsig"):
        extra_ref = refs[pos]
        pos += 1
    o_ref = refs[pos]
    wb_refs = refs[pos + 1:pos + 1 + n_a]

    @pl.when(pl.program_id(1) == 0)
    def _():
        for w_ref, wb_ref in zip(w_refs, wb_refs):
            wb_ref[...] = w_ref[...].astype(BF16)

    acc = None
    for a_ref, wb_ref in zip(a_refs, wb_refs):
        d = _dot(a_ref[...], wb_ref[...])
        acc = d if acc is None else acc + d
    if epilogue == "resid":
        acc = extra_ref[...] + acc
    elif epilogue == "logsig":
        acc = _log_sigmoid(acc + extra_ref[...])
    if len(o_ref.shape) == 4:
        for j in range(o_ref.shape[1]):
            o_ref[0, j] = acc[:, j * HEAD_DIM:(j + 1) * HEAD_DIM].astype(o_ref.dtype)
    else:
        o_ref[...] = acc.astype(o_ref.dtype)


def _matmul(a_list, w, row_offs, col_off, n_out, *, epilogue="none", extra=None, out_dtype=F32,
            row_start=0, n_rows=None, heads_of=None):
    n_rows = a_list[0].shape[0] if n_rows is None else n_rows
    align = math.gcd(n_rows, row_start) if heads_of is None else math.gcd(heads_of[1], row_start)
    tm = _row_tile(align)
    tn = _pick(n_out, (1024, 768, 512, 256, 128))
    first = row_start // tm
    n_a = len(a_list)
    in_specs, scratch = [], []
    for a in a_list:
        in_specs.append(pl.BlockSpec((tm, a.shape[1]), lambda n, m: (first + m, 0)))
    for a, ro in zip(a_list, row_offs):
        k = a.shape[1]
        assert ro % k == 0 and col_off % tn == 0
        in_specs.append(pl.BlockSpec((k, tn), functools.partial(
            lambda n, m, rb, cb: (rb, cb + n), rb=ro // k, cb=col_off // tn)))
        scratch.append(pltpu.VMEM((k, tn), BF16))
    args = list(a_list) + [w] * n_a
    if epilogue == "resid":
        in_specs.append(pl.BlockSpec((tm, tn), lambda n, m: (first + m, n)))
        args.append(extra)
    elif epilogue == "logsig":
        in_specs.append(pl.BlockSpec((1, tn), lambda n, m: (0, n)))
        args.append(extra)
    if heads_of is None:
        out_spec = pl.BlockSpec((tm, tn), lambda n, m: (m, n))
        out_shape = jax.ShapeDtypeStruct((n_rows, n_out), out_dtype)
    else:
        bsz, seq = heads_of
        per_seq = seq // tm
        out_spec = pl.BlockSpec((1, tn // HEAD_DIM, tm, HEAD_DIM),
                                lambda n, m: (m // per_seq, n, m % per_seq, 0))
        out_shape = jax.ShapeDtypeStruct((bsz, n_out // HEAD_DIM, seq, HEAD_DIM), out_dtype)
    return pl.pallas_call(
        functools.partial(_mm_kernel, n_a=n_a, epilogue=epilogue),
        grid=(n_out // tn, n_rows // tm),
        in_specs=in_specs,
        out_specs=out_spec,
        out_shape=out_shape,
        scratch_shapes=scratch,
        compiler_params=_cparams("arbitrary", "arbitrary"),
        name="matmul_" + epilogue,
    )(*args)


def _glu_kernel(a_ref, w1_ref, w2_ref, o_ref, wb1_ref, wb2_ref):
    @pl.when(pl.program_id(1) == 0)
    def _():
        wb1_ref[...] = w1_ref[...].astype(BF16)
        wb2_ref[...] = w2_ref[...].astype(BF16)

    a = a_ref[...]
    o_ref[...] = _dot(a, wb1_ref[...]) * jax.nn.sigmoid(_dot(a, wb2_ref[...]))


def _matmul_glu(a, w, n_out):
    t, k = a.shape
    tm = _row_tile(t)
    tn = _pick(n_out, (1024, 768, 512, 256, 128))
    nb = n_out // tn
    return pl.pallas_call(
        _glu_kernel,
        grid=(nb, t // tm),
        in_specs=[pl.BlockSpec((tm, k), lambda n, m: (m, 0)),
                  pl.BlockSpec((k, tn), lambda n, m: (0, n)),
                  pl.BlockSpec((k, tn), lambda n, m: (0, n + nb))],
        out_specs=pl.BlockSpec((tm, tn), lambda n, m: (m, n)),
        out_shape=jax.ShapeDtypeStruct((t, n_out), F32),
        scratch_shapes=[pltpu.VMEM((k, tn), BF16), pltpu.VMEM((k, tn), BF16)],
        compiler_params=_cparams("arbitrary", "arbitrary"),
        name="matmul_glu",
    )(a, w, w)


def _swiglu_partial(x, wg, wu, wd):
    g = _dot(x, wg)
    u = _dot(x, wu)
    return _dot((g * jax.nn.sigmoid(g) * u).astype(BF16), wd)


def _ffn_kernel(h_ref, g_ref, wg_ref, wu_ref, wd_ref, o_ref, x_ref):
    @pl.when(pl.program_id(1) == 0)
    def _():
        h = h_ref[...]
        o_ref[...] = h
        y = h * lax.rsqrt(jnp.mean(h * h, axis=-1, keepdims=True) + EPS)
        x_ref[...] = (y * g_ref[...]).astype(BF16)

    o_ref[...] += _swiglu_partial(x_ref[...], wg_ref[...].astype(BF16), wu_ref[...].astype(BF16),
                                  wd_ref[...].astype(BF16))


def _ffn(h, g, w_gate, w_up, w_down):
    t, d = h.shape
    f = w_gate.shape[1]
    tm = _row_tile(t)
    tf = _pick(f, (512, 256, 128))
    return pl.pallas_call(
        _ffn_kernel,
        grid=(t // tm, f // tf),
        in_specs=[pl.BlockSpec((tm, d), lambda m, j: (m, 0), pipeline_mode=pl.Buffered(1)),
                  pl.BlockSpec((1, d), lambda m, j: (0, 0)),
                  pl.BlockSpec((d, tf), lambda m, j: (0, j)),
                  pl.BlockSpec((d, tf), lambda m, j: (0, j)),
                  pl.BlockSpec((tf, d), lambda m, j: (j, 0))],
        out_specs=pl.BlockSpec((tm, d), lambda m, j: (m, 0), pipeline_mode=pl.Buffered(1)),
        out_shape=jax.ShapeDtypeStruct((t, d), F32),
        scratch_shapes=[pltpu.VMEM((tm, d), BF16)],
        compiler_params=_cparams("parallel", "arbitrary"),
        name="ffn",
    )(h, g.reshape(1, d), w_gate, w_up, w_down)


def _router_kernel(x_ref, w_ref, b_ref, o_ref, *, n_experts):
    logits = _dot(x_ref[...], w_ref[...].astype(BF16)) + b_ref[...]
    lane = lax.broadcasted_iota(jnp.int32, logits.shape, 1)
    logits = jnp.where(lane < n_experts, logits, NEG_INF)
    v1 = jnp.max(logits, axis=-1, keepdims=True)
    i1 = jnp.min(jnp.where(logits == v1, lane, LANES), axis=-1, keepdims=True)
    rest = jnp.where(lane == i1, NEG_INF, logits)
    v2 = jnp.max(rest, axis=-1, keepdims=True)
    i2 = jnp.min(jnp.where(rest == v2, lane, LANES), axis=-1, keepdims=True)
    e2 = jnp.exp(v2 - v1)
    p1 = 1.0 / (1.0 + e2)
    p2 = e2 / (1.0 + e2)
    o_ref[...] = jnp.where(lane == i1, p1, jnp.where(lane == i2, p2, 0.0))


def _router(x, w_router, b_router):
    t, d = x.shape
    n_e = w_router.shape[-1]
    tm = _row_tile(t)
    w_pad = jnp.pad(w_router, ((0, 0), (0, LANES - n_e)))
    b_pad = jnp.pad(b_router, (0, LANES - n_e)).reshape(1, LANES)
    return pl.pallas_call(
        functools.partial(_router_kernel, n_experts=n_e),
        grid=(t // tm,),
        in_specs=[pl.BlockSpec((tm, d), lambda m: (m, 0)),
                  pl.BlockSpec((d, LANES), lambda m: (0, 0)),
                  pl.BlockSpec((1, LANES), lambda m: (0, 0))],
        out_specs=pl.BlockSpec((tm, LANES), lambda m: (m, 0)),
        out_shape=jax.ShapeDtypeStruct((t, LANES), F32),
        compiler_params=_cparams("parallel"),
        name="router",
    )(x, w_pad, b_pad)


def _moe_rank_kernel(g_ref, pos_ref, pos_t_ref, start_ref, cnt_ref, carry_ref):
    i = pl.program_id(0)
    tm = g_ref.shape[0]

    @pl.when(i == 0)
    def _():
        carry_ref[...] = jnp.zeros(carry_ref.shape, F32)

    sel = g_ref[...] > 0.0
    r = lax.broadcasted_iota(jnp.int32, (tm, tm), 0)
    c = lax.broadcasted_iota(jnp.int32, (tm, tm), 1)
    before = jnp.where(c < r, 1.0, 0.0).astype(BF16)
    ones = jnp.where(sel, 1.0, 0.0)
    carry = carry_ref[...]
    rank = _dot(before, ones.astype(BF16)) + carry
    start_ref[...] = jnp.broadcast_to(carry, start_ref.shape)
    carry = rank[tm - 1:tm, :] + ones[tm - 1:tm, :]
    carry_ref[...] = carry
    cnt_ref[...] = jnp.broadcast_to(carry, cnt_ref.shape)
    pos = jnp.where(sel, rank, MOE_UNSELECTED)
    pos_ref[...] = pos
    pos_t_ref[...] = pos.T[0:HEAD_PAD, :]


def _moe_rank(gates, tc):
    t = gates.shape[0]
    nch = t // tc
    return pl.pallas_call(
        _moe_rank_kernel,
        grid=(nch,),
        in_specs=[pl.BlockSpec((tc, LANES), lambda i: (i, 0))],
        out_specs=[pl.BlockSpec((tc, LANES), lambda i: (i, 0)),
                   pl.BlockSpec((HEAD_PAD, tc), lambda i: (0, i)),
                   pl.BlockSpec((8, LANES), lambda i: (i, 0)),
                   pl.BlockSpec((8, LANES), lambda i: (0, 0))],
        out_shape=[jax.ShapeDtypeStruct((t, LANES), F32),
                   jax.ShapeDtypeStruct((HEAD_PAD, t), F32),
                   jax.ShapeDtypeStruct((nch * 8, LANES), F32),
                   jax.ShapeDtypeStruct((8, LANES), F32)],
        scratch_shapes=[pltpu.VMEM((1, LANES), F32)],
        compiler_params=_cparams("arbitrary"),
        name="moe_rank",
    )(gates)


def _moe_gather_kernel(sb_ref, ch_ref, first_ref, last_ref, valid_ref, esb_ref, base_ref,
                       pos_t_ref, x_ref, o_ref, acc_ref):
    i = pl.program_id(0)

    @pl.when(valid_ref[i] == 1)
    def _():
        sb = sb_ref[i]
        local = pos_t_ref[pl.ds(esb_ref[sb], 1), :].astype(jnp.int32) + base_ref[sb]
        rows = lax.broadcasted_iota(jnp.int32, (o_ref.shape[0], local.shape[1]), 0)
        onehot = jnp.where(local == rows, 1.0, 0.0).astype(BF16)
        part = _dot(onehot, x_ref[...])

        @pl.when(first_ref[i] == 1)
        def _():
            acc_ref[...] = part

        @pl.when(first_ref[i] == 0)
        def _():
            acc_ref[...] += part

        @pl.when(last_ref[i] == 1)
        def _():
            o_ref[...] = acc_ref[...].astype(o_ref.dtype)


def _moe_gather(plan, pos_t, x, n_rows, tc):
    d = x.shape[1]
    sb = MOE_SUB
    prefetch = (plan["g_sb"], plan["g_ch"], plan["g_first"], plan["g_last"], plan["g_valid"],
                plan["e_sb"], plan["base_sb"])
    grid_spec = pltpu.PrefetchScalarGridSpec(
        num_scalar_prefetch=len(prefetch),
        grid=(plan["g_sb"].shape[0],),
        in_specs=[pl.BlockSpec((HEAD_PAD, tc), lambda i, sbl, chl, *_: (0, chl[i])),
                  pl.BlockSpec((tc, d), lambda i, sbl, chl, *_: (chl[i], 0))],
        out_specs=pl.BlockSpec((sb, d), lambda i, sbl, chl, *_: (sbl[i], 0)),
        scratch_shapes=[pltpu.VMEM((sb, d), F32)],
    )
    return pl.pallas_call(
        _moe_gather_kernel,
        grid_spec=grid_spec,
        out_shape=jax.ShapeDtypeStruct((n_rows, d), BF16),
        compiler_params=_cparams("arbitrary"),
        name="moe_gather",
    )(*prefetch, pos_t, x)


def _moe_ffn_kernel(et_ref, nv_ref, x_ref, wg_ref, wu_ref, wd_ref, o_ref, acc_ref):
    del et_ref
    r = pl.program_id(0)
    f = pl.program_id(1)
    last_f = pl.num_programs(1) - 1
    n_valid = nv_ref[r]
    n_sub = o_ref.shape[0] // MOE_FFN_SUB

    @pl.when(f == 0)
    def _():
        acc_ref[...] = jnp.zeros(acc_ref.shape, F32)

    for n in range(1, n_sub + 1):
        @pl.when(n_valid == n)
        def _():
            rows = slice(0, n * MOE_FFN_SUB)
            acc_ref[rows, :] += _swiglu_partial(x_ref[rows, :], wg_ref[0].astype(BF16), wu_ref[0].astype(BF16),
                                                wd_ref[0].astype(BF16))

    @pl.when(f == last_f)
    def _():
        o_ref[...] = acc_ref[...].astype(o_ref.dtype)


def _moe_ffn(plan, xs, w_gate, w_up, w_down):
    n_rows, d = xs.shape
    f = w_gate.shape[2]
    tr = MOE_ROW_TILE
    tf = _pick(f, (512, 256, 128))
    nf = f // tf

    def w_in_map(r, j, et, nv):
        return (et[r], 0, jnp.where(nv[r] > 0, j, nf - 1))

    def w_out_map(r, j, et, nv):
        return (et[r], jnp.where(nv[r] > 0, j, nf - 1), 0)

    grid_spec = pltpu.PrefetchScalarGridSpec(
        num_scalar_prefetch=2,
        grid=(n_rows // tr, nf),
        in_specs=[pl.BlockSpec((tr, d), lambda r, j, et, nv: (r, 0)),
                  pl.BlockSpec((1, d, tf), w_in_map),
                  pl.BlockSpec((1, d, tf), w_in_map),
                  pl.BlockSpec((1, tf, d), w_out_map)],
        out_specs=pl.BlockSpec((tr, d), lambda r, j, et, nv: (r, 0)),
        scratch_shapes=[pltpu.VMEM((tr, d), F32)],
    )
    return pl.pallas_call(
        _moe_ffn_kernel,
        grid_spec=grid_spec,
        out_shape=jax.ShapeDtypeStruct((n_rows, d), BF16),
        compiler_params=_cparams("arbitrary", "arbitrary"),
        name="moe_ffn",
    )(plan["e_tile"], plan["nv_tile"], xs, w_gate, w_up, w_down)


def _moe_combine_kernel(ch_ref, sb_ref, first_ref, valid_ref, esb_ref, base_ref,
                        pos_ref, gate_ref, h_ref, y_ref, o_ref):
    i = pl.program_id(0)

    @pl.when(first_ref[i] == 1)
    def _():
        o_ref[...] = h_ref[...]

    @pl.when(valid_ref[i] == 1)
    def _():
        sb = sb_ref[i]
        lane = lax.broadcasted_iota(jnp.int32, pos_ref.shape, 1)
        mine = lane == esb_ref[sb]
        rank = jnp.sum(jnp.where(mine, pos_ref[...], 0.0), axis=-1, keepdims=True)
        local = rank.astype(jnp.int32) + base_ref[sb]
        gate = jnp.sum(jnp.where(mine, gate_ref[...], 0.0), axis=-1, keepdims=True)
        cols = lax.broadcasted_iota(jnp.int32, (pos_ref.shape[0], y_ref.shape[0]), 1)
        onehot = jnp.where(local == cols, 1.0, 0.0).astype(BF16)
        o_ref[...] += gate * _dot(onehot, y_ref[...])


def _moe_combine(plan, pos, gates, h, ys, tc):
    t, d = h.shape
    sb = MOE_SUB
    prefetch = (plan["c_ch"], plan["c_sb"], plan["c_first"], plan["c_valid"], plan["e_sb"], plan["base_sb"])
    grid_spec = pltpu.PrefetchScalarGridSpec(
        num_scalar_prefetch=len(prefetch),
        grid=(plan["c_ch"].shape[0],),
        in_specs=[pl.BlockSpec((tc, LANES), lambda i, chl, sbl, *_: (chl[i], 0)),
                  pl.BlockSpec((tc, LANES), lambda i, chl, sbl, *_: (chl[i], 0)),
                  pl.BlockSpec((tc, d), lambda i, chl, sbl, *_: (chl[i], 0)),
                  pl.BlockSpec((sb, d), lambda i, chl, sbl, *_: (sbl[i], 0))],
        out_specs=pl.BlockSpec((tc, d), lambda i, chl, sbl, *_: (chl[i], 0)),
    )
    return pl.pallas_call(
        _moe_combine_kernel,
        grid_spec=grid_spec,
        out_shape=jax.ShapeDtypeStruct((t, d), F32),
        compiler_params=_cparams("arbitrary"),
        name="moe_combine",
    )(*prefetch, pos, gates, h, ys)


def _pair_list(overlap, length):
    n_b = overlap.shape[1]
    flat = overlap.reshape(-1)
    n = jnp.sum(flat.astype(jnp.int32))
    idx = jnp.nonzero(flat, size=length, fill_value=0)[0].astype(jnp.int32)
    step = jnp.arange(length, dtype=jnp.int32)
    valid = step < n
    idx = jnp.where(valid, idx, idx[jnp.maximum(n - 1, 0)])
    a = idx // n_b
    b = idx % n_b
    nxt_valid = jnp.concatenate([valid[1:], jnp.zeros((1,), bool)])
    first = valid & (a != jnp.concatenate([jnp.full((1,), -1, jnp.int32), a[:-1]]))
    last = valid & ((a != jnp.concatenate([a[1:], jnp.full((1,), -1, jnp.int32)])) | ~nxt_valid)
    as_i32 = lambda v: v.astype(jnp.int32)
    return a, b, as_i32(first), as_i32(last), as_i32(valid)


def _moe_plan(counts, chunk_start, n_rows, nch):
    n_e = counts.shape[0]
    tr, sb = MOE_ROW_TILE, MOE_SUB
    seg = (counts + tr - 1) // tr * tr
    seg_end = jnp.cumsum(seg)
    off = seg_end - seg
    chunk_end = jnp.concatenate([chunk_start[:, 1:], counts[:, None]], axis=1)

    def owner(start):
        return jnp.minimum(jnp.sum(seg_end[None, :] <= start[:, None], axis=1), n_e - 1).astype(jnp.int32)

    sb_start = jnp.arange(n_rows // sb, dtype=jnp.int32) * sb
    e_sb = owner(sb_start)
    k0 = sb_start - off[e_sb]
    k1 = jnp.minimum(k0 + sb, counts[e_sb])
    sb_used = (sb_start < seg_end[-1]) & (k0 < counts[e_sb])
    overlap = sb_used[:, None] & (chunk_start[e_sb] < k1[:, None]) & (chunk_end[e_sb] > k0[:, None])
    length = n_rows // sb + n_e * nch
    g_sb, g_ch, g_first, g_last, g_valid = _pair_list(overlap, length)
    c_ch, c_sb, c_first, _, c_valid = _pair_list(overlap.T, length)

    tile_start = jnp.arange(n_rows // tr, dtype=jnp.int32) * tr
    e_tile = owner(tile_start)
    left = counts[e_tile] - (tile_start - off[e_tile])
    step = MOE_FFN_SUB
    nv_tile = jnp.where(tile_start < seg_end[-1], jnp.clip((left + step - 1) // step, 0, tr // step), 0)
    return dict(g_sb=g_sb, g_ch=g_ch, g_first=g_first, g_last=g_last, g_valid=g_valid,
                c_ch=c_ch, c_sb=c_sb, c_first=c_first, c_valid=c_valid,
                e_sb=e_sb, base_sb=(off[e_sb] - sb_start).astype(jnp.int32),
                e_tile=e_tile, nv_tile=nv_tile.astype(jnp.int32))


def _moe(x, h, w_router, b_router, w_gate, w_up, w_down):
    t = x.shape[0]
    n_e = w_router.shape[-1]
    tc = _pick(t, (640, 512, 384, 256, 128))
    assert tc % LANES == 0 and t % tc == 0 and n_e <= 8
    nch = t // tc
    n_rows = -(-(TOP_K * t + n_e * (MOE_ROW_TILE - 1)) // MOE_ROW_TILE) * MOE_ROW_TILE
    gates = _router(x, w_router, b_router)
    pos, pos_t, starts, totals = _moe_rank(gates, tc)
    counts = totals[0, :n_e].astype(jnp.int32)
    chunk_start = starts.reshape(nch, 8, LANES)[:, 0, :n_e].T.astype(jnp.int32)
    plan = _moe_plan(counts, chunk_start, n_rows, nch)
    xs = _moe_gather(plan, pos_t, x, n_rows, tc)
    ys = _moe_ffn(plan, xs, w_gate, w_up, w_down)
    return _moe_combine(plan, pos, gates, h, ys, tc)


def _ln_swish(hc, lg, lb):
    mu = jnp.mean(hc, axis=-1, keepdims=True)
    xc = hc - mu
    var = jnp.mean(xc * xc, axis=-1, keepdims=True)
    y = xc * lax.rsqrt(var + EPS) * lg + lb
    return y * jax.nn.sigmoid(y)


def _conv_kernel(prev_ref, main_ref, w_ref, b_ref, lg_ref, lb_ref, o_ref, ext_ref, acc_ref, sh_ref, *, tt, width):
    @pl.when(pl.program_id(1) == 0)
    def _():
        ext_ref[0:CONV_HALO, :] = jnp.zeros((CONV_HALO, ext_ref.shape[1]), F32)

    @pl.when(pl.program_id(1) > 0)
    def _():
        ext_ref[0:CONV_HALO, :] = prev_ref[...]

    ext_ref[CONV_HALO:CONV_HALO + tt, :] = main_ref[...]
    n_chunks = o_ref.shape[-1] // LANES
    lead = CONV_HALO - (width - 1)

    span = tt + CONV_HALO - SUBLANES

    def chunk(c, carry):
        c0 = pl.multiple_of(c * LANES, LANES)
        for r in range(1, SUBLANES):
            sh_ref[r - 1] = ext_ref[pl.ds(r, span), pl.ds(c0, LANES)]
        acc = jnp.zeros((tt, LANES), F32)
        for j in range(width):
            phase = (lead + j) % SUBLANES
            base = lead + j - phase
            if phase == 0:
                rows = ext_ref[pl.ds(base, tt), pl.ds(c0, LANES)]
            else:
                rows = sh_ref[phase - 1, pl.ds(base, tt), :]
            acc = acc + rows * w_ref[pl.ds(j, 1), pl.ds(c0, LANES)]
        acc_ref[:, pl.ds(c0, LANES)] = acc + b_ref[:, pl.ds(c0, LANES)]
        return carry

    lax.fori_loop(0, n_chunks, chunk, 0)
    o_ref[...] = _ln_swish(acc_ref[...], lg_ref[...], lb_ref[...]).astype(o_ref.dtype)


def _conv_prompt(u, w, b, lg, lb, bsz, seq):
    c = u.shape[1]
    width = w.shape[0]
    tt = _pick(seq, (256, 128, 64, 32))
    nt = seq // tt
    per_halo = tt // CONV_HALO
    return pl.pallas_call(
        functools.partial(_conv_kernel, tt=tt, width=width),
        grid=(bsz, nt),
        in_specs=[pl.BlockSpec((CONV_HALO, c), lambda bb, i: (jnp.maximum((bb * nt + i) * per_halo - 1, 0), 0)),
                  pl.BlockSpec((tt, c), lambda bb, i: (bb * nt + i, 0)),
                  pl.BlockSpec((width, c), lambda bb, i: (0, 0)),
                  pl.BlockSpec((1, c), lambda bb, i: (0, 0)),
                  pl.BlockSpec((1, c), lambda bb, i: (0, 0)),
                  pl.BlockSpec((1, c), lambda bb, i: (0, 0))],
        out_specs=pl.BlockSpec((tt, c), lambda bb, i: (bb * nt + i, 0)),
        out_shape=jax.ShapeDtypeStruct((u.shape[0], c), BF16),
        scratch_shapes=[pltpu.VMEM((tt + CONV_HALO, c), F32), pltpu.VMEM((tt, c), F32),
                        pltpu.VMEM((SUBLANES - 1, tt + CONV_HALO - SUBLANES, LANES), F32)],
        compiler_params=_cparams("parallel", "parallel"),
        name="conv_prompt",
    )(u, u, w, b.reshape(1, c), lg.reshape(1, c), lb.reshape(1, c))


def _conv_step_kernel(ext_ref, w_ref, b_ref, lg_ref, lb_ref, o_ref, *, width):
    n_t = o_ref.shape[0]
    for t in range(n_t):
        acc = ext_ref[t] * w_ref[0:1, :]
        for j in range(1, width):
            acc = acc + ext_ref[t + j] * w_ref[j:j + 1, :]
        o_ref[t] = _ln_swish(acc + b_ref[...], lg_ref[...], lb_ref[...]).astype(o_ref.dtype)


def _conv_sample(ext_t, w, b, lg, lb):
    l_ext, n, c = ext_t.shape
    width = w.shape[0]
    n_t = l_ext - (width - 1)
    return pl.pallas_call(
        functools.partial(_conv_step_kernel, width=width),
        out_shape=jax.ShapeDtypeStruct((n_t, n, c), BF16),
        compiler_params=pltpu.CompilerParams(vmem_limit_bytes=VMEM_LIMIT_BYTES),
        name="conv_sample",
    )(ext_t, w, b.reshape(1, c), lg.reshape(1, c), lb.reshape(1, c))


def _softmax_attend(q, k, v):
    s = _dot_nt(q, k) * ATTN_SCALE
    p = jnp.exp(s - jnp.max(s, axis=-1, keepdims=True))
    l = jnp.sum(p, axis=-1, keepdims=True)
    return _dot(p.astype(BF16), v) / l


def _mem_attn_kernel(q_ref, k_ref, v_ref, o_ref):
    for hh in range(MEM_HEADS):
        sl = slice(hh * HEAD_DIM, (hh + 1) * HEAD_DIM)
        o_ref[:, sl] = _softmax_attend(q_ref[:, sl].astype(BF16), k_ref[0, hh].astype(BF16),
                                       v_ref[0, hh].astype(BF16)).astype(o_ref.dtype)


def _mem_attn_cache_kernel(q_ref, k_ref, v_ref, o_ref, *, rows_per_req):
    n_mem = k_ref.shape[2] // MEM_HEADS
    for r in range(k_ref.shape[1]):
        rows = slice(r * rows_per_req, (r + 1) * rows_per_req)
        for hh in range(MEM_HEADS):
            sl = slice(hh * HEAD_DIM, (hh + 1) * HEAD_DIM)
            head_rows = pl.ds(hh, n_mem, stride=MEM_HEADS)
            o_ref[rows, sl] = _softmax_attend(q_ref[rows, sl].astype(BF16),
                                              k_ref[0, r, head_rows, :].astype(BF16),
                                              v_ref[0, r, head_rows, :].astype(BF16)).astype(o_ref.dtype)


def _mem_attn_cache(q, k_cache, v_cache, layer, rows_per_req):
    n_layers, n, m, _, _ = k_cache.shape
    w = q.shape[1]
    group = _pick(n, (4, 2, 1))
    flat = (n_layers, n, m * MEM_HEADS, HEAD_DIM)
    kv_spec = pl.BlockSpec((1, group, m * MEM_HEADS, HEAD_DIM), lambda j: (layer, j, 0, 0))
    return pl.pallas_call(
        functools.partial(_mem_attn_cache_kernel, rows_per_req=rows_per_req),
        grid=(n // group,),
        in_specs=[pl.BlockSpec((group * rows_per_req, w), lambda j: (j, 0)), kv_spec, kv_spec],
        out_specs=pl.BlockSpec((group * rows_per_req, w), lambda j: (j, 0)),
        out_shape=jax.ShapeDtypeStruct((n * rows_per_req, w), BF16),
        compiler_params=_cparams("parallel"),
        name="mem_attn_cache",
    )(q, k_cache.reshape(flat), v_cache.reshape(flat))


def _mem_attn(q, k, v, rows_per_req, col_off=0):
    w = MEM_WIDTH
    n, _, m, _ = k.shape
    tq = _pick(rows_per_req, (512, 256, 128, 64, 32, 16))
    per_req = rows_per_req // tq
    col_block = col_off // w
    assert col_off % w == 0
    return pl.pallas_call(
        _mem_attn_kernel,
        grid=(n, per_req),
        in_specs=[pl.BlockSpec((tq, w), lambda i, j: (i * per_req + j, col_block)),
                  pl.BlockSpec((1, MEM_HEADS, m, HEAD_DIM), lambda i, j: (i, 0, 0, 0)),
                  pl.BlockSpec((1, MEM_HEADS, m, HEAD_DIM), lambda i, j: (i, 0, 0, 0))],
        out_specs=pl.BlockSpec((tq, w), lambda i, j: (i * per_req + j, 0)),
        out_shape=jax.ShapeDtypeStruct((max(q.shape[0], n * rows_per_req), w), BF16),
        compiler_params=_cparams("parallel", "parallel"),
        name="mem_attn",
    )(q, k, v)


def _cumsum_kernel(lf_ref, crep_ref, crow_ref, *, blk, n_heads):
    seq = lf_ref.shape[1]
    r = lax.broadcasted_iota(jnp.int32, (blk, blk), 0)
    c = lax.broadcasted_iota(jnp.int32, (blk, blk), 1)
    tri = (c <= r).astype(F32)
    carry = jnp.zeros((1, LANES), F32)
    for i in range(seq // blk):
        rows = slice(i * blk, (i + 1) * blk)
        cs = _dot_exact(tri, lf_ref[0, rows, :]) + carry
        carry = cs[blk - 1:blk, :]
        cs2 = cs * LOG2_E
        crow_ref[0, :, rows] = cs2.T[0:HEAD_PAD, :]
        for hh in range(n_heads):
            crep_ref[0, hh, rows, :] = jnp.broadcast_to(cs2[:, hh:hh + 1], (blk, LANES))


def _fox_cumsum(logf_pad, n_heads):
    bsz, seq, _ = logf_pad.shape
    blk = _pick(seq, (256, 128))
    return pl.pallas_call(
        functools.partial(_cumsum_kernel, blk=blk, n_heads=n_heads),
        grid=(bsz,),
        in_specs=[pl.BlockSpec((1, seq, LANES), lambda b: (b, 0, 0))],
        out_specs=[pl.BlockSpec((1, n_heads, seq, LANES), lambda b: (b, 0, 0, 0)),
                   pl.BlockSpec((1, HEAD_PAD, seq), lambda b: (b, 0, 0))],
        out_shape=[jax.ShapeDtypeStruct((bsz, n_heads, seq, LANES), F32),
                   jax.ShapeDtypeStruct((bsz, HEAD_PAD, seq), F32)],
        compiler_params=_cparams("parallel"),
        name="fox_cumsum",
    )(logf_pad)


def _fox_prompt_kernel(q_ref, k_ref, v_ref, crep_ref, crow_ref, o_ref,
                       kb_ref, vt_ref, m_ref, l_ref, acc_ref, *, tq, tk):
    hh = pl.program_id(1)
    qi = pl.program_id(2)

    @pl.when(qi == 0)
    def _():
        kb_ref[...] = k_ref[0, 0].astype(BF16)
        vt_ref[...] = v_ref[0, 0].T.astype(BF16)

    q = q_ref[...].astype(BF16)
    q0 = pl.multiple_of(qi * tq, tq)
    cq = crow_ref[0, pl.ds(hh, 1), pl.ds(q0, tq)]
    m_ref[...] = jnp.full(m_ref.shape, NEG_INF, F32)
    l_ref[...] = jnp.zeros(l_ref.shape, F32)
    acc_ref[...] = jnp.zeros(acc_ref.shape, F32)

    def update(k0, on_diagonal):
        ck = jnp.tile(crep_ref[0, 0, pl.ds(k0, tk), :], (1, tq // LANES))
        z = _dot_nt(kb_ref[pl.ds(k0, tk), :], q) * (ATTN_SCALE * LOG2_E) - ck
        if on_diagonal:
            z = jnp.where(lax.broadcasted_iota(jnp.int32, (tk, tq), 0) <= lax.broadcasted_iota(jnp.int32, (tk, tq), 1),
                          z, NEG_INF)
        m_old = m_ref[...]
        m_new = jnp.maximum(m_old, jnp.max(z, axis=0, keepdims=True) + cq)
        alpha = jnp.exp2(m_old - m_new)
        p = jnp.exp2(z - (m_new - cq))
        l_ref[...] = alpha * l_ref[...] + jnp.sum(p, axis=0, keepdims=True)
        acc_ref[...] = alpha * acc_ref[...] + _dot(vt_ref[:, pl.ds(k0, tk)], p.astype(BF16))
        m_ref[...] = m_new

    def body(kj, carry):
        update(pl.multiple_of(kj * tk, tk), False)
        return carry

    lax.fori_loop(0, qi, body, 0)
    update(q0, True)
    o_ref[...] = (acc_ref[...] / l_ref[...]).T.astype(o_ref.dtype)


def _fox_prompt(q, k, v, crep, crow):
    bsz, n_heads, seq, _ = k.shape
    tq = _pick(seq, (512, 256, 128))
    tk = tq
    nq = seq // tq
    head_seq = pl.BlockSpec((1, 1, seq, HEAD_DIM), lambda b, h, i: (b, h, 0, 0))
    return pl.pallas_call(
        functools.partial(_fox_prompt_kernel, tq=tq, tk=tk),
        grid=(bsz, n_heads, nq),
        in_specs=[pl.BlockSpec((tq, HEAD_DIM), lambda b, h, i: (b * nq + i, h)),
                  head_seq,
                  head_seq,
                  pl.BlockSpec((1, 1, seq, LANES), lambda b, h, i: (b, h, 0, 0)),
                  pl.BlockSpec((1, HEAD_PAD, seq), lambda b, h, i: (b, 0, 0))],
        out_specs=pl.BlockSpec((tq, HEAD_DIM), lambda b, h, i: (b * nq + i, h)),
        out_shape=jax.ShapeDtypeStruct((q.shape[0], n_heads * HEAD_DIM), BF16),
        scratch_shapes=[pltpu.VMEM((seq, HEAD_DIM), BF16), pltpu.VMEM((HEAD_DIM, seq), BF16),
                        pltpu.VMEM((1, tq), F32), pltpu.VMEM((1, tq), F32),
                        pltpu.VMEM((HEAD_DIM, tq), F32)],
        compiler_params=_cparams("parallel", "parallel", "arbitrary"),
        name="fox_prompt",
    )(q, k, v, crep, crow)


def _fox_sample_kernel(*refs, pages_per_step, n_new, n_heads):
    pp = pages_per_step
    pt_ref = refs[0]
    q_ref, knew_ref, vnew_ref, lfnew_ref, lfcol_ref = refs[1:6]
    k_refs = refs[6:6 + pp]
    v_refs = refs[6 + pp:6 + 2 * pp]
    lf_refs = refs[6 + 2 * pp:6 + 3 * pp]
    o_ref = refs[6 + 3 * pp]
    m_ref, l_ref, acc_ref, carry_ref, cn_ref, knp_ref, vnp_ref = refs[7 + 3 * pp:]
    step = pl.program_id(1)
    page = k_refs[0].shape[2]
    n_rows = n_heads * Q_ROWS

    r = lax.broadcasted_iota(jnp.int32, (page, page), 0)
    c = lax.broadcasted_iota(jnp.int32, (page, page), 1)

    @pl.when(step == 0)
    def _():
        m_ref[...] = jnp.full(m_ref.shape, NEG_INF, F32)
        l_ref[...] = jnp.zeros(l_ref.shape, F32)
        acc_ref[...] = jnp.zeros(acc_ref.shape, F32)
        carry_ref[...] = jnp.zeros(carry_ref.shape, F32)
        lf_col = lfcol_ref[0]
        tok = lax.broadcasted_iota(jnp.int32, lf_col.shape, 0)
        c_col = jnp.zeros(lf_col.shape, F32)
        for j in range(n_new):
            c_col = c_col + jnp.where(tok >= j, lf_col[j:j + 1, :], 0.0)
        for hh in range(n_heads):
            cn_ref[hh * Q_ROWS:(hh + 1) * Q_ROWS, :] = c_col[:, hh:hh + 1]

    def per_head_rows(x16):
        return jnp.concatenate([jnp.broadcast_to(x16[hh:hh + 1, :], (Q_ROWS, page)) for hh in range(n_heads)],
                               axis=0)

    def attend(k_list, v_list, bias, mask=None):
        s = jnp.concatenate(
            [jnp.concatenate([_dot_nt(q_ref[0, hh], k_ref[0, hh].astype(BF16)) for hh in range(n_heads)], axis=0)
             for k_ref in k_list], axis=1)
        s = s * ATTN_SCALE + bias
        if mask is not None:
            s = jnp.where(mask, s, NEG_INF)
        m_old = m_ref[...]
        m_new = jnp.maximum(m_old, jnp.max(s, axis=-1, keepdims=True))
        alpha = jnp.exp(m_old - m_new)
        p = jnp.exp(s - m_new)
        l_ref[...] = alpha * l_ref[...] + jnp.sum(p, axis=-1, keepdims=True)
        pv = None
        for j, v_ref in enumerate(v_list):
            part = jnp.concatenate(
                [_dot(p[hh * Q_ROWS:(hh + 1) * Q_ROWS, j * page:(j + 1) * page].astype(BF16),
                      v_ref[0, hh].astype(BF16)) for hh in range(n_heads)], axis=0)
            pv = part if pv is None else pv + part
        acc_ref[...] = alpha * acc_ref[...] + pv
        m_ref[...] = m_new

    suffix = (r > c).astype(F32)
    carry = carry_ref[...]
    decay = []
    n_pages = pl.num_programs(1) * pp
    for j in range(pp):
        page_id = pt_ref[pl.program_id(0), n_pages - 1 - (step * pp + j)]
        lf = lf_refs[j][:, pl.ds(page_id % LF_PAGE_GROUP, 1), :].reshape(n_heads, page)
        decay.append(per_head_rows(_dot_exact(lf, suffix) + carry))
        carry = carry + jnp.sum(lf, axis=-1, keepdims=True)
    carry_ref[...] = carry
    attend(k_refs, v_refs, cn_ref[...] + jnp.concatenate(decay, axis=1))

    @pl.when(step == pl.num_programs(1) - 1)
    def _():
        c_row = _dot_exact(lfnew_ref[0], (r <= c).astype(F32))
        key = lax.broadcasted_iota(jnp.int32, (n_rows, page), 1)
        tok = lax.broadcasted_iota(jnp.int32, (n_rows, page), 0) % Q_ROWS
        knp_ref[...] = jnp.zeros(knp_ref.shape, F32)
        vnp_ref[...] = jnp.zeros(vnp_ref.shape, F32)
        for hh in range(n_heads):
            knp_ref[0, hh, 0:Q_ROWS, :] = knew_ref[0, hh]
            vnp_ref[0, hh, 0:Q_ROWS, :] = vnew_ref[0, hh]
        attend([knp_ref], [vnp_ref], cn_ref[...] - per_head_rows(c_row), mask=key <= tok)
        o_ref[0] = acc_ref[...] / l_ref[...]


def _fox_sample(q_rows, k_new, v_new, lf_new, lf_new_col, k_pool, v_pool, lf_pool, page_table, n_new):
    n, n_heads = q_rows.shape[:2]
    page = k_pool.shape[2]
    n_pages = page_table.shape[1]
    pp = _pick(n_pages, (8, 4, 2, 1))
    n_steps = n_pages // pp
    n_rows = n_heads * Q_ROWS

    def kv_map(j):
        return lambda i, s, pt: (pt[i, n_pages - 1 - (s * pp + j)], 0, 0, 0)

    def lf_map(j):
        return lambda i, s, pt: (0, pt[i, n_pages - 1 - (s * pp + j)] // LF_PAGE_GROUP, 0)

    req3 = lambda i, s, pt: (i, 0, 0)
    req4 = lambda i, s, pt: (i, 0, 0, 0)
    kv_block = (1, n_heads, page, HEAD_DIM)
    new_block = pl.BlockSpec((1, n_heads, Q_ROWS, HEAD_DIM), req4)
    in_specs = [new_block, new_block, new_block,
                pl.BlockSpec((1, HEAD_PAD, page), req3),
                pl.BlockSpec((1, Q_ROWS, LANES), req3)]
    in_specs += [pl.BlockSpec(kv_block, kv_map(j)) for j in range(pp)]
    in_specs += [pl.BlockSpec(kv_block, kv_map(j)) for j in range(pp)]
    in_specs += [pl.BlockSpec((n_heads, LF_PAGE_GROUP, page), lf_map(j)) for j in range(pp)]
    grid_spec = pltpu.PrefetchScalarGridSpec(
        num_scalar_prefetch=1,
        grid=(n, n_steps),
        in_specs=in_specs,
        out_specs=pl.BlockSpec((1, n_rows, HEAD_DIM), req3),
        scratch_shapes=[pltpu.VMEM((n_rows, 1), F32), pltpu.VMEM((n_rows, 1), F32),
                        pltpu.VMEM((n_rows, HEAD_DIM), F32), pltpu.VMEM((n_heads, 1), F32),
                        pltpu.VMEM((n_rows, 1), F32), pltpu.VMEM(kv_block, F32), pltpu.VMEM(kv_block, F32)],
    )
    return pl.pallas_call(
        functools.partial(_fox_sample_kernel, pages_per_step=pp, n_new=n_new, n_heads=n_heads),
        grid_spec=grid_spec,
        out_shape=jax.ShapeDtypeStruct((n, n_rows, HEAD_DIM), F32),
        compiler_params=_cparams("parallel", "arbitrary"),
        name="fox_sample",
    )(page_table, q_rows, k_new, v_new, lf_new, lf_new_col,
      *([k_pool] * pp), *([v_pool] * pp), *([lf_pool] * pp))


def kernel(x_prompt, x_sample, cache_fox_k, cache_fox_v, cache_fox_logf, state_conv, cache_mem_k, cache_mem_v,
           page_table, mem_prompt, g_mix, g_ffn, g_mem, w_mem_kv, w_in_a, w_out_a, conv_w, conv_b, conv_ln_g,
           conv_ln_b, g_kv, w_kvf, b_f, w_in_b, w_out_b, w_ff_gate, w_ff_up, w_ff_down, w_router, b_router,
           w_e_gate, w_e_up, w_e_down, g_final):
    bsz, seq, d = x_prompt.shape
    n_dec, t_dec, _ = x_sample.shape
    depth = g_mix.shape[0]
    n_a = w_in_a.shape[0]
    conv_ch = conv_w.shape[-1]
    width = conv_w.shape[1]
    n_mem = mem_prompt.shape[1]
    fox_w = w_in_b.shape[-1] - MEM_WIDTH
    n_heads = fox_w // HEAD_DIM
    n_pool, page = cache_fox_k.shape[:2]
    tp = bsz * seq
    ts = n_dec * t_dec
    assert n_heads <= HEAD_PAD and width - 1 <= CONV_HALO and t_dec <= 8

    h = jnp.concatenate([x_prompt.reshape(tp, d), x_sample.reshape(ts, d)], axis=0)

    mem_flat = mem_prompt.reshape(bsz * n_mem, d)
    mem_k_p, mem_v_p = [], []
    for i in range(depth):
        z = _matmul([_rmsnorm(mem_flat, g_mem[i], BF16)], w_mem_kv[i], [0], 0, 2 * MEM_WIDTH)
        mem_k_p.append(z[:, :MEM_WIDTH].reshape(bsz, n_mem, MEM_HEADS, HEAD_DIM).transpose(0, 2, 1, 3))
        mem_v_p.append(z[:, MEM_WIDTH:].reshape(bsz, n_mem, MEM_HEADS, HEAD_DIM).transpose(0, 2, 1, 3))

    def mem_attend(q_all, col_off, i):
        q_s = jnp.pad(q_all[tp:, col_off:col_off + MEM_WIDTH].reshape(n_dec, t_dec, MEM_WIDTH),
                      ((0, 0), (0, SAMPLE_Q_PAD - t_dec), (0, 0)))
        o_p = _mem_attn(q_all, mem_k_p[i], mem_v_p[i], seq, col_off)
        o_s = _mem_attn_cache(q_s.reshape(n_dec * SAMPLE_Q_PAD, MEM_WIDTH), cache_mem_k, cache_mem_v, i,
                              SAMPLE_Q_PAD)
        o_s = o_s.reshape(n_dec, SAMPLE_Q_PAD, MEM_WIDTH)[:, :t_dec].reshape(ts, MEM_WIDTH)
        return o_p.at[tp:].set(o_s)

    new_conv_p, new_conv_s = [], []
    kv = None
    for i in range(depth):
        if i == n_a:
            hn, hk = _rmsnorm_pair(h, g_mix[i], g_kv)
        else:
            hn = _rmsnorm(h, g_mix[i], BF16)
        if i < n_a:
            u = _matmul_glu(hn, w_in_a[i], conv_ch)
            q_all, q_mem_off = _matmul([hn], w_in_a[i], [0], 2 * conv_ch, MEM_WIDTH), 0
            u_s = u[tp:].reshape(n_dec, t_dec, conv_ch)
            ext_s = jnp.concatenate([state_conv[i], u_s], axis=1)
            keep = min(seq, width - 1)
            tail = jnp.stack([u[(b + 1) * seq - keep:(b + 1) * seq] for b in range(bsz)])
            new_conv_p.append(jnp.pad(tail, ((0, 0), (width - 1 - keep, 0), (0, 0))))
            new_conv_s.append(ext_s[:, t_dec:])
            mix_p = _conv_prompt(u, conv_w[i], conv_b[i], conv_ln_g[i], conv_ln_b[i], bsz, seq)
            mix_s = _conv_sample(ext_s.transpose(1, 0, 2), conv_w[i], conv_b[i], conv_ln_g[i], conv_ln_b[i])
            mix = mix_p.at[tp:].set(mix_s.transpose(1, 0, 2).reshape(ts, conv_ch))
            w_out = w_out_a[i]
        else:
            j = i - n_a
            if kv is None:
                k_p = _matmul([hk], w_kvf, [0], 0, fox_w, n_rows=tp, heads_of=(bsz, seq))
                v_p = _matmul([hk], w_kvf, [0], fox_w, fox_w, n_rows=tp, heads_of=(bsz, seq))
                kv_s = _matmul([hk], w_kvf, [0], 0, 2 * fox_w, row_start=tp, n_rows=ts)
                w_f = jnp.pad(w_kvf[:, 2 * fox_w:], ((0, 0), (0, LANES - n_heads)))
                b_f_pad = jnp.pad(b_f, (0, LANES - n_heads)).reshape(1, LANES)
                logf_pad = _matmul([hk], w_f, [0], 0, LANES, epilogue="logsig", extra=b_f_pad)
                crep, crow = _fox_cumsum(logf_pad[:tp].reshape(bsz, seq, LANES), n_heads)
                def new_rows(x):
                    x = x.reshape(n_dec, t_dec, n_heads, HEAD_DIM).transpose(0, 2, 1, 3)
                    return jnp.pad(x, ((0, 0), (0, 0), (0, Q_ROWS - t_dec), (0, 0)))

                k_new = new_rows(kv_s[:, :fox_w])
                v_new = new_rows(kv_s[:, fox_w:])
                lf_new = jnp.pad(logf_pad[tp:, :n_heads].reshape(n_dec, t_dec, n_heads).transpose(0, 2, 1),
                                 ((0, 0), (0, HEAD_PAD - n_heads), (0, page - t_dec)))
                lf_new_col = jnp.pad(logf_pad[tp:].reshape(n_dec, t_dec, LANES),
                                     ((0, 0), (0, Q_ROWS - t_dec), (0, 0)))
                lf_pool = cache_fox_logf.astype(F32).transpose(2, 0, 1)
                kv = (k_p, v_p, kv_s, logf_pad)
            q_all, q_mem_off = _matmul([hn], w_in_b[j], [0], 0, fox_w + MEM_WIDTH), fox_w
            mix_p = _fox_prompt(q_all, k_p, v_p, crep, crow)
            q_rows = jnp.pad(q_all[tp:, :fox_w].reshape(n_dec, t_dec, n_heads, HEAD_DIM).transpose(0, 2, 1, 3),
                             ((0, 0), (0, 0), (0, Q_ROWS - t_dec), (0, 0))).astype(BF16)
            o_rows = _fox_sample(q_rows, k_new, v_new, lf_new, lf_new_col, cache_fox_k.transpose(0, 2, 1, 3),
                                 cache_fox_v.transpose(0, 2, 1, 3), lf_pool, page_table, t_dec)
            mix_s = o_rows.reshape(n_dec, n_heads, Q_ROWS, HEAD_DIM)[:, :, :t_dec].transpose(0, 2, 1, 3)
            mix = mix_p.at[tp:].set(mix_s.reshape(ts, fox_w).astype(BF16))
            w_out = w_out_b[j]
        mo = mem_attend(q_all, q_mem_off, i)
        h = _matmul([mix, mo], w_out, [0, mix.shape[1]], 0, d, epilogue="resid", extra=h)
        jj = i // 2
        if i % 2 == 0:
            h = _ffn(h, g_ffn[i], w_ff_gate[jj], w_ff_up[jj], w_ff_down[jj])
        else:
            hf = _rmsnorm(h, g_ffn[i], BF16)
            h = _moe(hf, h, w_router[jj], b_router[jj], w_e_gate[jj], w_e_up[jj], w_e_down[jj])

    y_p = _rmsnorm(h, g_final, F32, 0, tp)
    y_s = _rmsnorm(h, g_final, F32, tp, ts)
    k_p, v_p, kv_s, logf_pad = kv
    logf = logf_pad[:, :n_heads]
    return (y_p.reshape(bsz, seq, d),
            y_s.reshape(n_dec, t_dec, d),
            k_p.transpose(0, 2, 1, 3),
            v_p.transpose(0, 2, 1, 3),
            logf[:tp].reshape(bsz, seq, n_heads),
            jnp.stack(new_conv_p),
            jnp.stack(mem_k_p).transpose(0, 1, 3, 2, 4),
            jnp.stack(mem_v_p).transpose(0, 1, 3, 2, 4),
            kv_s[:, :fox_w].reshape(n_dec, t_dec, n_heads, HEAD_DIM),
            kv_s[:, fox_w:].reshape(n_dec, t_dec, n_heads, HEAD_DIM),
            logf[tp:].reshape(n_dec, t_dec, n_heads),
            jnp.stack(new_conv_s))
```

```python
import functools
import math

import jax
import jax.numpy as jnp
from jax import lax
from jax.experimental import pallas as pl
from jax.experimental.pallas import tpu as pltpu

HEAD_DIM = 128
MEM_HEADS = 4
MEM_WIDTH = MEM_HEADS * HEAD_DIM
TOP_K = 2
EPS = 1e-6
NEG_INF = -1e30
ATTN_SCALE = HEAD_DIM ** -0.5
LOG2_E = math.log2(math.e)
LANES = 128
SUBLANES = 8
HEAD_PAD = 16
Q_ROWS = 8
PAGE_RING = 3
LF_PAGE_GROUP = 8
SAMPLE_Q_PAD = 16
CONV_HALO = 32
VMEM_LIMIT_BYTES = 56 * 1024 * 1024
MOE_SUB = 256
MOE_ROW_TILE = 3 * MOE_SUB
MOE_FFN_SUB = 128
MOE_UNSELECTED = -1.0e9

BF16 = jnp.bfloat16
F32 = jnp.float32


def _cparams(*semantics):
    return pltpu.CompilerParams(dimension_semantics=semantics, vmem_limit_bytes=VMEM_LIMIT_BYTES)


def _pick(n, candidates):
    for c in candidates:
        if c <= n and n % c == 0:
            return c
    return n


def _row_tile(t):
    return _pick(t, (1040, 1024, 640, 512, 256, 128, 64, 32, 16))


def _dot(a, b):
    return jnp.dot(a, b, preferred_element_type=F32)


def _dot_nt(a, b):
    return lax.dot_general(a, b, (((1,), (1,)), ((), ())), preferred_element_type=F32)


def _dot_exact(a, b):
    return jnp.dot(a, b, preferred_element_type=F32, precision=lax.Precision.HIGHEST)


def _log_sigmoid(x):
    return jnp.minimum(x, 0.0) - jnp.log1p(jnp.exp(-jnp.abs(x)))


def _rmsnorm_kernel(x_ref, g_ref, o_ref):
    x = x_ref[...]
    y = x * lax.rsqrt(jnp.mean(x * x, axis=-1, keepdims=True) + EPS)
    o_ref[...] = (y * g_ref[...]).astype(o_ref.dtype)


def _rmsnorm(x, g, out_dtype, row_start=0, n_rows=None):
    d = x.shape[1]
    n_rows = x.shape[0] if n_rows is None else n_rows
    tm = _pick(math.gcd(n_rows, row_start), (1024, 640, 512, 256, 128, 64, 32, 16))
    first = row_start // tm
    return pl.pallas_call(
        _rmsnorm_kernel,
        grid=(n_rows // tm,),
        in_specs=[pl.BlockSpec((tm, d), lambda i: (first + i, 0)),
                  pl.BlockSpec((1, d), lambda i: (0, 0))],
        out_specs=pl.BlockSpec((tm, d), lambda i: (i, 0)),
        out_shape=jax.ShapeDtypeStruct((n_rows, d), out_dtype),
        compiler_params=_cparams("parallel"),
        name="rmsnorm",
    )(x, g.reshape(1, d))


def _rmsnorm_pair_kernel(x_ref, g1_ref, g2_ref, o1_ref, o2_ref):
    x = x_ref[...]
    y = x * lax.rsqrt(jnp.mean(x * x, axis=-1, keepdims=True) + EPS)
    o1_ref[...] = (y * g1_ref[...]).astype(o1_ref.dtype)
    o2_ref[...] = (y * g2_ref[...]).astype(o2_ref.dtype)


def _rmsnorm_pair(x, g1, g2):
    t, d = x.shape
    tm = _pick(t, (640, 512, 256, 128, 64, 32, 16))
    rows = pl.BlockSpec((tm, d), lambda i: (i, 0))
    gain = pl.BlockSpec((1, d), lambda i: (0, 0))
    return pl.pallas_call(
        _rmsnorm_pair_kernel,
        grid=(t // tm,),
        in_specs=[rows, gain, gain],
        out_specs=[rows, rows],
        out_shape=[jax.ShapeDtypeStruct((t, d), BF16)] * 2,
        compiler_params=_cparams("parallel"),
        name="rmsnorm_pair",
    )(x, g1.reshape(1, d), g2.reshape(1, d))


def _mm_kernel(*refs, n_a, epilogue):
    a_refs = refs[:n_a]
    w_refs = refs[n_a:2 * n_a]
    pos = 2 * n_a
    extra_ref = None
    if epilogue in ("resid", "logsig"):
        extra_ref = refs[pos]
        pos += 1
    o_ref = refs[pos]
    wb_refs = refs[pos + 1:pos + 1 + n_a]

    @pl.when(pl.program_id(1) == 0)
    def _():
        for w_ref, wb_ref in zip(w_refs, wb_refs):
            wb_ref[...] = w_ref[...].astype(BF16)

    acc = None
    for a_ref, wb_ref in zip(a_refs, wb_refs):
        d = _dot(a_ref[...], wb_ref[...])
        acc = d if acc is None else acc + d
    if epilogue == "resid":
        acc = extra_ref[...] + acc
    elif epilogue == "logsig":
        acc = _log_sigmoid(acc + extra_ref[...])
    if len(o_ref.shape) == 4:
        for j in range(o_ref.shape[1]):
            o_ref[0, j] = acc[:, j * HEAD_DIM:(j + 1) * HEAD_DIM].astype(o_ref.dtype)
    else:
        o_ref[...] = acc.astype(o_ref.dtype)


def _matmul(a_list, w, row_offs, col_off, n_out, *, epilogue="none", extra=None, out_dtype=F32,
            row_start=0, n_rows=None, heads_of=None):
    n_rows = a_list[0].shape[0] if n_rows is None else n_rows
    align = math.gcd(n_rows, row_start) if heads_of is None else math.gcd(heads_of[1], row_start)
    tm = _row_tile(align)
    tn = _pick(n_out, (1024, 768, 512, 256, 128))
    first = row_start // tm
    n_a = len(a_list)
    in_specs, scratch = [], []
    for a in a_list:
        in_specs.append(pl.BlockSpec((tm, a.shape[1]), lambda n, m: (first + m, 0)))
    for a, ro in zip(a_list, row_offs):
        k = a.shape[1]
        assert ro % k == 0 and col_off % tn == 0
        in_specs.append(pl.BlockSpec((k, tn), functools.partial(
            lambda n, m, rb, cb: (rb, cb + n), rb=ro // k, cb=col_off // tn)))
        scratch.append(pltpu.VMEM((k, tn), BF16))
    args = list(a_list) + [w] * n_a
    if epilogue == "resid":
        in_specs.append(pl.BlockSpec((tm, tn), lambda n, m: (first + m, n)))
        args.append(extra)
    elif epilogue == "logsig":
        in_specs.append(pl.BlockSpec((1, tn), lambda n, m: (0, n)))
        args.append(extra)
    if heads_of is None:
        out_spec = pl.BlockSpec((tm, tn), lambda n, m: (m, n))
        out_shape = jax.ShapeDtypeStruct((n_rows, n_out), out_dtype)
    else:
        bsz, seq = heads_of
        per_seq = seq // tm
        out_spec = pl.BlockSpec((1, tn // HEAD_DIM, tm, HEAD_DIM),
                                lambda n, m: (m // per_seq, n, m % per_seq, 0))
        out_shape = jax.ShapeDtypeStruct((bsz, n_out // HEAD_DIM, seq, HEAD_DIM), out_dtype)
    return pl.pallas_call(
        functools.partial(_mm_kernel, n_a=n_a, epilogue=epilogue),
        grid=(n_out // tn, n_rows // tm),
        in_specs=in_specs,
        out_specs=out_spec,
        out_shape=out_shape,
        scratch_shapes=scratch,
        compiler_params=_cparams("arbitrary", "arbitrary"),
        name="matmul_" + epilogue,
    )(*args)


def _glu_kernel(a_ref, w1_ref, w2_ref, o_ref, wb1_ref, wb2_ref):
    @pl.when(pl.program_id(1) == 0)
    def _():
        wb1_ref[...] = w1_ref[...].astype(BF16)
        wb2_ref[...] = w2_ref[...].astype(BF16)

    a = a_ref[...]
    o_ref[...] = _dot(a, wb1_ref[...]) * jax.nn.sigmoid(_dot(a, wb2_ref[...]))


def _matmul_glu(a, w, n_out):
    t, k = a.shape
    tm = _row_tile(t)
    tn = _pick(n_out, (1024, 768, 512, 256, 128))
    nb = n_out // tn
    return pl.pallas_call(
        _glu_kernel,
        grid=(nb, t // tm),
        in_specs=[pl.BlockSpec((tm, k), lambda n, m: (m, 0)),
                  pl.BlockSpec((k, tn), lambda n, m: (0, n)),
                  pl.BlockSpec((k, tn), lambda n, m: (0, n + nb))],
        out_specs=pl.BlockSpec((tm, tn), lambda n, m: (m, n)),
        out_shape=jax.ShapeDtypeStruct((t, n_out), F32),
        scratch_shapes=[pltpu.VMEM((k, tn), BF16), pltpu.VMEM((k, tn), BF16)],
        compiler_params=_cparams("arbitrary", "arbitrary"),
        name="matmul_glu",
    )(a, w, w)


def _swiglu_partial(x, wg, wu, wd):
    g = _dot(x, wg)
    u = _dot(x, wu)
    return _dot((g * jax.nn.sigmoid(g) * u).astype(BF16), wd)


def _ffn_kernel(h_ref, g_ref, wg_ref, wu_ref, wd_ref, o_ref, x_ref):
    @pl.when(pl.program_id(1) == 0)
    def _():
        h = h_ref[...]
        o_ref[...] = h
        y = h * lax.rsqrt(jnp.mean(h * h, axis=-1, keepdims=True) + EPS)
        x_ref[...] = (y * g_ref[...]).astype(BF16)

    o_ref[...] += _swiglu_partial(x_ref[...], wg_ref[...].astype(BF16), wu_ref[...].astype(BF16),
                                  wd_ref[...].astype(BF16))


def _ffn(h, g, w_gate, w_up, w_down):
    t, d = h.shape
    f = w_gate.shape[1]
    tm = _row_tile(t)
    tf = _pick(f, (512, 256, 128))
    return pl.pallas_call(
        _ffn_kernel,
        grid=(t // tm, f // tf),
        in_specs=[pl.BlockSpec((tm, d), lambda m, j: (m, 0), pipeline_mode=pl.Buffered(1)),
                  pl.BlockSpec((1, d), lambda m, j: (0, 0)),
                  pl.BlockSpec((d, tf), lambda m, j: (0, j)),
                  pl.BlockSpec((d, tf), lambda m, j: (0, j)),
                  pl.BlockSpec((tf, d), lambda m, j: (j, 0))],
        out_specs=pl.BlockSpec((tm, d), lambda m, j: (m, 0), pipeline_mode=pl.Buffered(1)),
        out_shape=jax.ShapeDtypeStruct((t, d), F32),
        scratch_shapes=[pltpu.VMEM((tm, d), BF16)],
        compiler_params=_cparams("parallel", "arbitrary"),
        name="ffn",
    )(h, g.reshape(1, d), w_gate, w_up, w_down)


def _router_kernel(x_ref, w_ref, b_ref, o_ref, *, n_experts):
    logits = _dot(x_ref[...], w_ref[...].astype(BF16)) + b_ref[...]
    lane = lax.broadcasted_iota(jnp.int32, logits.shape, 1)
    logits = jnp.where(lane < n_experts, logits, NEG_INF)
    v1 = jnp.max(logits, axis=-1, keepdims=True)
    i1 = jnp.min(jnp.where(logits == v1, lane, LANES), axis=-1, keepdims=True)
    rest = jnp.where(lane == i1, NEG_INF, logits)
    v2 = jnp.max(rest, axis=-1, keepdims=True)
    i2 = jnp.min(jnp.where(rest == v2, lane, LANES), axis=-1, keepdims=True)
    e2 = jnp.exp(v2 - v1)
    p1 = 1.0 / (1.0 + e2)
    p2 = e2 / (1.0 + e2)
    o_ref[...] = jnp.where(lane == i1, p1, jnp.where(lane == i2, p2, 0.0))


def _router(x, w_router, b_router):
    t, d = x.shape
    n_e = w_router.shape[-1]
    tm = _row_tile(t)
    w_pad = jnp.pad(w_router, ((0, 0), (0, LANES - n_e)))
    b_pad = jnp.pad(b_router, (0, LANES - n_e)).reshape(1, LANES)
    return pl.pallas_call(
        functools.partial(_router_kernel, n_experts=n_e),
        grid=(t // tm,),
        in_specs=[pl.BlockSpec((tm, d), lambda m: (m, 0)),
                  pl.BlockSpec((d, LANES), lambda m: (0, 0)),
                  pl.BlockSpec((1, LANES), lambda m: (0, 0))],
        out_specs=pl.BlockSpec((tm, LANES), lambda m: (m, 0)),
        out_shape=jax.ShapeDtypeStruct((t, LANES), F32),
        compiler_params=_cparams("parallel"),
        name="router",
    )(x, w_pad, b_pad)


def _moe_rank_kernel(g_ref, pos_ref, pos_t_ref, start_ref, cnt_ref, carry_ref):
    i = pl.program_id(0)
    tm = g_ref.shape[0]

    @pl.when(i == 0)
    def _():
        carry_ref[...] = jnp.zeros(carry_ref.shape, F32)

    sel = g_ref[...] > 0.0
    r = lax.broadcasted_iota(jnp.int32, (tm, tm), 0)
    c = lax.broadcasted_iota(jnp.int32, (tm, tm), 1)
    before = jnp.where(c < r, 1.0, 0.0).astype(BF16)
    ones = jnp.where(sel, 1.0, 0.0)
    carry = carry_ref[...]
    rank = _dot(before, ones.astype(BF16)) + carry
    start_ref[...] = jnp.broadcast_to(carry, start_ref.shape)
    carry = rank[tm - 1:tm, :] + ones[tm - 1:tm, :]
    carry_ref[...] = carry
    cnt_ref[...] = jnp.broadcast_to(carry, cnt_ref.shape)
    pos = jnp.where(sel, rank, MOE_UNSELECTED)
    pos_ref[...] = pos
    pos_t_ref[...] = pos.T[0:HEAD_PAD, :]


def _moe_rank(gates, tc):
    t = gates.shape[0]
    nch = t // tc
    return pl.pallas_call(
        _moe_rank_kernel,
        grid=(nch,),
        in_specs=[pl.BlockSpec((tc, LANES), lambda i: (i, 0))],
        out_specs=[pl.BlockSpec((tc, LANES), lambda i: (i, 0)),
                   pl.BlockSpec((HEAD_PAD, tc), lambda i: (0, i)),
                   pl.BlockSpec((8, LANES), lambda i: (i, 0)),
                   pl.BlockSpec((8, LANES), lambda i: (0, 0))],
        out_shape=[jax.ShapeDtypeStruct((t, LANES), F32),
                   jax.ShapeDtypeStruct((HEAD_PAD, t), F32),
                   jax.ShapeDtypeStruct((nch * 8, LANES), F32),
                   jax.ShapeDtypeStruct((8, LANES), F32)],
        scratch_shapes=[pltpu.VMEM((1, LANES), F32)],
        compiler_params=_cparams("arbitrary"),
        name="moe_rank",
    )(gates)


def _moe_gather_kernel(sb_ref, ch_ref, first_ref, last_ref, valid_ref, esb_ref, base_ref,
                       pos_t_ref, x_ref, o_ref, acc_ref):
    i = pl.program_id(0)

    @pl.when(valid_ref[i] == 1)
    def _():
        sb = sb_ref[i]
        local = pos_t_ref[pl.ds(esb_ref[sb], 1), :].astype(jnp.int32) + base_ref[sb]
        rows = lax.broadcasted_iota(jnp.int32, (o_ref.shape[0], local.shape[1]), 0)
        onehot = jnp.where(local == rows, 1.0, 0.0).astype(BF16)
        part = _dot(onehot, x_ref[...])

        @pl.when(first_ref[i] == 1)
        def _():
            acc_ref[...] = part

        @pl.when(first_ref[i] == 0)
        def _():
            acc_ref[...] += part

        @pl.when(last_ref[i] == 1)
        def _():
            o_ref[...] = acc_ref[...].astype(o_ref.dtype)


def _moe_gather(plan, pos_t, x, n_rows, tc):
    d = x.shape[1]
    sb = MOE_SUB
    prefetch = (plan["g_sb"], plan["g_ch"], plan["g_first"], plan["g_last"], plan["g_valid"],
                plan["e_sb"], plan["base_sb"])
    grid_spec = pltpu.PrefetchScalarGridSpec(
        num_scalar_prefetch=len(prefetch),
        grid=(plan["g_sb"].shape[0],),
        in_specs=[pl.BlockSpec((HEAD_PAD, tc), lambda i, sbl, chl, *_: (0, chl[i])),
                  pl.BlockSpec((tc, d), lambda i, sbl, chl, *_: (chl[i], 0))],
        out_specs=pl.BlockSpec((sb, d), lambda i, sbl, chl, *_: (sbl[i], 0)),
        scratch_shapes=[pltpu.VMEM((sb, d), F32)],
    )
    return pl.pallas_call(
        _moe_gather_kernel,
        grid_spec=grid_spec,
        out_shape=jax.ShapeDtypeStruct((n_rows, d), BF16),
        compiler_params=_cparams("arbitrary"),
        name="moe_gather",
    )(*prefetch, pos_t, x)


def _moe_ffn_kernel(et_ref, nv_ref, x_ref, wg_ref, wu_ref, wd_ref, o_ref, acc_ref):
    del et_ref
    r = pl.program_id(0)
    f = pl.program_id(1)
    last_f = pl.num_programs(1) - 1
    n_valid = nv_ref[r]
    n_sub = o_ref.shape[0] // MOE_FFN_SUB

    @pl.when(f == 0)
    def _():
        acc_ref[...] = jnp.zeros(acc_ref.shape, F32)

    for n in range(1, n_sub + 1):
        @pl.when(n_valid == n)
        def _():
            rows = slice(0, n * MOE_FFN_SUB)
            acc_ref[rows, :] += _swiglu_partial(x_ref[rows, :], wg_ref[0].astype(BF16), wu_ref[0].astype(BF16),
                                                wd_ref[0].astype(BF16))

    @pl.when(f == last_f)
    def _():
        o_ref[...] = acc_ref[...].astype(o_ref.dtype)


def _moe_ffn(plan, xs, w_gate, w_up, w_down):
    n_rows, d = xs.shape
    f = w_gate.shape[2]
    tr = MOE_ROW_TILE
    tf = _pick(f, (512, 256, 128))
    nf = f // tf

    def w_in_map(r, j, et, nv):
        return (et[r], 0, jnp.where(nv[r] > 0, j, nf - 1))

    def w_out_map(r, j, et, nv):
        return (et[r], jnp.where(nv[r] > 0, j, nf - 1), 0)

    grid_spec = pltpu.PrefetchScalarGridSpec(
        num_scalar_prefetch=2,
        grid=(n_rows // tr, nf),
        in_specs=[pl.BlockSpec((tr, d), lambda r, j, et, nv: (r, 0)),
                  pl.BlockSpec((1, d, tf), w_in_map),
                  pl.BlockSpec((1, d, tf), w_in_map),
                  pl.BlockSpec((1, tf, d), w_out_map)],
        out_specs=pl.BlockSpec((tr, d), lambda r, j, et, nv: (r, 0)),
        scratch_shapes=[pltpu.VMEM((tr, d), F32)],
    )
    return pl.pallas_call(
        _moe_ffn_kernel,
        grid_spec=grid_spec,
        out_shape=jax.ShapeDtypeStruct((n_rows, d), BF16),
        compiler_params=_cparams("arbitrary", "arbitrary"),
        name="moe_ffn",
    )(plan["e_tile"], plan["nv_tile"], xs, w_gate, w_up, w_down)


def _moe_combine_kernel(ch_ref, sb_ref, first_ref, valid_ref, esb_ref, base_ref,
                        pos_ref, gate_ref, h_ref, y_ref, o_ref):
    i = pl.program_id(0)

    @pl.when(first_ref[i] == 1)
    def _():
        o_ref[...] = h_ref[...]

    @pl.when(valid_ref[i] == 1)
    def _():
        sb = sb_ref[i]
        lane = lax.broadcasted_iota(jnp.int32, pos_ref.shape, 1)
        mine = lane == esb_ref[sb]
        rank = jnp.sum(jnp.where(mine, pos_ref[...], 0.0), axis=-1, keepdims=True)
        local = rank.astype(jnp.int32) + base_ref[sb]
        gate = jnp.sum(jnp.where(mine, gate_ref[...], 0.0), axis=-1, keepdims=True)
        cols = lax.broadcasted_iota(jnp.int32, (pos_ref.shape[0], y_ref.shape[0]), 1)
        onehot = jnp.where(local == cols, 1.0, 0.0).astype(BF16)
        o_ref[...] += gate * _dot(onehot, y_ref[...])


def _moe_combine(plan, pos, gates, h, ys, tc):
    t, d = h.shape
    sb = MOE_SUB
    prefetch = (plan["c_ch"], plan["c_sb"], plan["c_first"], plan["c_valid"], plan["e_sb"], plan["base_sb"])
    grid_spec = pltpu.PrefetchScalarGridSpec(
        num_scalar_prefetch=len(prefetch),
        grid=(plan["c_ch"].shape[0],),
        in_specs=[pl.BlockSpec((tc, LANES), lambda i, chl, sbl, *_: (chl[i], 0)),
                  pl.BlockSpec((tc, LANES), lambda i, chl, sbl, *_: (chl[i], 0)),
                  pl.BlockSpec((tc, d), lambda i, chl, sbl, *_: (chl[i], 0)),
                  pl.BlockSpec((sb, d), lambda i, chl, sbl, *_: (sbl[i], 0))],
        out_specs=pl.BlockSpec((tc, d), lambda i, chl, sbl, *_: (chl[i], 0)),
    )
    return pl.pallas_call(
        _moe_combine_kernel,
        grid_spec=grid_spec,
        out_shape=jax.ShapeDtypeStruct((t, d), F32),
        compiler_params=_cparams("arbitrary"),
        name="moe_combine",
    )(*prefetch, pos, gates, h, ys)


def _pair_list(overlap, length):
    n_b = overlap.shape[1]
    flat = overlap.reshape(-1)
    n = jnp.sum(flat.astype(jnp.int32))
    idx = jnp.nonzero(flat, size=length, fill_value=0)[0].astype(jnp.int32)
    step = jnp.arange(length, dtype=jnp.int32)
    valid = step < n
    idx = jnp.where(valid, idx, idx[jnp.maximum(n - 1, 0)])
    a = idx // n_b
    b = idx % n_b
    nxt_valid = jnp.concatenate([valid[1:], jnp.zeros((1,), bool)])
    first = valid & (a != jnp.concatenate([jnp.full((1,), -1, jnp.int32), a[:-1]]))
    last = valid & ((a != jnp.concatenate([a[1:], jnp.full((1,), -1, jnp.int32)])) | ~nxt_valid)
    as_i32 = lambda v: v.astype(jnp.int32)
    return a, b, as_i32(first), as_i32(last), as_i32(valid)


def _moe_plan(counts, chunk_start, n_rows, nch):
    n_e = counts.shape[0]
    tr, sb = MOE_ROW_TILE, MOE_SUB
    seg = (counts + tr - 1) // tr * tr
    seg_end = jnp.cumsum(seg)
    off = seg_end - seg
    chunk_end = jnp.concatenate([chunk_start[:, 1:], counts[:, None]], axis=1)

    def owner(start):
        return jnp.minimum(jnp.sum(seg_end[None, :] <= start[:, None], axis=1), n_e - 1).astype(jnp.int32)

    sb_start = jnp.arange(n_rows // sb, dtype=jnp.int32) * sb
    e_sb = owner(sb_start)
    k0 = sb_start - off[e_sb]
    k1 = jnp.minimum(k0 + sb, counts[e_sb])
    sb_used = (sb_start < seg_end[-1]) & (k0 < counts[e_sb])
    overlap = sb_used[:, None] & (chunk_start[e_sb] < k1[:, None]) & (chunk_end[e_sb] > k0[:, None])
    length = n_rows // sb + n_e * nch
    g_sb, g_ch, g_first, g_last, g_valid = _pair_list(overlap, length)
    c_ch, c_sb, c_first, _, c_valid = _pair_list(overlap.T, length)

    tile_start = jnp.arange(n_rows // tr, dtype=jnp.int32) * tr
    e_tile = owner(tile_start)
    left = counts[e_tile] - (tile_start - off[e_tile])
    step = MOE_FFN_SUB
    nv_tile = jnp.where(tile_start < seg_end[-1], jnp.clip((left + step - 1) // step, 0, tr // step), 0)
    return dict(g_sb=g_sb, g_ch=g_ch, g_first=g_first, g_last=g_last, g_valid=g_valid,
                c_ch=c_ch, c_sb=c_sb, c_first=c_first, c_valid=c_valid,
                e_sb=e_sb, base_sb=(off[e_sb] - sb_start).astype(jnp.int32),
                e_tile=e_tile, nv_tile=nv_tile.astype(jnp.int32))


def _moe(x, h, w_router, b_router, w_gate, w_up, w_down):
    t = x.shape[0]
    n_e = w_router.shape[-1]
    tc = _pick(t, (640, 512, 384, 256, 128))
    assert tc % LANES == 0 and t % tc == 0 and n_e <= 8
    nch = t // tc
    n_rows = -(-(TOP_K * t + n_e * (MOE_ROW_TILE - 1)) // MOE_ROW_TILE) * MOE_ROW_TILE
    gates = _router(x, w_router, b_router)
    pos, pos_t, starts, totals = _moe_rank(gates, tc)
    counts = totals[0, :n_e].astype(jnp.int32)
    chunk_start = starts.reshape(nch, 8, LANES)[:, 0, :n_e].T.astype(jnp.int32)
    plan = _moe_plan(counts, chunk_start, n_rows, nch)
    xs = _moe_gather(plan, pos_t, x, n_rows, tc)
    ys = _moe_ffn(plan, xs, w_gate, w_up, w_down)
    return _moe_combine(plan, pos, gates, h, ys, tc)


def _ln_swish(hc, lg, lb):
    mu = jnp.mean(hc, axis=-1, keepdims=True)
    xc = hc - mu
    var = jnp.mean(xc * xc, axis=-1, keepdims=True)
    y = xc * lax.rsqrt(var + EPS) * lg + lb
    return y * jax.nn.sigmoid(y)


def _conv_kernel(prev_ref, main_ref, w_ref, b_ref, lg_ref, lb_ref, o_ref, ext_ref, acc_ref, sh_ref, *, tt, width):
    @pl.when(pl.program_id(1) == 0)
    def _():
        ext_ref[0:CONV_HALO, :] = jnp.zeros((CONV_HALO, ext_ref.shape[1]), F32)

    @pl.when(pl.program_id(1) > 0)
    def _():
        ext_ref[0:CONV_HALO, :] = prev_ref[...]

    ext_ref[CONV_HALO:CONV_HALO + tt, :] = main_ref[...]
    n_chunks = o_ref.shape[-1] // LANES
    lead = CONV_HALO - (width - 1)

    span = tt + CONV_HALO - SUBLANES

    def chunk(c, carry):
        c0 = pl.multiple_of(c * LANES, LANES)
        for r in range(1, SUBLANES):
            sh_ref[r - 1] = ext_ref[pl.ds(r, span), pl.ds(c0, LANES)]
        acc = jnp.zeros((tt, LANES), F32)
        for j in range(width):
            phase = (lead + j) % SUBLANES
            base = lead + j - phase
            if phase == 0:
                rows = ext_ref[pl.ds(base, tt), pl.ds(c0, LANES)]
            else:
                rows = sh_ref[phase - 1, pl.ds(base, tt), :]
            acc = acc + rows * w_ref[pl.ds(j, 1), pl.ds(c0, LANES)]
        acc_ref[:, pl.ds(c0, LANES)] = acc + b_ref[:, pl.ds(c0, LANES)]
        return carry

    lax.fori_loop(0, n_chunks, chunk, 0)
    o_ref[...] = _ln_swish(acc_ref[...], lg_ref[...], lb_ref[...]).astype(o_ref.dtype)


def _conv_prompt(u, w, b, lg, lb, bsz, seq):
    c = u.shape[1]
    width = w.shape[0]
    tt = _pick(seq, (256, 128, 64, 32))
    nt = seq // tt
    per_halo = tt // CONV_HALO
    return pl.pallas_call(
        functools.partial(_conv_kernel, tt=tt, width=width),
        grid=(bsz, nt),
        in_specs=[pl.BlockSpec((CONV_HALO, c), lambda bb, i: (jnp.maximum((bb * nt + i) * per_halo - 1, 0), 0)),
                  pl.BlockSpec((tt, c), lambda bb, i: (bb * nt + i, 0)),
                  pl.BlockSpec((width, c), lambda bb, i: (0, 0)),
                  pl.BlockSpec((1, c), lambda bb, i: (0, 0)),
                  pl.BlockSpec((1, c), lambda bb, i: (0, 0)),
                  pl.BlockSpec((1, c), lambda bb, i: (0, 0))],
        out_specs=pl.BlockSpec((tt, c), lambda bb, i: (bb * nt + i, 0)),
        out_shape=jax.ShapeDtypeStruct((u.shape[0], c), BF16),
        scratch_shapes=[pltpu.VMEM((tt + CONV_HALO, c), F32), pltpu.VMEM((tt, c), F32),
                        pltpu.VMEM((SUBLANES - 1, tt + CONV_HALO - SUBLANES, LANES), F32)],
        compiler_params=_cparams("parallel", "parallel"),
        name="conv_prompt",
    )(u, u, w, b.reshape(1, c), lg.reshape(1, c), lb.reshape(1, c))


def _conv_step_kernel(ext_ref, w_ref, b_ref, lg_ref, lb_ref, o_ref, *, width):
    n_t = o_ref.shape[0]
    for t in range(n_t):
        acc = ext_ref[t] * w_ref[0:1, :]
        for j in range(1, width):
            acc = acc + ext_ref[t + j] * w_ref[j:j + 1, :]
        o_ref[t] = _ln_swish(acc + b_ref[...], lg_ref[...], lb_ref[...]).astype(o_ref.dtype)


def _conv_sample(ext_t, w, b, lg, lb):
    l_ext, n, c = ext_t.shape
    width = w.shape[0]
    n_t = l_ext - (width - 1)
    return pl.pallas_call(
        functools.partial(_conv_step_kernel, width=width),
        out_shape=jax.ShapeDtypeStruct((n_t, n, c), BF16),
        compiler_params=pltpu.CompilerParams(vmem_limit_bytes=VMEM_LIMIT_BYTES),
        name="conv_sample",
    )(ext_t, w, b.reshape(1, c), lg.reshape(1, c), lb.reshape(1, c))


def _softmax_attend(q, k, v):
    s = _dot_nt(q, k) * ATTN_SCALE
    p = jnp.exp(s - jnp.max(s, axis=-1, keepdims=True))
    l = jnp.sum(p, axis=-1, keepdims=True)
    return _dot(p.astype(BF16), v) / l


def _mem_attn_kernel(q_ref, k_ref, v_ref, o_ref):
    for hh in range(MEM_HEADS):
        sl = slice(hh * HEAD_DIM, (hh + 1) * HEAD_DIM)
        o_ref[:, sl] = _softmax_attend(q_ref[:, sl].astype(BF16), k_ref[0, hh].astype(BF16),
                                       v_ref[0, hh].astype(BF16)).astype(o_ref.dtype)


def _mem_attn_cache_kernel(q_ref, k_ref, v_ref, o_ref, *, rows_per_req):
    n_mem = k_ref.shape[2] // MEM_HEADS
    for r in range(k_ref.shape[1]):
        rows = slice(r * rows_per_req, (r + 1) * rows_per_req)
        for hh in range(MEM_HEADS):
            sl = slice(hh * HEAD_DIM, (hh + 1) * HEAD_DIM)
            head_rows = pl.ds(hh, n_mem, stride=MEM_HEADS)
            o_ref[rows, sl] = _softmax_attend(q_ref[rows, sl].astype(BF16),
                                              k_ref[0, r, head_rows, :].astype(BF16),
                                              v_ref[0, r, head_rows, :].astype(BF16)).astype(o_ref.dtype)


def _mem_attn_cache(q, k_cache, v_cache, layer, rows_per_req):
    n_layers, n, m, _, _ = k_cache.shape
    w = q.shape[1]
    group = _pick(n, (4, 2, 1))
    flat = (n_layers, n, m * MEM_HEADS, HEAD_DIM)
    kv_spec = pl.BlockSpec((1, group, m * MEM_HEADS, HEAD_DIM), lambda j: (layer, j, 0, 0))
    return pl.pallas_call(
        functools.partial(_mem_attn_cache_kernel, rows_per_req=rows_per_req),
        grid=(n // group,),
        in_specs=[pl.BlockSpec((group * rows_per_req, w), lambda j: (j, 0)), kv_spec, kv_spec],
        out_specs=pl.BlockSpec((group * rows_per_req, w), lambda j: (j, 0)),
        out_shape=jax.ShapeDtypeStruct((n * rows_per_req, w), BF16),
        compiler_params=_cparams("parallel"),
        name="mem_attn_cache",
    )(q, k_cache.reshape(flat), v_cache.reshape(flat))


def _mem_attn(q, k, v, rows_per_req, col_off=0):
    w = MEM_WIDTH
    n, _, m, _ = k.shape
    tq = _pick(rows_per_req, (512, 256, 128, 64, 32, 16))
    per_req = rows_per_req // tq
    col_block = col_off // w
    assert col_off % w == 0
    return pl.pallas_call(
        _mem_attn_kernel,
        grid=(n, per_req),
        in_specs=[pl.BlockSpec((tq, w), lambda i, j: (i * per_req + j, col_block)),
                  pl.BlockSpec((1, MEM_HEADS, m, HEAD_DIM), lambda i, j: (i, 0, 0, 0)),
                  pl.BlockSpec((1, MEM_HEADS, m, HEAD_DIM), lambda i, j: (i, 0, 0, 0))],
        out_specs=pl.BlockSpec((tq, w), lambda i, j: (i * per_req + j, 0)),
        out_shape=jax.ShapeDtypeStruct((max(q.shape[0], n * rows_per_req), w), BF16),
        compiler_params=_cparams("parallel", "parallel"),
        name="mem_attn",
    )(q, k, v)


def _cumsum_kernel(lf_ref, crep_ref, crow_ref, *, blk, n_heads):
    seq = lf_ref.shape[1]
    r = lax.broadcasted_iota(jnp.int32, (blk, blk), 0)
    c = lax.broadcasted_iota(jnp.int32, (blk, blk), 1)
    tri = (c <= r).astype(F32)
    carry = jnp.zeros((1, LANES), F32)
    for i in range(seq // blk):
        rows = slice(i * blk, (i + 1) * blk)
        cs = _dot_exact(tri, lf_ref[0, rows, :]) + carry
        carry = cs[blk - 1:blk, :]
        cs2 = cs * LOG2_E
        crow_ref[0, :, rows] = cs2.T[0:HEAD_PAD, :]
        for hh in range(n_heads):
            crep_ref[0, hh, rows, :] = jnp.broadcast_to(cs2[:, hh:hh + 1], (blk, LANES))


def _fox_cumsum(logf_pad, n_heads):
    bsz, seq, _ = logf_pad.shape
    blk = _pick(seq, (256, 128))
    return pl.pallas_call(
        functools.partial(_cumsum_kernel, blk=blk, n_heads=n_heads),
        grid=(bsz,),
        in_specs=[pl.BlockSpec((1, seq, LANES), lambda b: (b, 0, 0))],
        out_specs=[pl.BlockSpec((1, n_heads, seq, LANES), lambda b: (b, 0, 0, 0)),
                   pl.BlockSpec((1, HEAD_PAD, seq), lambda b: (b, 0, 0))],
        out_shape=[jax.ShapeDtypeStruct((bsz, n_heads, seq, LANES), F32),
                   jax.ShapeDtypeStruct((bsz, HEAD_PAD, seq), F32)],
        compiler_params=_cparams("parallel"),
        name="fox_cumsum",
    )(logf_pad)


def _fox_prompt_kernel(q_ref, k_ref, v_ref, crep_ref, crow_ref, o_ref,
                       kb_ref, vt_ref, m_ref, l_ref, acc_ref, *, tq, tk):
    hh = pl.program_id(1)
    qi = pl.program_id(2)

    @pl.when(qi == 0)
    def _():
        kb_ref[...] = k_ref[0, 0].astype(BF16)
        vt_ref[...] = v_ref[0, 0].T.astype(BF16)

    q = q_ref[...].astype(BF16)
    q0 = pl.multiple_of(qi * tq, tq)
    cq = crow_ref[0, pl.ds(hh, 1), pl.ds(q0, tq)]
    m_ref[...] = jnp.full(m_ref.shape, NEG_INF, F32)
    l_ref[...] = jnp.zeros(l_ref.shape, F32)
    acc_ref[...] = jnp.zeros(acc_ref.shape, F32)

    def update(k0, on_diagonal):
        ck = jnp.tile(crep_ref[0, 0, pl.ds(k0, tk), :], (1, tq // LANES))
        z = _dot_nt(kb_ref[pl.ds(k0, tk), :], q) * (ATTN_SCALE * LOG2_E) - ck
        if on_diagonal:
            z = jnp.where(lax.broadcasted_iota(jnp.int32, (tk, tq), 0) <= lax.broadcasted_iota(jnp.int32, (tk, tq), 1),
                          z, NEG_INF)
        m_old = m_ref[...]
        m_new = jnp.maximum(m_old, jnp.max(z, axis=0, keepdims=True) + cq)
        alpha = jnp.exp2(m_old - m_new)
        p = jnp.exp2(z - (m_new - cq))
        l_ref[...] = alpha * l_ref[...] + jnp.sum(p, axis=0, keepdims=True)
        acc_ref[...] = alpha * acc_ref[...] + _dot(vt_ref[:, pl.ds(k0, tk)], p.astype(BF16))
        m_ref[...] = m_new

    def body(kj, carry):
        update(pl.multiple_of(kj * tk, tk), False)
        return carry

    lax.fori_loop(0, qi, body, 0)
    update(q0, True)
    o_ref[...] = (acc_ref[...] / l_ref[...]).T.astype(o_ref.dtype)


def _fox_prompt(q, k, v, crep, crow):
    bsz, n_heads, seq, _ = k.shape
    tq = _pick(seq, (512, 256, 128))
    tk = tq
    nq = seq // tq
    head_seq = pl.BlockSpec((1, 1, seq, HEAD_DIM), lambda b, h, i: (b, h, 0, 0))
    return pl.pallas_call(
        functools.partial(_fox_prompt_kernel, tq=tq, tk=tk),
        grid=(bsz, n_heads, nq),
        in_specs=[pl.BlockSpec((tq, HEAD_DIM), lambda b, h, i: (b * nq + i, h)),
                  head_seq,
                  head_seq,
                  pl.BlockSpec((1, 1, seq, LANES), lambda b, h, i: (b, h, 0, 0)),
                  pl.BlockSpec((1, HEAD_PAD, seq), lambda b, h, i: (b, 0, 0))],
        out_specs=pl.BlockSpec((tq, HEAD_DIM), lambda b, h, i: (b * nq + i, h)),
        out_shape=jax.ShapeDtypeStruct((q.shape[0], n_heads * HEAD_DIM), BF16),
        scratch_shapes=[pltpu.VMEM((seq, HEAD_DIM), BF16), pltpu.VMEM((HEAD_DIM, seq), BF16),
                        pltpu.VMEM((1, tq), F32), pltpu.VMEM((1, tq), F32),
                        pltpu.VMEM((HEAD_DIM, tq), F32)],
        compiler_params=_cparams("parallel", "parallel", "arbitrary"),
        name="fox_prompt",
    )(q, k, v, crep, crow)


def _fox_sample_kernel(*refs, pages_per_step, n_new, n_heads):
    pp = pages_per_step
    pt_ref = refs[0]
    q_ref, knew_ref, vnew_ref, lfnew_ref, lfcol_ref = refs[1:6]
    k_hbm, v_hbm = refs[6:8]
    lf_refs = refs[8:8 + pp]
    o_ref = refs[8 + pp]
    (m_ref, l_ref, acc_ref, carry_ref, cn_ref, knp_ref, vnp_ref,
     kbuf_ref, vbuf_ref, ksem_ref, vsem_ref) = refs[9 + pp:]
    req = pl.program_id(0)
    step = pl.program_id(1)
    n_steps = pl.num_programs(1)
    n_pages = n_steps * pp
    page = knp_ref.shape[2]
    n_rows = n_heads * Q_ROWS

    flat = req * n_steps + step
    total = pl.num_programs(0) * n_steps

    def page_copies(g):
        slot = g % PAGE_RING
        r_g = g // n_steps
        s_g = g % n_steps
        copies = []
        for j in range(pp):
            page_id = pt_ref[r_g, n_pages - 1 - (s_g * pp + j)]
            copies.append(pltpu.make_async_copy(k_hbm.at[page_id], kbuf_ref.at[slot, j], ksem_ref.at[slot]))
            copies.append(pltpu.make_async_copy(v_hbm.at[page_id], vbuf_ref.at[slot, j], vsem_ref.at[slot]))
        return copies

    @pl.when(flat == 0)
    def _():
        for g0 in range(PAGE_RING - 1):
            @pl.when(g0 < total)
            def _():
                for cp in page_copies(jnp.int32(g0)):
                    cp.start()

    @pl.when(flat + (PAGE_RING - 1) < total)
    def _():
        for cp in page_copies(flat + (PAGE_RING - 1)):
            cp.start()

    for cp in page_copies(flat):
        cp.wait()
    slot = flat % PAGE_RING

    r = lax.broadcasted_iota(jnp.int32, (page, page), 0)
    c = lax.broadcasted_iota(jnp.int32, (page, page), 1)

    @pl.when(step == 0)
    def _():
        m_ref[...] = jnp.full(m_ref.shape, NEG_INF, F32)
        l_ref[...] = jnp.zeros(l_ref.shape, F32)
        acc_ref[...] = jnp.zeros(acc_ref.shape, F32)
        carry_ref[...] = jnp.zeros(carry_ref.shape, F32)
        lf_col = lfcol_ref[0]
        tok = lax.broadcasted_iota(jnp.int32, lf_col.shape, 0)
        c_col = jnp.zeros(lf_col.shape, F32)
        for j in range(n_new):
            c_col = c_col + jnp.where(tok >= j, lf_col[j:j + 1, :], 0.0)
        for hh in range(n_heads):
            cn_ref[hh * Q_ROWS:(hh + 1) * Q_ROWS, :] = c_col[:, hh:hh + 1]

    def per_head_rows(x16):
        return jnp.concatenate([jnp.broadcast_to(x16[hh:hh + 1, :], (Q_ROWS, page)) for hh in range(n_heads)],
                               axis=0)

    def attend(k_list, v_list, bias, mask=None):
        s = jnp.concatenate(
            [jnp.concatenate([_dot_nt(q_ref[0, hh], k_of(hh).astype(BF16)) for hh in range(n_heads)], axis=0)
             for k_of in k_list], axis=1)
        s = s * ATTN_SCALE + bias
        if mask is not None:
            s = jnp.where(mask, s, NEG_INF)
        m_old = m_ref[...]
        m_new = jnp.maximum(m_old, jnp.max(s, axis=-1, keepdims=True))
        alpha = jnp.exp(m_old - m_new)
        p = jnp.exp(s - m_new)
        l_ref[...] = alpha * l_ref[...] + jnp.sum(p, axis=-1, keepdims=True)
        pv = None
        for j, v_of in enumerate(v_list):
            part = jnp.concatenate(
                [_dot(p[hh * Q_ROWS:(hh + 1) * Q_ROWS, j * page:(j + 1) * page].astype(BF16),
                      v_of(hh).astype(BF16)) for hh in range(n_heads)], axis=0)
            pv = part if pv is None else pv + part
        acc_ref[...] = alpha * acc_ref[...] + pv
        m_ref[...] = m_new

    suffix = (r > c).astype(F32)
    carry = carry_ref[...]
    decay = []
    for j in range(pp):
        page_id = pt_ref[req, n_pages - 1 - (step * pp + j)]
        lf = lf_refs[j][:, pl.ds(page_id % LF_PAGE_GROUP, 1), :].reshape(n_heads, page)
        decay.append(per_head_rows(_dot_exact(lf, suffix) + carry))
        carry = carry + jnp.sum(lf, axis=-1, keepdims=True)
    carry_ref[...] = carry
    attend([functools.partial(lambda hh, j: kbuf_ref[slot, j, hh], j=j) for j in range(pp)],
           [functools.partial(lambda hh, j: vbuf_ref[slot, j, hh], j=j) for j in range(pp)],
           cn_ref[...] + jnp.concatenate(decay, axis=1))

    @pl.when(step == pl.num_programs(1) - 1)
    def _():
        c_row = _dot_exact(lfnew_ref[0], (r <= c).astype(F32))
        key = lax.broadcasted_iota(jnp.int32, (n_rows, page), 1)
        tok = lax.broadcasted_iota(jnp.int32, (n_rows, page), 0) % Q_ROWS
        knp_ref[...] = jnp.zeros(knp_ref.shape, F32)
        vnp_ref[...] = jnp.zeros(vnp_ref.shape, F32)
        for hh in range(n_heads):
            knp_ref[0, hh, 0:Q_ROWS, :] = knew_ref[0, hh]
            vnp_ref[0, hh, 0:Q_ROWS, :] = vnew_ref[0, hh]
        attend([lambda hh: knp_ref[0, hh]], [lambda hh: vnp_ref[0, hh]], cn_ref[...] - per_head_rows(c_row),
               mask=key <= tok)
        o_ref[0] = acc_ref[...] / l_ref[...]


def _fox_sample(q_rows, k_new, v_new, lf_new, lf_new_col, k_pool, v_pool, lf_pool, page_table, n_new):
    n, n_heads = q_rows.shape[:2]
    page = k_pool.shape[2]
    n_pages = page_table.shape[1]
    pp = _pick(n_pages, (8, 4, 2, 1))
    n_steps = n_pages // pp
    n_rows = n_heads * Q_ROWS

    def lf_map(j):
        return lambda i, s, pt: (0, pt[i, n_pages - 1 - (s * pp + j)] // LF_PAGE_GROUP, 0)

    req3 = lambda i, s, pt: (i, 0, 0)
    req4 = lambda i, s, pt: (i, 0, 0, 0)
    kv_block = (1, n_heads, page, HEAD_DIM)
    new_block = pl.BlockSpec((1, n_heads, Q_ROWS, HEAD_DIM), req4)
    in_specs = [new_block, new_block, new_block,
                pl.BlockSpec((1, HEAD_PAD, page), req3),
                pl.BlockSpec((1, Q_ROWS, LANES), req3)]
    in_specs += [pl.BlockSpec(memory_space=pl.ANY), pl.BlockSpec(memory_space=pl.ANY)]
    in_specs += [pl.BlockSpec((n_heads, LF_PAGE_GROUP, page), lf_map(j)) for j in range(pp)]
    ring = (PAGE_RING, pp, n_heads, page, HEAD_DIM)
    grid_spec = pltpu.PrefetchScalarGridSpec(
        num_scalar_prefetch=1,
        grid=(n, n_steps),
        in_specs=in_specs,
        out_specs=pl.BlockSpec((1, n_rows, HEAD_DIM), req3),
        scratch_shapes=[pltpu.VMEM((n_rows, 1), F32), pltpu.VMEM((n_rows, 1), F32),
                        pltpu.VMEM((n_rows, HEAD_DIM), F32), pltpu.VMEM((n_heads, 1), F32),
                        pltpu.VMEM((n_rows, 1), F32), pltpu.VMEM(kv_block, F32), pltpu.VMEM(kv_block, F32),
                        pltpu.VMEM(ring, F32), pltpu.VMEM(ring, F32),
                        pltpu.SemaphoreType.DMA((PAGE_RING,)), pltpu.SemaphoreType.DMA((PAGE_RING,))],
    )
    return pl.pallas_call(
        functools.partial(_fox_sample_kernel, pages_per_step=pp, n_new=n_new, n_heads=n_heads),
        grid_spec=grid_spec,
        out_shape=jax.ShapeDtypeStruct((n, n_rows, HEAD_DIM), F32),
        compiler_params=_cparams("arbitrary", "arbitrary"),
        name="fox_sample",
    )(page_table, q_rows, k_new, v_new, lf_new, lf_new_col, k_pool, v_pool, *([lf_pool] * pp))


def kernel(x_prompt, x_sample, cache_fox_k, cache_fox_v, cache_fox_logf, state_conv, cache_mem_k, cache_mem_v,
           page_table, mem_prompt, g_mix, g_ffn, g_mem, w_mem_kv, w_in_a, w_out_a, conv_w, conv_b, conv_ln_g,
           conv_ln_b, g_kv, w_kvf, b_f, w_in_b, w_out_b, w_ff_gate, w_ff_up, w_ff_down, w_router, b_router,
           w_e_gate, w_e_up, w_e_down, g_final):
    bsz, seq, d = x_prompt.shape
    n_dec, t_dec, _ = x_sample.shape
    depth = g_mix.shape[0]
    n_a = w_in_a.shape[0]
    conv_ch = conv_w.shape[-1]
    width = conv_w.shape[1]
    n_mem = mem_prompt.shape[1]
    fox_w = w_in_b.shape[-1] - MEM_WIDTH
    n_heads = fox_w // HEAD_DIM
    n_pool, page = cache_fox_k.shape[:2]
    tp = bsz * seq
    ts = n_dec * t_dec
    assert n_heads <= HEAD_PAD and width - 1 <= CONV_HALO and t_dec <= 8

    h = jnp.concatenate([x_prompt.reshape(tp, d), x_sample.reshape(ts, d)], axis=0)

    mem_flat = mem_prompt.reshape(bsz * n_mem, d)
    mem_k_p, mem_v_p = [], []
    for i in range(depth):
        z = _matmul([_rmsnorm(mem_flat, g_mem[i], BF16)], w_mem_kv[i], [0], 0, 2 * MEM_WIDTH)
        mem_k_p.append(z[:, :MEM_WIDTH].reshape(bsz, n_mem, MEM_HEADS, HEAD_DIM).transpose(0, 2, 1, 3))
        mem_v_p.append(z[:, MEM_WIDTH:].reshape(bsz, n_mem, MEM_HEADS, HEAD_DIM).transpose(0, 2, 1, 3))

    def mem_attend(q_all, col_off, i):
        q_s = jnp.pad(q_all[tp:, col_off:col_off + MEM_WIDTH].reshape(n_dec, t_dec, MEM_WIDTH),
                      ((0, 0), (0, SAMPLE_Q_PAD - t_dec), (0, 0)))
        o_p = _mem_attn(q_all, mem_k_p[i], mem_v_p[i], seq, col_off)
        o_s = _mem_attn_cache(q_s.reshape(n_dec * SAMPLE_Q_PAD, MEM_WIDTH), cache_mem_k, cache_mem_v, i,
                              SAMPLE_Q_PAD)
        o_s = o_s.reshape(n_dec, SAMPLE_Q_PAD, MEM_WIDTH)[:, :t_dec].reshape(ts, MEM_WIDTH)
        return o_p.at[tp:].set(o_s)

    new_conv_p, new_conv_s = [], []
    kv = None
    for i in range(depth):
        if i == n_a:
            hn, hk = _rmsnorm_pair(h, g_mix[i], g_kv)
        else:
            hn = _rmsnorm(h, g_mix[i], BF16)
        if i < n_a:
            u = _matmul_glu(hn, w_in_a[i], conv_ch)
            q_all, q_mem_off = _matmul([hn], w_in_a[i], [0], 2 * conv_ch, MEM_WIDTH), 0
            u_s = u[tp:].reshape(n_dec, t_dec, conv_ch)
            ext_s = jnp.concatenate([state_conv[i], u_s], axis=1)
            keep = min(seq, width - 1)
            tail = jnp.stack([u[(b + 1) * seq - keep:(b + 1) * seq] for b in range(bsz)])
            new_conv_p.append(jnp.pad(tail, ((0, 0), (width - 1 - keep, 0), (0, 0))))
            new_conv_s.append(ext_s[:, t_dec:])
            mix_p = _conv_prompt(u, conv_w[i], conv_b[i], conv_ln_g[i], conv_ln_b[i], bsz, seq)
            mix_s = _conv_sample(ext_s.transpose(1, 0, 2), conv_w[i], conv_b[i], conv_ln_g[i], conv_ln_b[i])
            mix = mix_p.at[tp:].set(mix_s.transpose(1, 0, 2).reshape(ts, conv_ch))
            w_out = w_out_a[i]
        else:
            j = i - n_a
            if kv is None:
                k_p = _matmul([hk], w_kvf, [0], 0, fox_w, n_rows=tp, heads_of=(bsz, seq))
                v_p = _matmul([hk], w_kvf, [0], fox_w, fox_w, n_rows=tp, heads_of=(bsz, seq))
                kv_s = _matmul([hk], w_kvf, [0], 0, 2 * fox_w, row_start=tp, n_rows=ts)
                w_f = jnp.pad(w_kvf[:, 2 * fox_w:], ((0, 0), (0, LANES - n_heads)))
                b_f_pad = jnp.pad(b_f, (0, LANES - n_heads)).reshape(1, LANES)
                logf_pad = _matmul([hk], w_f, [0], 0, LANES, epilogue="logsig", extra=b_f_pad)
                crep, crow = _fox_cumsum(logf_pad[:tp].reshape(bsz, seq, LANES), n_heads)
                def new_rows(x):
                    x = x.reshape(n_dec, t_dec, n_heads, HEAD_DIM).transpose(0, 2, 1, 3)
                    return jnp.pad(x, ((0, 0), (0, 0), (0, Q_ROWS - t_dec), (0, 0)))

                k_new = new_rows(kv_s[:, :fox_w])
                v_new = new_rows(kv_s[:, fox_w:])
                lf_new = jnp.pad(logf_pad[tp:, :n_heads].reshape(n_dec, t_dec, n_heads).transpose(0, 2, 1),
                                 ((0, 0), (0, HEAD_PAD - n_heads), (0, page - t_dec)))
                lf_new_col = jnp.pad(logf_pad[tp:].reshape(n_dec, t_dec, LANES),
                                     ((0, 0), (0, Q_ROWS - t_dec), (0, 0)))
                lf_pool = cache_fox_logf.astype(F32).transpose(2, 0, 1)
                kv = (k_p, v_p, kv_s, logf_pad)
            q_all, q_mem_off = _matmul([hn], w_in_b[j], [0], 0, fox_w + MEM_WIDTH), fox_w
            mix_p = _fox_prompt(q_all, k_p, v_p, crep, crow)
            q_rows = jnp.pad(q_all[tp:, :fox_w].reshape(n_dec, t_dec, n_heads, HEAD_DIM).transpose(0, 2, 1, 3),
                             ((0, 0), (0, 0), (0, Q_ROWS - t_dec), (0, 0))).astype(BF16)
            o_rows = _fox_sample(q_rows, k_new, v_new, lf_new, lf_new_col, cache_fox_k.transpose(0, 2, 1, 3),
                                 cache_fox_v.transpose(0, 2, 1, 3), lf_pool, page_table, t_dec)
            mix_s = o_rows.reshape(n_dec, n_heads, Q_ROWS, HEAD_DIM)[:, :, :t_dec].transpose(0, 2, 1, 3)
            mix = mix_p.at[tp:].set(mix_s.reshape(ts, fox_w).astype(BF16))
            w_out = w_out_b[j]
        mo = mem_attend(q_all, q_mem_off, i)
        h = _matmul([mix, mo], w_out, [0, mix.shape[1]], 0, d, epilogue="resid", extra=h)
        jj = i // 2
        if i % 2 == 0:
            h = _ffn(h, g_ffn[i], w_ff_gate[jj], w_ff_up[jj], w_ff_down[jj])
        else:
            hf = _rmsnorm(h, g_ffn[i], BF16)
            h = _moe(hf, h, w_router[jj], b_router[jj], w_e_gate[jj], w_e_up[jj], w_e_down[jj])

    y_p = _rmsnorm(h, g_final, F32, 0, tp)
    y_s = _rmsnorm(h, g_final, F32, tp, ts)
    k_p, v_p, kv_s, logf_pad = kv
    logf = logf_pad[:, :n_heads]
    return (y_p.reshape(bsz, seq, d),
            y_s.reshape(n_dec, t_dec, d),
            k_p.transpose(0, 2, 1, 3),
            v_p.transpose(0, 2, 1, 3),
            logf[:tp].reshape(bsz, seq, n_heads),
            jnp.stack(new_conv_p),
            jnp.stack(mem_k_p).transpose(0, 1, 3, 2, 4),
            jnp.stack(mem_v_p).transpose(0, 1, 3, 2, 4),
            kv_s[:, :fox_w].reshape(n_dec, t_dec, n_heads, HEAD_DIM),
            kv_s[:, fox_w:].reshape(n_dec, t_dec, n_heads, HEAD_DIM),
            logf[tp:].reshape(n_dec, t_dec, n_heads),
            jnp.stack(new_conv_s))
```
